```python
import math
import jax, jax.numpy as jnp
from jax import lax
import numpy as np


D_MODEL = 2048
BATCH = 2
SEQ = 4096
DEPTH = 1
DEC_BATCH = 4
DEC_SEQ = 4096
PAST_LEN = 128

GLA_HEADS = 4
GLA_DK = 128
GLA_DV = 256
GLA_GATE_RANK = 16
GLA_TAU = 16.0
GLA_CHUNK = 64
MLA_HEADS = 8
MLA_Q_RANK = 512
MLA_KV_RANK = 512
MLA_NOPE = 128
MLA_ROPE = 64
MLA_V = 128
ROPE_BASE = 10000.0
Q_BLOCK = 128
MEM_TOKENS = 256
MEM_HEADS = 4
MEM_HEAD_DIM = D_MODEL // MEM_HEADS
D_FF = 5632
CONV_WIDTH = 3
EPS = 1e-6

GLA_WIDTH = GLA_HEADS * GLA_DV
MLA_WIDTH = MLA_HEADS * MLA_V
MIX_WIDTH = GLA_WIDTH + MLA_WIDTH
IN_SPLITS = (GLA_HEADS * GLA_DK, GLA_HEADS * GLA_DK, GLA_WIDTH, GLA_WIDTH, GLA_GATE_RANK, GLA_GATE_RANK, MLA_Q_RANK, MLA_KV_RANK, MLA_ROPE)
IN_WIDTH = sum(IN_SPLITS)

kernel_name = 'hymba_gla_mla_memxattn_convffn_bidir_encoder'


def rmsnorm(x, g):
    xf = x.astype(jnp.float32)
    y = xf * lax.rsqrt(jnp.mean(xf * xf, axis=-1, keepdims=True) + EPS)
    return (y * g.astype(jnp.float32)).astype(x.dtype)


def split_offsets():
    offs, acc = [], 0
    for w in IN_SPLITS[:-1]:
        acc += w
        offs.append(acc)
    return offs


def rope(x, positions):
    half = x.shape[-1] // 2
    freqs = ROPE_BASE ** (-jnp.arange(half, dtype=jnp.float32) / half)
    ang = positions[:, None] * freqs[None, :]
    cos = jnp.cos(ang)[None, :, None, :]
    sin = jnp.sin(ang)[None, :, None, :]
    xf = x.astype(jnp.float32)
    x1, x2 = xf[..., :half], xf[..., half:]
    out = jnp.concatenate([x1 * cos - x2 * sin, x1 * sin + x2 * cos], axis=-1)
    return out.astype(x.dtype)


def gla_direction(q, k, v, log_a, strict):
    B, S, H, DK = q.shape
    DV = v.shape[-1]
    C = GLA_CHUNK
    N = S // C

    def chunks(t):
        return t.reshape(B, N, C, H, t.shape[-1]).transpose(1, 0, 3, 2, 4)

    qc, kc, vc, la = chunks(q), chunks(k), chunks(v), chunks(log_a)
    b = jnp.cumsum(la, axis=3)
    ref = b[:, :, :, C // 2:C // 2 + 1]
    qi = qc * jnp.exp(b - ref)
    ki = kc * jnp.exp(ref - b)
    att = jnp.einsum('nbhid,nbhjd->nbhij', qi, ki)
    mask = jnp.tril(jnp.ones((C, C), dtype=bool), k=-1 if strict else 0)
    att = jnp.where(mask, att, 0.0)
    o = jnp.einsum('nbhij,nbhje->nbhie', att, vc)
    b_last = b[:, :, :, -1:]
    kv = jnp.einsum('nbhjd,nbhje->nbhde', kc * jnp.exp(b_last - b), vc)
    decay = jnp.exp(b_last[:, :, :, 0])

    def step(state, inp):
        d, kv_n = inp
        return d[..., None] * state + kv_n, state

    _, s_prev = lax.scan(step, jnp.zeros((B, H, DK, DV), jnp.float32), (decay, kv))
    o = o + jnp.einsum('nbhid,nbhde->nbhie', qc * jnp.exp(b), s_prev)
    return o.transpose(1, 0, 3, 2, 4).reshape(B, S, H, DV)


def gla_mixer(gq, gk, gv, gr, ggf, ggb, w2f, bf, w2b, bb, out_norm):
    B, S, _ = gq.shape
    f32 = jnp.float32
    q = gq.reshape(B, S, GLA_HEADS, GLA_DK).astype(f32) * (GLA_DK ** -0.5)
    k = gk.reshape(B, S, GLA_HEADS, GLA_DK).astype(f32)
    v = gv.reshape(B, S, GLA_HEADS, GLA_DV).astype(f32)
    la_f = (jax.nn.log_sigmoid((ggf @ w2f + bf).astype(f32)) / GLA_TAU).reshape(B, S, GLA_HEADS, GLA_DK)
    la_b = (jax.nn.log_sigmoid((ggb @ w2b + bb).astype(f32)) / GLA_TAU).reshape(B, S, GLA_HEADS, GLA_DK)
    o_f = gla_direction(q, k, v, la_f, False)
    flip = lambda t: jnp.flip(t, axis=1)
    o_b = flip(gla_direction(flip(q), flip(k), flip(v), flip(la_b), True))
    o = o_f + o_b
    o = o * lax.rsqrt(jnp.mean(o * o, axis=-1, keepdims=True) + EPS) * out_norm.astype(f32)
    return o.reshape(B, S, GLA_WIDTH).astype(gq.dtype) * jax.nn.silu(gr)


def mla_mixer(cq, ckv, kr, q_norm, w_q_up, kv_norm, w_kv_up):
    B, S, _ = cq.shape
    f32 = jnp.float32
    pos = jnp.arange(S, dtype=f32)
    q = (rmsnorm(cq, q_norm) @ w_q_up).reshape(B, S, MLA_HEADS, MLA_NOPE + MLA_ROPE)
    q = jnp.concatenate([q[..., :MLA_NOPE], rope(q[..., MLA_NOPE:], pos)], axis=-1)
    kv = (rmsnorm(ckv, kv_norm) @ w_kv_up).reshape(B, S, MLA_HEADS, MLA_NOPE + MLA_V)
    k_nope, v = kv[..., :MLA_NOPE], kv[..., MLA_NOPE:]
    k_pe = rope(kr.reshape(B, S, 1, MLA_ROPE), pos)
    k = jnp.concatenate([k_nope, jnp.broadcast_to(k_pe, (B, S, MLA_HEADS, MLA_ROPE))], axis=-1)
    scale = (MLA_NOPE + MLA_ROPE) ** -0.5
    nb = S // Q_BLOCK
    qb = (q.astype(f32) * scale).reshape(B, nb, Q_BLOCK, MLA_HEADS, MLA_NOPE + MLA_ROPE).transpose(1, 0, 2, 3, 4)
    kf = k.astype(f32)
    vf = v.astype(f32)

    def block(qblk):
        s = jnp.einsum('bqhd,bkhd->bhqk', qblk, kf)
        p = jax.nn.softmax(s, axis=-1)
        return jnp.einsum('bhqk,bkhd->bqhd', p, vf)

    o = lax.map(block, qb)
    return o.transpose(1, 0, 2, 3, 4).reshape(B, S, MLA_WIDTH).astype(cq.dtype)


def memory_xattn(h, m, wq, wk, wv, wo):
    B, S, _ = h.shape
    M = m.shape[1]
    f32 = jnp.float32
    q = (h @ wq).reshape(B, S, MEM_HEADS, MEM_HEAD_DIM).astype(f32)
    k = (m @ wk).reshape(B, M, MEM_HEADS, MEM_HEAD_DIM).astype(f32)
    v = (m @ wv).reshape(B, M, MEM_HEADS, MEM_HEAD_DIM).astype(f32)
    p = jax.nn.softmax(jnp.einsum('bqhd,bkhd->bhqk', q, k) * (MEM_HEAD_DIM ** -0.5), axis=-1)
    o = jnp.einsum('bhqk,bkhd->bqhd', p, v).reshape(B, S, D_MODEL).astype(h.dtype)
    return o @ wo


def conv_ffn(h, w_up, conv_w, conv_b, w_down):
    S = h.shape[1]
    u = h @ w_up
    pad = CONV_WIDTH // 2
    up = jnp.pad(u, ((0, 0), (pad, CONV_WIDTH - 1 - pad), (0, 0)))
    u = sum(up[:, j:j + S] * conv_w[j] for j in range(CONV_WIDTH)) + conv_b
    g, val = u[..., :D_FF], u[..., D_FF:]
    return (jax.nn.gelu(g, approximate=True) * val) @ w_down


def apply_layer(x, mem, lp):
    h = rmsnorm(x, lp['norm_mix_pre'])
    gq, gk, gv, gr, ggf, ggb, cq, ckv, kr = jnp.split(h @ lp['w_in'], split_offsets(), axis=-1)
    o_gla = gla_mixer(gq, gk, gv, gr, ggf, ggb, lp['gla_gate_w2_fwd'], lp['gla_gate_b_fwd'],
                      lp['gla_gate_w2_bwd'], lp['gla_gate_b_bwd'], lp['gla_out_norm'])
    o_mla = mla_mixer(cq, ckv, kr, lp['mla_q_norm'], lp['mla_w_q_up'], lp['mla_kv_norm'], lp['mla_w_kv_up'])
    y = jnp.concatenate([o_gla, o_mla], axis=-1) @ lp['w_out']
    x = x + rmsnorm(y, lp['norm_mix_post'])
    h = rmsnorm(x, lp['norm_mem_pre'])
    m = rmsnorm(mem, lp['mem_kv_norm'])
    y = memory_xattn(h, m, lp['w_mem_q'], lp['w_mem_k'], lp['w_mem_v'], lp['w_mem_o'])
    x = x + rmsnorm(y, lp['norm_mem_post'])
    h = rmsnorm(x, lp['norm_ffn_pre'])
    y = conv_ffn(h, lp['w_ffn_up'], lp['ffn_conv_w'], lp['ffn_conv_b'], lp['w_ffn_down'])
    return x + rmsnorm(y, lp['norm_ffn_post'])


def setup_inputs(seed: int = 0) -> dict:
    key = jax.random.key(seed)
    ks = jax.random.split(key, 40)
    f32 = jnp.float32

    def w(k, shape, fan_in):
        return jax.random.normal(k, shape, f32) * (fan_in ** -0.5)

    def gain(k, n):
        return 1.0 + 0.05 * jax.random.normal(k, (DEPTH, n), f32)

    def bias(k, shape, s):
        return s * jax.random.normal(k, shape, f32)

    L = DEPTH
    return {
        'x_prompt': jax.random.normal(ks[0], (BATCH, SEQ, D_MODEL), f32),
        'x_sample': jax.random.normal(ks[1], (DEC_BATCH, DEC_SEQ, D_MODEL), f32),
        'mem_prompt': jax.random.normal(ks[2], (BATCH, MEM_TOKENS, D_MODEL), f32),
        'mem_sample': jax.random.normal(ks[3], (DEC_BATCH, MEM_TOKENS, D_MODEL), f32),
        'norm_mix_pre': gain(ks[4], D_MODEL),
        'norm_mix_post': gain(ks[5], D_MODEL),
        'w_in': w(ks[6], (L, D_MODEL, IN_WIDTH), D_MODEL),
        'gla_gate_w2_fwd': w(ks[7], (L, GLA_GATE_RANK, GLA_HEADS * GLA_DK), GLA_GATE_RANK),
        'gla_gate_b_fwd': bias(ks[8], (L, GLA_HEADS * GLA_DK), 0.1),
        'gla_gate_w2_bwd': w(ks[9], (L, GLA_GATE_RANK, GLA_HEADS * GLA_DK), GLA_GATE_RANK),
        'gla_gate_b_bwd': bias(ks[10], (L, GLA_HEADS * GLA_DK), 0.1),
        'gla_out_norm': gain(ks[11], GLA_DV),
        'mla_q_norm': gain(ks[12], MLA_Q_RANK),
        'mla_w_q_up': w(ks[13], (L, MLA_Q_RANK, MLA_HEADS * (MLA_NOPE + MLA_ROPE)), MLA_Q_RANK),
        'mla_kv_norm': gain(ks[14], MLA_KV_RANK),
        'mla_w_kv_up': w(ks[15], (L, MLA_KV_RANK, MLA_HEADS * (MLA_NOPE + MLA_V)), MLA_KV_RANK),
        'w_out': w(ks[16], (L, MIX_WIDTH, D_MODEL), MIX_WIDTH),
        'norm_mem_pre': gain(ks[17], D_MODEL),
        'norm_mem_post': gain(ks[18], D_MODEL),
        'mem_kv_norm': gain(ks[19], D_MODEL),
        'w_mem_q': w(ks[20], (L, D_MODEL, D_MODEL), D_MODEL),
        'w_mem_k': w(ks[21], (L, D_MODEL, D_MODEL), D_MODEL),
        'w_mem_v': w(ks[22], (L, D_MODEL, D_MODEL), D_MODEL),
        'w_mem_o': w(ks[23], (L, D_MODEL, D_MODEL), D_MODEL),
        'norm_ffn_pre': gain(ks[24], D_MODEL),
        'norm_ffn_post': gain(ks[25], D_MODEL),
        'w_ffn_up': w(ks[26], (L, D_MODEL, 2 * D_FF), D_MODEL),
        'ffn_conv_w': w(ks[27], (L, CONV_WIDTH, 2 * D_FF), CONV_WIDTH),
        'ffn_conv_b': bias(ks[28], (L, 2 * D_FF), 0.01),
        'w_ffn_down': w(ks[29], (L, D_FF, D_MODEL), D_FF),
    }


def reference(x_prompt, x_sample, mem_prompt, mem_sample, norm_mix_pre, norm_mix_post, w_in,
              gla_gate_w2_fwd, gla_gate_b_fwd, gla_gate_w2_bwd, gla_gate_b_bwd, gla_out_norm,
              mla_q_norm, mla_w_q_up, mla_kv_norm, mla_w_kv_up, w_out,
              norm_mem_pre, norm_mem_post, mem_kv_norm, w_mem_q, w_mem_k, w_mem_v, w_mem_o,
              norm_ffn_pre, norm_ffn_post, w_ffn_up, ffn_conv_w, ffn_conv_b, w_ffn_down):
    yp, ys = x_prompt, x_sample
    for l in range(DEPTH):
        lp = {
            'norm_mix_pre': norm_mix_pre[l], 'norm_mix_post': norm_mix_post[l], 'w_in': w_in[l],
            'gla_gate_w2_fwd': gla_gate_w2_fwd[l], 'gla_gate_b_fwd': gla_gate_b_fwd[l],
            'gla_gate_w2_bwd': gla_gate_w2_bwd[l], 'gla_gate_b_bwd': gla_gate_b_bwd[l],
            'gla_out_norm': gla_out_norm[l],
            'mla_q_norm': mla_q_norm[l], 'mla_w_q_up': mla_w_q_up[l],
            'mla_kv_norm': mla_kv_norm[l], 'mla_w_kv_up': mla_w_kv_up[l], 'w_out': w_out[l],
            'norm_mem_pre': norm_mem_pre[l], 'norm_mem_post': norm_mem_post[l], 'mem_kv_norm': mem_kv_norm[l],
            'w_mem_q': w_mem_q[l], 'w_mem_k': w_mem_k[l], 'w_mem_v': w_mem_v[l], 'w_mem_o': w_mem_o[l],
            'norm_ffn_pre': norm_ffn_pre[l], 'norm_ffn_post': norm_ffn_post[l],
            'w_ffn_up': w_ffn_up[l], 'ffn_conv_w': ffn_conv_w[l], 'ffn_conv_b': ffn_conv_b[l],
            'w_ffn_down': w_ffn_down[l],
        }
        yp = apply_layer(yp, mem_prompt, lp)
        ys = apply_layer(ys, mem_sample, lp)
    return (yp, ys)
```

```python
import functools
import math

import jax
import jax.numpy as jnp
from jax import lax
from jax.experimental import pallas as pl
from jax.experimental.pallas import tpu as pltpu

F32 = jnp.float32
BF16 = jnp.bfloat16

EPS = 1e-6
D_MODEL = 2048
GLA_HEADS = 4
GLA_DK = 128
GLA_DV = 256
GLA_GATE_RANK = 16
GLA_TAU = 16.0
GLA_CHUNK = 64
GLA_QK = GLA_HEADS * GLA_DK
GLA_WIDTH = GLA_HEADS * GLA_DV
MLA_HEADS = 8
MLA_RANK = 512
MLA_NOPE = 128
MLA_ROPE = 64
MLA_V = 128
MLA_QK = MLA_NOPE + MLA_ROPE
MLA_WIDTH = MLA_HEADS * MLA_V
ROPE_BASE = 10000.0
MEM_HEADS = 4
MEM_HEAD_DIM = D_MODEL // MEM_HEADS
D_FF = 5632
CONV_WIDTH = 3

LANES = 128
SUBLANES = 8
BF16_ROWS = 16
VMEM_LIMIT_BYTES = 56 * 1024 * 1024
LOG2E = 1.4426950408889634

IN_GQ = 0
IN_GK = IN_GQ + GLA_QK
IN_GV = IN_GK + GLA_QK
IN_GR = IN_GV + GLA_WIDTH
IN_CQ = IN_GR + GLA_WIDTH
IN_CKV = IN_CQ + MLA_RANK
IN_GATE = IN_CKV + MLA_RANK
IN_KR = IN_GATE + LANES
IN_WIDTH_PADDED = IN_KR + LANES


def _cparams(*semantics):
    return pltpu.CompilerParams(dimension_semantics=semantics, vmem_limit_bytes=VMEM_LIMIT_BYTES)


def _resident(shape):
    zeros = (0,) * len(shape)
    return pl.BlockSpec(shape, lambda *_: zeros, pipeline_mode=pl.Buffered(1))


def _rms(x, g):
    return x * lax.rsqrt(jnp.mean(x * x, axis=-1, keepdims=True) + EPS) * g


def _dot(a, b):
    return jnp.dot(a, b, preferred_element_type=F32)


def _dot_nt(a, b):
    return lax.dot_general(a, b, (((1,), (1,)), ((), ())), preferred_element_type=F32)


def _dot_tn(a, b):
    return lax.dot_general(a, b, (((0,), (0,)), ((), ())), preferred_element_type=F32)


def _col_chunk(n):
    for c in (512, 256, LANES):
        if n % c == 0:
            return c
    raise ValueError(f"width {n} is not a multiple of {LANES}")


def _norm_matmul_body(x_ref, g_ref, w_ref, o_ref):
    h = _rms(x_ref[...], g_ref[...]).astype(BF16)
    n = o_ref.shape[-1]
    nc = _col_chunk(n)
    for c in range(0, n, nc):
        o_ref[:, c:c + nc] = _dot(h, w_ref[:, c:c + nc]).astype(o_ref.dtype)


def _norm_matmul(x, g, w, tm):
    t, d = x.shape
    n = w.shape[1]
    return pl.pallas_call(
        _norm_matmul_body,
        out_shape=jax.ShapeDtypeStruct((t, n), BF16),
        grid=(t // tm,),
        in_specs=[pl.BlockSpec((tm, d), lambda i: (i, 0)), _resident((1, d)), _resident((d, n))],
        out_specs=pl.BlockSpec((tm, n), lambda i: (i, 0)),
        compiler_params=_cparams("parallel"),
        name="norm_matmul",
    )(x, g, w)


def _log_sigmoid(x):
    return jnp.minimum(x, 0.0) - jnp.log1p(jnp.exp(-jnp.abs(x)))


def _gla_tables(reverse):
    c = GLA_CHUNK
    row = lax.broadcasted_iota(jnp.int32, (c, c), 0)
    col = lax.broadcasted_iota(jnp.int32, (c, c), 1)
    if reverse:
        cum = (col >= row)
        keep = col > row
    else:
        cum = (col <= row)
        keep = col <= row
    return jnp.where(cum, 1.0, 0.0).astype(BF16), keep


def _gla_log_decay(gate_ref, w2_ref, b2_ref, la_ref):
    logits = _dot(gate_ref[0], w2_ref[...]) + b2_ref[...]
    la_ref[...] = _log_sigmoid(logits) * (1.0 / GLA_TAU)


def _gla_chunk_head(q, k, v, la, state_t, cum_mat, keep, reverse):
    c = GLA_CHUNK
    la_hi = la.astype(BF16)
    la_lo = (la - la_hi.astype(F32)).astype(BF16)
    cum = _dot(cum_mat, la_hi) + _dot(cum_mat, la_lo)
    if reverse:
        mid = cum[c // 2 - 1:c // 2, :]
        last = cum[0:1, :]
    else:
        mid = cum[c // 2:c // 2 + 1, :]
        last = cum[c - 1:c, :]
    qs = q.astype(F32) * (GLA_DK ** -0.5)
    kf = k.astype(F32)
    qi = (qs * jnp.exp(cum - mid)).astype(BF16)
    ki = (kf * jnp.exp(mid - cum)).astype(BF16)
    att = jnp.where(keep, _dot_nt(qi, ki), 0.0).astype(BF16)
    o = _dot(att, v)
    o = o + _dot_nt((qs * jnp.exp(cum)).astype(BF16), state_t.astype(BF16))
    kd = (kf * jnp.exp(last - cum)).astype(BF16)
    new_state = state_t * jnp.exp(last) + _dot_tn(v, kd)
    return o, new_state


def _gla_fwd_body(q_ref, k_ref, v_ref, gate_ref, w2_ref, b2_ref, o_ref, state_ref, la_ref):
    @pl.when(pl.program_id(1) == 0)
    def _():
        state_ref[...] = jnp.zeros_like(state_ref)

    _gla_log_decay(gate_ref, w2_ref, b2_ref, la_ref)
    cum_mat, keep = _gla_tables(False)
    n_chunks = q_ref.shape[1] // GLA_CHUNK

    def chunk(ci, carry):
        rows = pl.ds(pl.multiple_of(ci * GLA_CHUNK, GLA_CHUNK), GLA_CHUNK)
        for h in range(GLA_HEADS):
            kcols = slice(h * GLA_DK, (h + 1) * GLA_DK)
            vcols = slice(h * GLA_DV, (h + 1) * GLA_DV)
            o, st = _gla_chunk_head(q_ref[0, rows, kcols], k_ref[0, rows, kcols], v_ref[0, rows, vcols],
                                    la_ref[rows, kcols], state_ref[h], cum_mat, keep, False)
            state_ref[h] = st
            o_ref[0, rows, vcols] = o
        return carry

    lax.fori_loop(0, n_chunks, chunk, 0)


def _gla_bwd_body(q_ref, k_ref, v_ref, gate_ref, w2_ref, b2_ref, of_ref, r_ref, gn_ref,
                  o_ref, state_ref, la_ref):
    @pl.when(pl.program_id(1) == 0)
    def _():
        state_ref[...] = jnp.zeros_like(state_ref)

    _gla_log_decay(gate_ref, w2_ref, b2_ref, la_ref)
    cum_mat, keep = _gla_tables(True)
    n_chunks = q_ref.shape[1] // GLA_CHUNK
    gn = gn_ref[...]

    def chunk(step, carry):
        ci = n_chunks - 1 - step
        rows = pl.ds(pl.multiple_of(ci * GLA_CHUNK, GLA_CHUNK), GLA_CHUNK)
        for h in range(GLA_HEADS):
            kcols = slice(h * GLA_DK, (h + 1) * GLA_DK)
            vcols = slice(h * GLA_DV, (h + 1) * GLA_DV)
            o, st = _gla_chunk_head(q_ref[0, rows, kcols], k_ref[0, rows, kcols], v_ref[0, rows, vcols],
                                    la_ref[rows, kcols], state_ref[h], cum_mat, keep, True)
            state_ref[h] = st
            o = o + of_ref[0, rows, vcols]
            r = r_ref[0, rows, vcols].astype(F32)
            silu = r * (1.0 / (1.0 + jnp.exp(-r)))
            o_ref[0, rows, vcols] = (_rms(o, gn) * silu).astype(o_ref.dtype)
        return carry

    lax.fori_loop(0, n_chunks, chunk, 0)


def _gla(proj, w2f, b2f, w2b, b2b, out_norm, ts):
    b, s, _ = proj.shape
    ns = s // ts

    def col_spec(width, offset, rev):
        blk = offset // width
        if rev:
            return pl.BlockSpec((1, ts, width), lambda bi, si: (bi, ns - 1 - si, blk))
        return pl.BlockSpec((1, ts, width), lambda bi, si: (bi, si, blk))

    def common_specs(rev):
        return [col_spec(GLA_QK, IN_GQ, rev), col_spec(GLA_QK, IN_GK, rev), col_spec(GLA_WIDTH, IN_GV, rev),
                col_spec(LANES, IN_GATE, rev), _resident((LANES, GLA_QK)), _resident((1, GLA_QK))]

    scratch = [pltpu.VMEM((GLA_HEADS, GLA_DV, GLA_DK), F32), pltpu.VMEM((ts, GLA_QK), F32)]
    o_fwd = pl.pallas_call(
        _gla_fwd_body,
        out_shape=jax.ShapeDtypeStruct((b, s, GLA_WIDTH), F32),
        grid=(b, ns),
        in_specs=common_specs(False),
        out_specs=pl.BlockSpec((1, ts, GLA_WIDTH), lambda bi, si: (bi, si, 0)),
        scratch_shapes=scratch,
        compiler_params=_cparams("parallel", "arbitrary"),
        name="gla_fwd",
    )(proj, proj, proj, proj, w2f, b2f)
    return pl.pallas_call(
        _gla_bwd_body,
        out_shape=jax.ShapeDtypeStruct((b, s, GLA_WIDTH), BF16),
        grid=(b, ns),
        in_specs=common_specs(True) + [
            pl.BlockSpec((1, ts, GLA_WIDTH), lambda bi, si: (bi, ns - 1 - si, 0)),
            col_spec(GLA_WIDTH, IN_GR, True),
            _resident((1, GLA_DV)),
        ],
        out_specs=pl.BlockSpec((1, ts, GLA_WIDTH), lambda bi, si: (bi, ns - 1 - si, 0)),
        scratch_shapes=scratch,
        compiler_params=_cparams("parallel", "arbitrary"),
        name="gla_bwd",
    )(proj, proj, proj, proj, w2b, b2b, o_fwd, proj, out_norm)


def _mla_prep_body(cq_ref, ckv_ref, kr_ref, qn_ref, kvn_ref, wq_nope_ref, wq_rope_ref, wq_rot_ref, wkv_ref,
                   cos_ref, sin_ref, q_out, k_out, v_out):
    hq = _rms(cq_ref[0].astype(F32), qn_ref[...]).astype(BF16)
    hkv = _rms(ckv_ref[0].astype(F32), kvn_ref[...]).astype(BF16)
    cos = cos_ref[...]
    sin = sin_ref[...]
    q_scale = MLA_QK ** -0.5 * LOG2E
    q_nope = _dot(hq, wq_nope_ref[...]) * q_scale
    q_pe = (_dot(hq, wq_rope_ref[...]) * cos + _dot(hq, wq_rot_ref[...]) * sin) * q_scale
    kv = _dot(hkv, wkv_ref[...])
    kr = kr_ref[0].astype(F32)
    k_pe = (kr[:, :MLA_ROPE] * cos[:, :MLA_ROPE] + kr[:, MLA_ROPE:] * sin[:, :MLA_ROPE]).astype(BF16)
    for h in range(MLA_HEADS):
        q_out[0, h, :, :MLA_NOPE] = q_nope[:, h * MLA_NOPE:(h + 1) * MLA_NOPE].astype(BF16)
        q_out[0, h, :, MLA_NOPE:] = q_pe[:, h * MLA_ROPE:(h + 1) * MLA_ROPE].astype(BF16)
        base = h * (MLA_NOPE + MLA_V)
        k_out[0, h, :, :MLA_NOPE] = kv[:, base:base + MLA_NOPE].astype(BF16)
        k_out[0, h, :, MLA_NOPE:] = k_pe
        v_out[0, h] = kv[:, base + MLA_NOPE:base + MLA_NOPE + MLA_V].astype(BF16)


def _mla_attn_body(q_ref, k_ref, v_ref, o_ref):
    s = _dot_nt(q_ref[0, 0], k_ref[0, 0])
    m = jnp.max(s, axis=-1, keepdims=True)
    p = jnp.exp2(s - m)
    l = jnp.sum(p, axis=-1, keepdims=True)
    o_ref[0] = (_dot(p.astype(BF16), v_ref[0, 0]) / l).astype(o_ref.dtype)


def _mla(proj, q_norm, kv_norm, wq_nope, wq_rope, wq_rot, wkv, cos_t, sin_t, ts, tq):
    b, s, _ = proj.shape
    rope_w = MLA_HEADS * MLA_ROPE
    q, k, v = pl.pallas_call(
        _mla_prep_body,
        out_shape=(jax.ShapeDtypeStruct((b, MLA_HEADS, s, MLA_QK), BF16),
                   jax.ShapeDtypeStruct((b, MLA_HEADS, s, MLA_QK), BF16),
                   jax.ShapeDtypeStruct((b, MLA_HEADS, s, MLA_V), BF16)),
        grid=(b, s // ts),
        in_specs=[
            pl.BlockSpec((1, ts, MLA_RANK), lambda bi, si: (bi, si, IN_CQ // MLA_RANK)),
            pl.BlockSpec((1, ts, MLA_RANK), lambda bi, si: (bi, si, IN_CKV // MLA_RANK)),
            pl.BlockSpec((1, ts, LANES), lambda bi, si: (bi, si, IN_KR // LANES)),
            _resident((1, MLA_RANK)), _resident((1, MLA_RANK)),
            _resident((MLA_RANK, MLA_HEADS * MLA_NOPE)), _resident((MLA_RANK, rope_w)),
            _resident((MLA_RANK, rope_w)), _resident((MLA_RANK, MLA_HEADS * (MLA_NOPE + MLA_V))),
            pl.BlockSpec((ts, rope_w), lambda bi, si: (si, 0)),
            pl.BlockSpec((ts, rope_w), lambda bi, si: (si, 0)),
        ],
        out_specs=(pl.BlockSpec((1, MLA_HEADS, ts, MLA_QK), lambda bi, si: (bi, 0, si, 0)),
                   pl.BlockSpec((1, MLA_HEADS, ts, MLA_QK), lambda bi, si: (bi, 0, si, 0)),
                   pl.BlockSpec((1, MLA_HEADS, ts, MLA_V), lambda bi, si: (bi, 0, si, 0))),
        compiler_params=_cparams("parallel", "parallel"),
        name="mla_prep",
    )(proj, proj, proj, q_norm, kv_norm, wq_nope, wq_rope, wq_rot, wkv, cos_t, sin_t)
    return pl.pallas_call(
        _mla_attn_body,
        out_shape=jax.ShapeDtypeStruct((b, s, MLA_WIDTH), BF16),
        grid=(b, MLA_HEADS, s // tq),
        in_specs=[
            pl.BlockSpec((1, 1, tq, MLA_QK), lambda bi, hi, qi: (bi, hi, qi, 0)),
            pl.BlockSpec((1, 1, s, MLA_QK), lambda bi, hi, qi: (bi, hi, 0, 0)),
            pl.BlockSpec((1, 1, s, MLA_V), lambda bi, hi, qi: (bi, hi, 0, 0)),
        ],
        out_specs=pl.BlockSpec((1, tq, MLA_V), lambda bi, hi, qi: (bi, qi, hi)),
        compiler_params=_cparams("parallel", "parallel", "parallel"),
        name="mla_attn",
    )(q, k, v)


def _residual_norm_store(x, o_ref, sumsq, g):
    scale = lax.rsqrt(sumsq * (1.0 / o_ref.shape[-1]) + EPS)
    o_ref[...] = x + o_ref[...] * scale * g


def _mix_out_body(a1_ref, a2_ref, w1_ref, w2_ref, x_ref, g_ref, o_ref):
    a1 = a1_ref[...]
    a2 = a2_ref[...]
    n = o_ref.shape[-1]
    nc = _col_chunk(n)
    sumsq = jnp.zeros((o_ref.shape[0], 1), F32)
    for c in range(0, n, nc):
        y = _dot(a1, w1_ref[:, c:c + nc]) + _dot(a2, w2_ref[:, c:c + nc])
        sumsq = sumsq + jnp.sum(y * y, axis=-1, keepdims=True)
        o_ref[:, c:c + nc] = y
    _residual_norm_store(x_ref[...], o_ref, sumsq, g_ref[...])


def _mix_out(o_gla, o_mla, w1, w2, x, g, tm):
    t, d = x.shape
    return pl.pallas_call(
        _mix_out_body,
        out_shape=jax.ShapeDtypeStruct((t, d), F32),
        grid=(t // tm,),
        in_specs=[
            pl.BlockSpec((tm, GLA_WIDTH), lambda i: (i, 0)),
            pl.BlockSpec((tm, MLA_WIDTH), lambda i: (i, 0)),
            _resident((GLA_WIDTH, d)), _resident((MLA_WIDTH, d)),
            pl.BlockSpec((tm, d), lambda i: (i, 0)),
            _resident((1, d)),
        ],
        out_specs=pl.BlockSpec((tm, d), lambda i: (i, 0)),
        compiler_params=_cparams("parallel"),
        name="mix_out",
    )(o_gla, o_mla, w1, w2, x, g)


def _mem_attn_body(x_ref, gpre_ref, wq_ref, kv_ref, wo_ref, gpost_ref, o_ref, a_ref):
    x = x_ref[0]
    h = _rms(x, gpre_ref[...]).astype(BF16)
    q_scale = MEM_HEAD_DIM ** -0.5 * LOG2E
    for hd in range(MEM_HEADS):
        cols = slice(hd * MEM_HEAD_DIM, (hd + 1) * MEM_HEAD_DIM)
        vcols = slice(D_MODEL + hd * MEM_HEAD_DIM, D_MODEL + (hd + 1) * MEM_HEAD_DIM)
        q = (_dot(h, wq_ref[:, cols]) * q_scale).astype(BF16)
        s = _dot_nt(q, kv_ref[0, :, cols])
        m = jnp.max(s, axis=-1, keepdims=True)
        p = jnp.exp2(s - m)
        l = jnp.sum(p, axis=-1, keepdims=True)
        a_ref[:, cols] = (_dot(p.astype(BF16), kv_ref[0, :, vcols]) / l).astype(BF16)
    a = a_ref[...]
    sumsq = jnp.zeros((x.shape[0], 1), F32)
    nc = MEM_HEAD_DIM
    for c in range(0, D_MODEL, nc):
        y = _dot(a, wo_ref[:, c:c + nc])
        sumsq = sumsq + jnp.sum(y * y, axis=-1, keepdims=True)
        o_ref[0, :, c:c + nc] = y
    _residual_norm_store(x, o_ref.at[0], sumsq, gpost_ref[...])


def _mem_attn(x, gpre, wq, kv, wo, gpost, tm):
    b, s, d = x.shape
    m = kv.shape[1]
    return pl.pallas_call(
        _mem_attn_body,
        out_shape=jax.ShapeDtypeStruct((b, s, d), F32),
        grid=(b, s // tm),
        in_specs=[
            pl.BlockSpec((1, tm, d), lambda bi, si: (bi, si, 0)),
            _resident((1, d)), _resident((d, d)),
            pl.BlockSpec((1, m, 2 * d), lambda bi, si: (bi, 0, 0)),
            _resident((d, d)), _resident((1, d)),
        ],
        out_specs=pl.BlockSpec((1, tm, d), lambda bi, si: (bi, si, 0)),
        scratch_shapes=[pltpu.VMEM((tm, d), BF16)],
        compiler_params=_cparams("parallel", "parallel"),
        name="mem_attn",
    )(x, gpre, wq, kv, wo, gpost)


def _gelu_tanh(x):
    c = math.sqrt(2.0 / math.pi)
    return 0.5 * x * (1.0 + jnp.tanh(c * (x + 0.044715 * (x * x * x))))


def _seq_conv(u, cw, cb, tm):
    main = u[:tm]
    row = lax.broadcasted_iota(jnp.int32, main.shape, 0)
    before = jnp.where(row == 0, u[tm + BF16_ROWS - 1:tm + BF16_ROWS], pltpu.roll(main, 1, 0))
    after = jnp.where(row == tm - 1, u[tm:tm + 1], pltpu.roll(main, tm - 1, 0))
    return before * cw[0:1] + main * cw[1:2] + after * cw[2:3] + cb


def _ffn_body(x_ref, xprev_ref, xnext_ref, gpre_ref, wg_ref, wv_ref, cwg_ref, cwv_ref, cbg_ref, cbv_ref,
              wd_ref, gpost_ref, o_ref, h_ref, acc_ref):
    si = pl.program_id(1)
    j = pl.program_id(2)
    tm = x_ref.shape[1]

    @pl.when(j == 0)
    def _():
        g = gpre_ref[...]
        h_ref[:tm] = _rms(x_ref[0], g).astype(BF16)
        nxt = jnp.where(si == pl.num_programs(1) - 1, 0.0, _rms(xnext_ref[0], g))
        prv = jnp.where(si == 0, 0.0, _rms(xprev_ref[0], g))
        h_ref[tm:] = jnp.concatenate([nxt, prv], axis=0).astype(BF16)
        acc_ref[...] = jnp.zeros_like(acc_ref)

    h = h_ref[...]
    cg = _seq_conv(_dot(h, wg_ref[...]), cwg_ref[...], cbg_ref[...], tm)
    cv = _seq_conv(_dot(h, wv_ref[...]), cwv_ref[...], cbv_ref[...], tm)
    act = (_gelu_tanh(cg) * cv).astype(BF16)
    acc_ref[...] += _dot(act, wd_ref[...])

    @pl.when(j == pl.num_programs(2) - 1)
    def _():
        y = acc_ref[...]
        o_ref[0] = x_ref[0] + _rms(y, gpost_ref[...])


def _ffn(x, gpre, w_up, conv_w, conv_b, w_down, gpost, tm, fc):
    b, s, d = x.shape
    nf = D_FF // fc
    halo = SUBLANES
    tb = tm // halo
    last_blk = s // halo - 1
    return pl.pallas_call(
        _ffn_body,
        out_shape=jax.ShapeDtypeStruct((b, s, d), F32),
        grid=(b, s // tm, nf),
        in_specs=[
            pl.BlockSpec((1, tm, d), lambda bi, si, j: (bi, si, 0)),
            pl.BlockSpec((1, halo, d), lambda bi, si, j: (bi, jnp.maximum(si * tb - 1, 0), 0)),
            pl.BlockSpec((1, halo, d), lambda bi, si, j: (bi, jnp.minimum((si + 1) * tb, last_blk), 0)),
            _resident((1, d)),
            pl.BlockSpec((d, fc), lambda bi, si, j: (0, j)),
            pl.BlockSpec((d, fc), lambda bi, si, j: (0, nf + j)),
            pl.BlockSpec((CONV_WIDTH, fc), lambda bi, si, j: (0, j)),
            pl.BlockSpec((CONV_WIDTH, fc), lambda bi, si, j: (0, nf + j)),
            pl.BlockSpec((1, fc), lambda bi, si, j: (0, j)),
            pl.BlockSpec((1, fc), lambda bi, si, j: (0, nf + j)),
            pl.BlockSpec((fc, d), lambda bi, si, j: (j, 0)),
            _resident((1, d)),
        ],
        out_specs=pl.BlockSpec((1, tm, d), lambda bi, si, j: (bi, si, 0)),
        scratch_shapes=[pltpu.VMEM((tm + BF16_ROWS, d), BF16), pltpu.VMEM((tm, d), F32)],
        compiler_params=_cparams("parallel", "parallel", "arbitrary"),
        name="conv_ffn",
    )(x, x, x, gpre, w_up, w_up, conv_w, conv_w, conv_b, conv_b, w_down, gpost)


def _rotate_half_columns(w):
    k, n = w.shape
    g = w.reshape(k, n // MLA_ROPE, 2, MLA_ROPE // 2)
    return jnp.concatenate([-g[:, :, 1:], g[:, :, :1]], axis=2).reshape(k, n)


def _prepare_weights(p, seq):
    row = lambda v: v.reshape(1, -1).astype(F32)
    w_in = p["w_in"]
    offs, acc = [], 0
    for wdt in (GLA_QK, GLA_QK, GLA_WIDTH, GLA_WIDTH, GLA_GATE_RANK, GLA_GATE_RANK, MLA_RANK, MLA_RANK, MLA_ROPE):
        offs.append((acc, acc + wdt))
        acc += wdt
    part = [w_in[:, a:b] for a, b in offs]
    gq, gk, gv, gr, ggf, ggb, cq, ckv, kr = part
    d = w_in.shape[0]
    gate_pad = jnp.zeros((d, LANES - 2 * GLA_GATE_RANK), w_in.dtype)
    w_in_p = jnp.concatenate([gq, gk, gv, gr, cq, ckv, ggf, ggb, gate_pad, kr, _rotate_half_columns(kr)],
                             axis=1).astype(BF16)

    def gate_w2(w2, slot):
        full = jnp.zeros((LANES, GLA_QK), F32)
        return full.at[slot * GLA_GATE_RANK:(slot + 1) * GLA_GATE_RANK].set(w2).astype(BF16)

    wq = p["mla_w_q_up"].reshape(MLA_RANK, MLA_HEADS, MLA_QK)
    wq_nope = wq[:, :, :MLA_NOPE].reshape(MLA_RANK, MLA_HEADS * MLA_NOPE)
    wq_rope = wq[:, :, MLA_NOPE:].reshape(MLA_RANK, MLA_HEADS * MLA_ROPE)

    half = MLA_ROPE // 2
    freqs = ROPE_BASE ** (-jnp.arange(half, dtype=F32) / half)
    ang = jnp.arange(seq, dtype=F32)[:, None] * freqs[None, :]
    cos_t = jnp.tile(jnp.cos(ang), (1, 2 * MLA_HEADS))
    sin_t = jnp.tile(jnp.sin(ang), (1, 2 * MLA_HEADS))

    return dict(
        g_mix_pre=row(p["norm_mix_pre"]), g_mix_post=row(p["norm_mix_post"]), w_in=w_in_p,
        w2f=gate_w2(p["gla_gate_w2_fwd"], 0), b2f=row(p["gla_gate_b_fwd"]),
        w2b=gate_w2(p["gla_gate_w2_bwd"], 1), b2b=row(p["gla_gate_b_bwd"]),
        gla_out_norm=row(p["gla_out_norm"]),
        mla_q_norm=row(p["mla_q_norm"]), mla_kv_norm=row(p["mla_kv_norm"]),
        wq_nope=wq_nope.astype(BF16), wq_rope=wq_rope.astype(BF16),
        wq_rot=_rotate_half_columns(wq_rope).astype(BF16), wkv=p["mla_w_kv_up"].astype(BF16),
        cos=cos_t, sin=sin_t,
        w_out_gla=p["w_out"][:GLA_WIDTH].astype(BF16), w_out_mla=p["w_out"][GLA_WIDTH:].astype(BF16),
        g_mem_pre=row(p["norm_mem_pre"]), g_mem_post=row(p["norm_mem_post"]), g_mem_kv=row(p["mem_kv_norm"]),
        w_mem_q=p["w_mem_q"].astype(BF16), w_mem_o=p["w_mem_o"].astype(BF16),
        w_mem_kv=jnp.concatenate([p["w_mem_k"], p["w_mem_v"]], axis=1).astype(BF16),
        g_ffn_pre=row(p["norm_ffn_pre"]), g_ffn_post=row(p["norm_ffn_post"]),
        w_ffn_up=p["w_ffn_up"].astype(BF16), ffn_conv_w=p["ffn_conv_w"].astype(F32),
        ffn_conv_b=row(p["ffn_conv_b"]), w_ffn_down=p["w_ffn_down"].astype(BF16),
    )


def _tiles(seq):
    tm = min(512, seq)
    return dict(tm=tm, gla_ts=tm, mla_ts=tm, mla_tq=min(256, seq), ffn_fc=512)


def _apply_layer(x, mem, w):
    b, s, d = x.shape
    t = _tiles(s)
    tm = t["tm"]
    x2 = x.reshape(b * s, d)
    proj = _norm_matmul(x2, w["g_mix_pre"], w["w_in"], tm).reshape(b, s, IN_WIDTH_PADDED)
    o_gla = _gla(proj, w["w2f"], w["b2f"], w["w2b"], w["b2b"], w["gla_out_norm"], t["gla_ts"])
    o_mla = _mla(proj, w["mla_q_norm"], w["mla_kv_norm"], w["wq_nope"], w["wq_rope"], w["wq_rot"], w["wkv"],
                 w["cos"], w["sin"], t["mla_ts"], t["mla_tq"])
    x2 = _mix_out(o_gla.reshape(b * s, GLA_WIDTH), o_mla.reshape(b * s, MLA_WIDTH),
                  w["w_out_gla"], w["w_out_mla"], x2, w["g_mix_post"], tm)
    mtok = mem.shape[1]
    kv = _norm_matmul(mem.reshape(b * mtok, d), w["g_mem_kv"], w["w_mem_kv"], min(tm, b * mtok))
    x3 = _mem_attn(x2.reshape(b, s, d), w["g_mem_pre"], w["w_mem_q"], kv.reshape(b, mtok, 2 * d),
                   w["w_mem_o"], w["g_mem_post"], tm)
    return _ffn(x3, w["g_ffn_pre"], w["w_ffn_up"], w["ffn_conv_w"], w["ffn_conv_b"], w["w_ffn_down"],
                w["g_ffn_post"], tm, t["ffn_fc"])


def kernel(x_prompt, x_sample, mem_prompt, mem_sample, norm_mix_pre, norm_mix_post, w_in, gla_gate_w2_fwd, gla_gate_b_fwd, gla_gate_w2_bwd, gla_gate_b_bwd, gla_out_norm, mla_q_norm, mla_w_q_up, mla_kv_norm, mla_w_kv_up, w_out, norm_mem_pre, norm_mem_post, mem_kv_norm, w_mem_q, w_mem_k, w_mem_v, w_mem_o, norm_ffn_pre, norm_ffn_post, w_ffn_up, ffn_conv_w, ffn_conv_b, w_ffn_down):
    params = dict(
        norm_mix_pre=norm_mix_pre, norm_mix_post=norm_mix_post, w_in=w_in,
        gla_gate_w2_fwd=gla_gate_w2_fwd, gla_gate_b_fwd=gla_gate_b_fwd,
        gla_gate_w2_bwd=gla_gate_w2_bwd, gla_gate_b_bwd=gla_gate_b_bwd, gla_out_norm=gla_out_norm,
        mla_q_norm=mla_q_norm, mla_w_q_up=mla_w_q_up, mla_kv_norm=mla_kv_norm, mla_w_kv_up=mla_w_kv_up,
        w_out=w_out, norm_mem_pre=norm_mem_pre, norm_mem_post=norm_mem_post, mem_kv_norm=mem_kv_norm,
        w_mem_q=w_mem_q, w_mem_k=w_mem_k, w_mem_v=w_mem_v, w_mem_o=w_mem_o,
        norm_ffn_pre=norm_ffn_pre, norm_ffn_post=norm_ffn_post, w_ffn_up=w_ffn_up,
        ffn_conv_w=ffn_conv_w, ffn_conv_b=ffn_conv_b, w_ffn_down=w_ffn_down,
    )
    depth = w_in.shape[0]
    yp, ys = x_prompt, x_sample
    for layer in range(depth):
        p = {k: v[layer] for k, v in params.items()}
        assert yp.shape[1] == ys.shape[1]
        w = _prepare_weights(p, yp.shape[1])
        yp = _apply_layer(yp, mem_prompt, w)
        ys = _apply_layer(ys, mem_sample, w)
    return (yp, ys)
```

```python
import functools
import math

import jax
import jax.numpy as jnp
from jax import lax
from jax.experimental import pallas as pl
from jax.experimental.pallas import tpu as pltpu

F32 = jnp.float32
BF16 = jnp.bfloat16

EPS = 1e-6
D_MODEL = 2048
GLA_HEADS = 4
GLA_DK = 128
GLA_DV = 256
GLA_GATE_RANK = 16
GLA_TAU = 16.0
GLA_CHUNK = 64
GLA_QK = GLA_HEADS * GLA_DK
GLA_WIDTH = GLA_HEADS * GLA_DV
MLA_HEADS = 8
MLA_RANK = 512
MLA_NOPE = 128
MLA_ROPE = 64
MLA_V = 128
MLA_QK = MLA_NOPE + MLA_ROPE
MLA_WIDTH = MLA_HEADS * MLA_V
ROPE_BASE = 10000.0
MEM_HEADS = 4
MEM_HEAD_DIM = D_MODEL // MEM_HEADS
D_FF = 5632
CONV_WIDTH = 3
FFN_SUB = 256

LANES = 128
SUBLANES = 8
BF16_ROWS = 16
VMEM_LIMIT_BYTES = 56 * 1024 * 1024
LOG2E = 1.4426950408889634

IN_GQ = 0
IN_GK = IN_GQ + GLA_QK
IN_GV = IN_GK + GLA_QK
IN_GR = IN_GV + GLA_WIDTH
IN_CQ = IN_GR + GLA_WIDTH
IN_CKV = IN_CQ + MLA_RANK
IN_GATE = IN_CKV + MLA_RANK
IN_KR = IN_GATE + LANES
IN_WIDTH_PADDED = IN_KR + LANES


def _cparams(*semantics):
    return pltpu.CompilerParams(dimension_semantics=semantics, vmem_limit_bytes=VMEM_LIMIT_BYTES)


def _resident(shape):
    zeros = (0,) * len(shape)
    return pl.BlockSpec(shape, lambda *_: zeros, pipeline_mode=pl.Buffered(1))


def _rms(x, g):
    return x * lax.rsqrt(jnp.mean(x * x, axis=-1, keepdims=True) + EPS) * g


def _dot(a, b):
    return jnp.dot(a, b, preferred_element_type=F32)


def _dot_nt(a, b):
    return lax.dot_general(a, b, (((1,), (1,)), ((), ())), preferred_element_type=F32)


def _dot_tn(a, b):
    return lax.dot_general(a, b, (((0,), (0,)), ((), ())), preferred_element_type=F32)


def _col_chunk(n):
    for c in (512, 256, LANES):
        if n % c == 0:
            return c
    raise ValueError(f"width {n} is not a multiple of {LANES}")


def _norm_matmul_body(x_ref, g_ref, w_ref, o_ref):
    h = _rms(x_ref[...], g_ref[...]).astype(BF16)
    n = o_ref.shape[-1]
    nc = _col_chunk(n)
    for c in range(0, n, nc):
        o_ref[:, c:c + nc] = _dot(h, w_ref[:, c:c + nc]).astype(o_ref.dtype)


def _norm_matmul(x, g, w, tm):
    t, d = x.shape
    n = w.shape[1]
    return pl.pallas_call(
        _norm_matmul_body,
        out_shape=jax.ShapeDtypeStruct((t, n), BF16),
        grid=(t // tm,),
        in_specs=[pl.BlockSpec((tm, d), lambda i: (i, 0)), _resident((1, d)), _resident((d, n))],
        out_specs=pl.BlockSpec((tm, n), lambda i: (i, 0)),
        compiler_params=_cparams("parallel"),
        name="norm_matmul",
    )(x, g, w)


def _log_sigmoid(x):
    return jnp.minimum(x, 0.0) - jnp.log1p(jnp.exp(-jnp.abs(x)))


def _gla_tables(reverse):
    c = GLA_CHUNK
    row = lax.broadcasted_iota(jnp.int32, (c, c), 0)
    col = lax.broadcasted_iota(jnp.int32, (c, c), 1)
    if reverse:
        cum = (col >= row)
        keep = col > row
    else:
        cum = (col <= row)
        keep = col <= row
    return jnp.where(cum, 1.0, 0.0).astype(BF16), keep


def _gla_log_decay(gate_ref, w2_ref, b2_ref, la_ref):
    logits = _dot(gate_ref[0], w2_ref[...]) + b2_ref[...]
    la_ref[...] = _log_sigmoid(logits) * (1.0 / GLA_TAU)


def _gla_chunk_head(q, k, v, la, state_t, cum_mat, keep, reverse):
    c = GLA_CHUNK
    la_hi = la.astype(BF16)
    la_lo = (la - la_hi.astype(F32)).astype(BF16)
    cum = _dot(cum_mat, la_hi) + _dot(cum_mat, la_lo)
    if reverse:
        mid = cum[c // 2 - 1:c // 2, :]
        last = cum[0:1, :]
    else:
        mid = cum[c // 2:c // 2 + 1, :]
        last = cum[c - 1:c, :]
    qs = q.astype(F32) * (GLA_DK ** -0.5)
    kf = k.astype(F32)
    qi = (qs * jnp.exp(cum - mid)).astype(BF16)
    ki = (kf * jnp.exp(mid - cum)).astype(BF16)
    att = jnp.where(keep, _dot_nt(qi, ki), 0.0).astype(BF16)
    o = _dot(att, v)
    o = o + _dot_nt((qs * jnp.exp(cum)).astype(BF16), state_t.astype(BF16))
    kd = (kf * jnp.exp(last - cum)).astype(BF16)
    new_state = state_t * jnp.exp(last) + _dot_tn(v, kd)
    return o, new_state


def _gla_fwd_body(q_ref, k_ref, v_ref, gate_ref, w2_ref, b2_ref, o_ref, state_ref, la_ref):
    @pl.when(pl.program_id(1) == 0)
    def _():
        state_ref[...] = jnp.zeros_like(state_ref)

    _gla_log_decay(gate_ref, w2_ref, b2_ref, la_ref)
    cum_mat, keep = _gla_tables(False)
    n_chunks = q_ref.shape[1] // GLA_CHUNK

    def chunk(ci, carry):
        rows = pl.ds(pl.multiple_of(ci * GLA_CHUNK, GLA_CHUNK), GLA_CHUNK)
        for h in range(GLA_HEADS):
            kcols = slice(h * GLA_DK, (h + 1) * GLA_DK)
            vcols = slice(h * GLA_DV, (h + 1) * GLA_DV)
            o, st = _gla_chunk_head(q_ref[0, rows, kcols], k_ref[0, rows, kcols], v_ref[0, rows, vcols],
                                    la_ref[rows, kcols], state_ref[h], cum_mat, keep, False)
            state_ref[h] = st
            o_ref[0, rows, vcols] = o
        return carry

    lax.fori_loop(0, n_chunks, chunk, 0)


def _gla_bwd_body(q_ref, k_ref, v_ref, gate_ref, w2_ref, b2_ref, of_ref, r_ref, gn_ref,
                  o_ref, state_ref, la_ref):
    @pl.when(pl.program_id(1) == 0)
    def _():
        state_ref[...] = jnp.zeros_like(state_ref)

    _gla_log_decay(gate_ref, w2_ref, b2_ref, la_ref)
    cum_mat, keep = _gla_tables(True)
    n_chunks = q_ref.shape[1] // GLA_CHUNK
    gn = gn_ref[...]

    def chunk(step, carry):
        ci = n_chunks - 1 - step
        rows = pl.ds(pl.multiple_of(ci * GLA_CHUNK, GLA_CHUNK), GLA_CHUNK)
        for h in range(GLA_HEADS):
            kcols = slice(h * GLA_DK, (h + 1) * GLA_DK)
            vcols = slice(h * GLA_DV, (h + 1) * GLA_DV)
            o, st = _gla_chunk_head(q_ref[0, rows, kcols], k_ref[0, rows, kcols], v_ref[0, rows, vcols],
                                    la_ref[rows, kcols], state_ref[h], cum_mat, keep, True)
            state_ref[h] = st
            o = o + of_ref[0, rows, vcols]
            r = r_ref[0, rows, vcols].astype(F32)
            silu = r * (1.0 / (1.0 + jnp.exp(-r)))
            o_ref[0, rows, vcols] = (_rms(o, gn) * silu).astype(o_ref.dtype)
        return carry

    lax.fori_loop(0, n_chunks, chunk, 0)


def _gla(proj, w2f, b2f, w2b, b2b, out_norm, ts):
    b, s, _ = proj.shape
    ns = s // ts

    def col_spec(width, offset, rev):
        blk = offset // width
        if rev:
            return pl.BlockSpec((1, ts, width), lambda bi, si: (bi, ns - 1 - si, blk))
        return pl.BlockSpec((1, ts, width), lambda bi, si: (bi, si, blk))

    def common_specs(rev):
        return [col_spec(GLA_QK, IN_GQ, rev), col_spec(GLA_QK, IN_GK, rev), col_spec(GLA_WIDTH, IN_GV, rev),
                col_spec(LANES, IN_GATE, rev), _resident((LANES, GLA_QK)), _resident((1, GLA_QK))]

    scratch = [pltpu.VMEM((GLA_HEADS, GLA_DV, GLA_DK), F32), pltpu.VMEM((ts, GLA_QK), F32)]
    o_fwd = pl.pallas_call(
        _gla_fwd_body,
        out_shape=jax.ShapeDtypeStruct((b, s, GLA_WIDTH), F32),
        grid=(b, ns),
        in_specs=common_specs(False),
        out_specs=pl.BlockSpec((1, ts, GLA_WIDTH), lambda bi, si: (bi, si, 0)),
        scratch_shapes=scratch,
        compiler_params=_cparams("parallel", "arbitrary"),
        name="gla_fwd",
    )(proj, proj, proj, proj, w2f, b2f)
    return pl.pallas_call(
        _gla_bwd_body,
        out_shape=jax.ShapeDtypeStruct((b, s, GLA_WIDTH), BF16),
        grid=(b, ns),
        in_specs=common_specs(True) + [
            pl.BlockSpec((1, ts, GLA_WIDTH), lambda bi, si: (bi, ns - 1 - si, 0)),
            col_spec(GLA_WIDTH, IN_GR, True),
            _resident((1, GLA_DV)),
        ],
        out_specs=pl.BlockSpec((1, ts, GLA_WIDTH), lambda bi, si: (bi, ns - 1 - si, 0)),
        scratch_shapes=scratch,
        compiler_params=_cparams("parallel", "arbitrary"),
        name="gla_bwd",
    )(proj, proj, proj, proj, w2b, b2b, o_fwd, proj, out_norm)


def _mla_prep_body(cq_ref, ckv_ref, kr_ref, qn_ref, kvn_ref, wq_nope_ref, wq_rope_ref, wq_rot_ref, wkv_ref,
                   cos_ref, sin_ref, q_out, k_out, v_out):
    hq = _rms(cq_ref[0].astype(F32), qn_ref[...]).astype(BF16)
    hkv = _rms(ckv_ref[0].astype(F32), kvn_ref[...]).astype(BF16)
    cos = cos_ref[...]
    sin = sin_ref[...]
    q_scale = MLA_QK ** -0.5 * LOG2E
    q_nope = _dot(hq, wq_nope_ref[...]) * q_scale
    q_pe = (_dot(hq, wq_rope_ref[...]) * cos + _dot(hq, wq_rot_ref[...]) * sin) * q_scale
    kv = _dot(hkv, wkv_ref[...])
    kr = kr_ref[0].astype(F32)
    k_pe = (kr[:, :MLA_ROPE] * cos[:, :MLA_ROPE] + kr[:, MLA_ROPE:] * sin[:, :MLA_ROPE]).astype(BF16)
    for h in range(MLA_HEADS):
        q_out[0, h, :, :MLA_NOPE] = q_nope[:, h * MLA_NOPE:(h + 1) * MLA_NOPE].astype(BF16)
        q_out[0, h, :, MLA_NOPE:] = q_pe[:, h * MLA_ROPE:(h + 1) * MLA_ROPE].astype(BF16)
        base = h * (MLA_NOPE + MLA_V)
        k_out[0, h, :, :MLA_NOPE] = kv[:, base:base + MLA_NOPE].astype(BF16)
        k_out[0, h, :, MLA_NOPE:] = k_pe
        v_out[0, h] = kv[:, base + MLA_NOPE:base + MLA_NOPE + MLA_V].astype(BF16)


def _mla_attn_body(q_ref, k_ref, v_ref, o_ref):
    s = _dot_nt(q_ref[0, 0], k_ref[0, 0])
    m = jnp.max(s, axis=-1, keepdims=True)
    p = jnp.exp2(s - m)
    l = jnp.sum(p, axis=-1, keepdims=True)
    o_ref[0] = (_dot(p.astype(BF16), v_ref[0, 0]) / l).astype(o_ref.dtype)


def _mla(proj, q_norm, kv_norm, wq_nope, wq_rope, wq_rot, wkv, cos_t, sin_t, ts, tq):
    b, s, _ = proj.shape
    rope_w = MLA_HEADS * MLA_ROPE
    q, k, v = pl.pallas_call(
        _mla_prep_body,
        out_shape=(jax.ShapeDtypeStruct((b, MLA_HEADS, s, MLA_QK), BF16),
                   jax.ShapeDtypeStruct((b, MLA_HEADS, s, MLA_QK), BF16),
                   jax.ShapeDtypeStruct((b, MLA_HEADS, s, MLA_V), BF16)),
        grid=(b, s // ts),
        in_specs=[
            pl.BlockSpec((1, ts, MLA_RANK), lambda bi, si: (bi, si, IN_CQ // MLA_RANK)),
            pl.BlockSpec((1, ts, MLA_RANK), lambda bi, si: (bi, si, IN_CKV // MLA_RANK)),
            pl.BlockSpec((1, ts, LANES), lambda bi, si: (bi, si, IN_KR // LANES)),
            _resident((1, MLA_RANK)), _resident((1, MLA_RANK)),
            _resident((MLA_RANK, MLA_HEADS * MLA_NOPE)), _resident((MLA_RANK, rope_w)),
            _resident((MLA_RANK, rope_w)), _resident((MLA_RANK, MLA_HEADS * (MLA_NOPE + MLA_V))),
            pl.BlockSpec((ts, rope_w), lambda bi, si: (si, 0)),
            pl.BlockSpec((ts, rope_w), lambda bi, si: (si, 0)),
        ],
        out_specs=(pl.BlockSpec((1, MLA_HEADS, ts, MLA_QK), lambda bi, si: (bi, 0, si, 0)),
                   pl.BlockSpec((1, MLA_HEADS, ts, MLA_QK), lambda bi, si: (bi, 0, si, 0)),
                   pl.BlockSpec((1, MLA_HEADS, ts, MLA_V), lambda bi, si: (bi, 0, si, 0))),
        compiler_params=_cparams("parallel", "parallel"),
        name="mla_prep",
    )(proj, proj, proj, q_norm, kv_norm, wq_nope, wq_rope, wq_rot, wkv, cos_t, sin_t)
    return pl.pallas_call(
        _mla_attn_body,
        out_shape=jax.ShapeDtypeStruct((b, s, MLA_WIDTH), BF16),
        grid=(b, MLA_HEADS, s // tq),
        in_specs=[
            pl.BlockSpec((1, 1, tq, MLA_QK), lambda bi, hi, qi: (bi, hi, qi, 0)),
            pl.BlockSpec((1, 1, s, MLA_QK), lambda bi, hi, qi: (bi, hi, 0, 0)),
            pl.BlockSpec((1, 1, s, MLA_V), lambda bi, hi, qi: (bi, hi, 0, 0)),
        ],
        out_specs=pl.BlockSpec((1, tq, MLA_V), lambda bi, hi, qi: (bi, qi, hi)),
        compiler_params=_cparams("parallel", "parallel", "parallel"),
        name="mla_attn",
    )(q, k, v)


def _residual_norm_store(x, o_ref, sumsq, g):
    scale = lax.rsqrt(sumsq * (1.0 / o_ref.shape[-1]) + EPS)
    o_ref[...] = x + o_ref[...] * scale * g


def _mix_out_body(a1_ref, a2_ref, w1_ref, w2_ref, x_ref, g_ref, o_ref):
    a1 = a1_ref[...]
    a2 = a2_ref[...]
    n = o_ref.shape[-1]
    nc = _col_chunk(n)
    sumsq = jnp.zeros((o_ref.shape[0], 1), F32)
    for c in range(0, n, nc):
        y = _dot(a1, w1_ref[:, c:c + nc]) + _dot(a2, w2_ref[:, c:c + nc])
        sumsq = sumsq + jnp.sum(y * y, axis=-1, keepdims=True)
        o_ref[:, c:c + nc] = y
    _residual_norm_store(x_ref[...], o_ref, sumsq, g_ref[...])


def _mix_out(o_gla, o_mla, w1, w2, x, g, tm):
    t, d = x.shape
    return pl.pallas_call(
        _mix_out_body,
        out_shape=jax.ShapeDtypeStruct((t, d), F32),
        grid=(t // tm,),
        in_specs=[
            pl.BlockSpec((tm, GLA_WIDTH), lambda i: (i, 0)),
            pl.BlockSpec((tm, MLA_WIDTH), lambda i: (i, 0)),
            _resident((GLA_WIDTH, d)), _resident((MLA_WIDTH, d)),
            pl.BlockSpec((tm, d), lambda i: (i, 0)),
            _resident((1, d)),
        ],
        out_specs=pl.BlockSpec((tm, d), lambda i: (i, 0)),
        compiler_params=_cparams("parallel"),
        name="mix_out",
    )(o_gla, o_mla, w1, w2, x, g)


def _mem_attn_body(x_ref, gpre_ref, wq_ref, kv_ref, wo_ref, gpost_ref, o_ref, a_ref):
    x = x_ref[0]
    h = _rms(x, gpre_ref[...]).astype(BF16)
    q_scale = MEM_HEAD_DIM ** -0.5 * LOG2E
    for hd in range(MEM_HEADS):
        cols = slice(hd * MEM_HEAD_DIM, (hd + 1) * MEM_HEAD_DIM)
        vcols = slice(D_MODEL + hd * MEM_HEAD_DIM, D_MODEL + (hd + 1) * MEM_HEAD_DIM)
        q = (_dot(h, wq_ref[:, cols]) * q_scale).astype(BF16)
        s = _dot_nt(q, kv_ref[0, :, cols])
        m = jnp.max(s, axis=-1, keepdims=True)
        p = jnp.exp2(s - m)
        l = jnp.sum(p, axis=-1, keepdims=True)
        a_ref[:, cols] = (_dot(p.astype(BF16), kv_ref[0, :, vcols]) / l).astype(BF16)
    a = a_ref[...]
    sumsq = jnp.zeros((x.shape[0], 1), F32)
    nc = MEM_HEAD_DIM
    for c in range(0, D_MODEL, nc):
        y = _dot(a, wo_ref[:, c:c + nc])
        sumsq = sumsq + jnp.sum(y * y, axis=-1, keepdims=True)
        o_ref[0, :, c:c + nc] = y
    _residual_norm_store(x, o_ref.at[0], sumsq, gpost_ref[...])


def _mem_attn(x, gpre, wq, kv, wo, gpost, tm):
    b, s, d = x.shape
    m = kv.shape[1]
    return pl.pallas_call(
        _mem_attn_body,
        out_shape=jax.ShapeDtypeStruct((b, s, d), F32),
        grid=(b, s // tm),
        in_specs=[
            pl.BlockSpec((1, tm, d), lambda bi, si: (bi, si, 0)),
            _resident((1, d)), _resident((d, d)),
            pl.BlockSpec((1, m, 2 * d), lambda bi, si: (bi, 0, 0)),
            _resident((d, d)), _resident((1, d)),
        ],
        out_specs=pl.BlockSpec((1, tm, d), lambda bi, si: (bi, si, 0)),
        scratch_shapes=[pltpu.VMEM((tm, d), BF16)],
        compiler_params=_cparams("parallel", "parallel"),
        name="mem_attn",
    )(x, gpre, wq, kv, wo, gpost)


def _gelu_tanh_times(x, v_half):
    c = math.sqrt(2.0 / math.pi)
    inner = x * (c + (c * 0.044715) * (x * x))
    return x * (1.0 + jnp.tanh(inner)) * v_half


def _ffn_body(x_ref, xprev_ref, xnext_ref, gpre_ref, wu_ref, cw_ref, cb_ref, wd_ref, gpost_ref,
              o_ref, h_ref, acc_ref, u_ref):
    si = pl.program_id(1)
    j = pl.program_id(2)
    tm = x_ref.shape[1]
    sub = FFN_SUB
    hl = SUBLANES
    n_lane_tiles = 2 * sub // LANES

    @pl.when(j == 0)
    def _():
        g = gpre_ref[...]
        h_ref[:tm] = _rms(x_ref[0], g).astype(BF16)
        nxt = jnp.where(si == pl.num_programs(1) - 1, 0.0, _rms(xnext_ref[0], g))
        prv = jnp.where(si == 0, 0.0, _rms(xprev_ref[0], g))
        h_ref[tm:] = jnp.concatenate([nxt, prv], axis=0).astype(BF16)
        acc_ref[...] = jnp.zeros_like(acc_ref)

    h = h_ref[...]
    n_sub = wd_ref.shape[0] // sub
    cols = [slice(2 * c * sub, 2 * (c + 1) * sub) for c in range(n_sub)]

    def up_project(c):
        u = _dot(h, wu_ref[:, cols[c]])
        for k in range(n_lane_tiles):
            lanes = slice(k * LANES, (k + 1) * LANES)
            u_ref[c % 2, k, hl:hl + tm] = u[:tm, lanes]
            u_ref[c % 2, k, :hl] = u[tm + hl:, lanes]
            u_ref[c % 2, k, hl + tm:] = u[tm:tm + hl, lanes]

    def window(c, offset):
        return jnp.concatenate([u_ref[c % 2, k, hl + offset:hl + offset + tm] for k in range(n_lane_tiles)],
                               axis=1)

    def gate_and_down(c):
        cw = cw_ref[:, cols[c]]
        cu = window(c, -1) * cw[0:1] + window(c, 0) * cw[1:2] + window(c, 1) * cw[2:3] + cb_ref[:, cols[c]]
        act = _gelu_tanh_times(cu[:, :sub], cu[:, sub:]).astype(BF16)
        acc_ref[...] += _dot(act, wd_ref[c * sub:(c + 1) * sub, :])

    up_project(0)
    for c in range(n_sub):
        if c + 1 < n_sub:
            up_project(c + 1)
        gate_and_down(c)

    @pl.when(j == pl.num_programs(2) - 1)
    def _():
        o_ref[0] = x_ref[0] + _rms(acc_ref[...], gpost_ref[...])


def _ffn(x, gpre, w_up, conv_w, conv_b, w_down, gpost, tm, fc):
    b, s, d = x.shape
    nf = D_FF // fc
    halo = SUBLANES
    tb = tm // halo
    last_blk = s // halo - 1
    return pl.pallas_call(
        _ffn_body,
        out_shape=jax.ShapeDtypeStruct((b, s, d), F32),
        grid=(b, s // tm, nf),
        in_specs=[
            pl.BlockSpec((1, tm, d), lambda bi, si, j: (bi, si, 0)),
            pl.BlockSpec((1, halo, d), lambda bi, si, j: (bi, jnp.maximum(si * tb - 1, 0), 0)),
            pl.BlockSpec((1, halo, d), lambda bi, si, j: (bi, jnp.minimum((si + 1) * tb, last_blk), 0)),
            _resident((1, d)),
            pl.BlockSpec((d, 2 * fc), lambda bi, si, j: (0, j)),
            pl.BlockSpec((CONV_WIDTH, 2 * fc), lambda bi, si, j: (0, j)),
            pl.BlockSpec((1, 2 * fc), lambda bi, si, j: (0, j)),
            pl.BlockSpec((fc, d), lambda bi, si, j: (j, 0)),
            _resident((1, d)),
        ],
        out_specs=pl.BlockSpec((1, tm, d), lambda bi, si, j: (bi, si, 0)),
        scratch_shapes=[pltpu.VMEM((tm + 2 * halo, d), BF16), pltpu.VMEM((tm, d), F32),
                        pltpu.VMEM((2, 2 * FFN_SUB // LANES, tm + 2 * halo, LANES), F32)],
        compiler_params=_cparams("parallel", "parallel", "arbitrary"),
        name="conv_ffn",
    )(x, x, x, gpre, w_up, conv_w, conv_b, w_down, gpost)


def _rotate_half_columns(w):
    k, n = w.shape
    g = w.reshape(k, n // MLA_ROPE, 2, MLA_ROPE // 2)
    return jnp.concatenate([-g[:, :, 1:], g[:, :, :1]], axis=2).reshape(k, n)


def _interleave_gate_value(w):
    r = w.shape[0]
    return w.reshape(r, 2, D_FF // FFN_SUB, FFN_SUB).transpose(0, 2, 1, 3).reshape(r, 2 * D_FF)


def _prepare_weights(p, seq):
    row = lambda v: v.reshape(1, -1).astype(F32)
    w_in = p["w_in"]
    offs, acc = [], 0
    for wdt in (GLA_QK, GLA_QK, GLA_WIDTH, GLA_WIDTH, GLA_GATE_RANK, GLA_GATE_RANK, MLA_RANK, MLA_RANK, MLA_ROPE):
        offs.append((acc, acc + wdt))
        acc += wdt
    part = [w_in[:, a:b] for a, b in offs]
    gq, gk, gv, gr, ggf, ggb, cq, ckv, kr = part
    d = w_in.shape[0]
    gate_pad = jnp.zeros((d, LANES - 2 * GLA_GATE_RANK), w_in.dtype)
    w_in_p = jnp.concatenate([gq, gk, gv, gr, cq, ckv, ggf, ggb, gate_pad, kr, _rotate_half_columns(kr)],
                             axis=1).astype(BF16)

    def gate_w2(w2, slot):
        full = jnp.zeros((LANES, GLA_QK), F32)
        return full.at[slot * GLA_GATE_RANK:(slot + 1) * GLA_GATE_RANK].set(w2).astype(BF16)

    wq = p["mla_w_q_up"].reshape(MLA_RANK, MLA_HEADS, MLA_QK)
    wq_nope = wq[:, :, :MLA_NOPE].reshape(MLA_RANK, MLA_HEADS * MLA_NOPE)
    wq_rope = wq[:, :, MLA_NOPE:].reshape(MLA_RANK, MLA_HEADS * MLA_ROPE)

    half = MLA_ROPE // 2
    freqs = ROPE_BASE ** (-jnp.arange(half, dtype=F32) / half)
    ang = jnp.arange(seq, dtype=F32)[:, None] * freqs[None, :]
    cos_t = jnp.tile(jnp.cos(ang), (1, 2 * MLA_HEADS))
    sin_t = jnp.tile(jnp.sin(ang), (1, 2 * MLA_HEADS))

    value_half = jnp.concatenate([jnp.ones((1, D_FF), F32), jnp.full((1, D_FF), 0.5, F32)], axis=1)

    return dict(
        g_mix_pre=row(p["norm_mix_pre"]), g_mix_post=row(p["norm_mix_post"]), w_in=w_in_p,
        w2f=gate_w2(p["gla_gate_w2_fwd"], 0), b2f=row(p["gla_gate_b_fwd"]),
        w2b=gate_w2(p["gla_gate_w2_bwd"], 1), b2b=row(p["gla_gate_b_bwd"]),
        gla_out_norm=row(p["gla_out_norm"]),
        mla_q_norm=row(p["mla_q_norm"]), mla_kv_norm=row(p["mla_kv_norm"]),
        wq_nope=wq_nope.astype(BF16), wq_rope=wq_rope.astype(BF16),
        wq_rot=_rotate_half_columns(wq_rope).astype(BF16), wkv=p["mla_w_kv_up"].astype(BF16),
        cos=cos_t, sin=sin_t,
        w_out_gla=p["w_out"][:GLA_WIDTH].astype(BF16), w_out_mla=p["w_out"][GLA_WIDTH:].astype(BF16),
        g_mem_pre=row(p["norm_mem_pre"]), g_mem_post=row(p["norm_mem_post"]), g_mem_kv=row(p["mem_kv_norm"]),
        w_mem_q=p["w_mem_q"].astype(BF16), w_mem_o=p["w_mem_o"].astype(BF16),
        w_mem_kv=jnp.concatenate([p["w_mem_k"], p["w_mem_v"]], axis=1).astype(BF16),
        g_ffn_pre=row(p["norm_ffn_pre"]), g_ffn_post=row(p["norm_ffn_post"]),
        w_ffn_up=_interleave_gate_value(p["w_ffn_up"]).astype(BF16),
        ffn_conv_w=_interleave_gate_value(p["ffn_conv_w"].astype(F32) * value_half),
        ffn_conv_b=_interleave_gate_value(row(p["ffn_conv_b"]) * value_half),
        w_ffn_down=p["w_ffn_down"].astype(BF16),
    )


def _tiles(seq):
    tm = min(512, seq)
    return dict(tm=tm, gla_ts=tm, mla_ts=tm, mla_tq=min(256, seq), ffn_fc=512)


def _apply_layer(x, mem, w):
    b, s, d = x.shape
    t = _tiles(s)
    tm = t["tm"]
    x2 = x.reshape(b * s, d)
    proj = _norm_matmul(x2, w["g_mix_pre"], w["w_in"], tm).reshape(b, s, IN_WIDTH_PADDED)
    o_gla = _gla(proj, w["w2f"], w["b2f"], w["w2b"], w["b2b"], w["gla_out_norm"], t["gla_ts"])
    o_mla = _mla(proj, w["mla_q_norm"], w["mla_kv_norm"], w["wq_nope"], w["wq_rope"], w["wq_rot"], w["wkv"],
                 w["cos"], w["sin"], t["mla_ts"], t["mla_tq"])
    x2 = _mix_out(o_gla.reshape(b * s, GLA_WIDTH), o_mla.reshape(b * s, MLA_WIDTH),
                  w["w_out_gla"], w["w_out_mla"], x2, w["g_mix_post"], tm)
    mtok = mem.shape[1]
    kv = _norm_matmul(mem.reshape(b * mtok, d), w["g_mem_kv"], w["w_mem_kv"], min(tm, b * mtok))
    x3 = _mem_attn(x2.reshape(b, s, d), w["g_mem_pre"], w["w_mem_q"], kv.reshape(b, mtok, 2 * d),
                   w["w_mem_o"], w["g_mem_post"], tm)
    return _ffn(x3, w["g_ffn_pre"], w["w_ffn_up"], w["ffn_conv_w"], w["ffn_conv_b"], w["w_ffn_down"],
                w["g_ffn_post"], tm, t["ffn_fc"])


def kernel(x_prompt, x_sample, mem_prompt, mem_sample, norm_mix_pre, norm_mix_post, w_in, gla_gate_w2_fwd, gla_gate_b_fwd, gla_gate_w2_bwd, gla_gate_b_bwd, gla_out_norm, mla_q_norm, mla_w_q_up, mla_kv_norm, mla_w_kv_up, w_out, norm_mem_pre, norm_mem_post, mem_kv_norm, w_mem_q, w_mem_k, w_mem_v, w_mem_o, norm_ffn_pre, norm_ffn_post, w_ffn_up, ffn_conv_w, ffn_conv_b, w_ffn_down):
    params = dict(
        norm_mix_pre=norm_mix_pre, norm_mix_post=norm_mix_post, w_in=w_in,
        gla_gate_w2_fwd=gla_gate_w2_fwd, gla_gate_b_fwd=gla_gate_b_fwd,
        gla_gate_w2_bwd=gla_gate_w2_bwd, gla_gate_b_bwd=gla_gate_b_bwd, gla_out_norm=gla_out_norm,
        mla_q_norm=mla_q_norm, mla_w_q_up=mla_w_q_up, mla_kv_norm=mla_kv_norm, mla_w_kv_up=mla_w_kv_up,
        w_out=w_out, norm_mem_pre=norm_mem_pre, norm_mem_post=norm_mem_post, mem_kv_norm=mem_kv_norm,
        w_mem_q=w_mem_q, w_mem_k=w_mem_k, w_mem_v=w_mem_v, w_mem_o=w_mem_o,
        norm_ffn_pre=norm_ffn_pre, norm_ffn_post=norm_ffn_post, w_ffn_up=w_ffn_up,
        ffn_conv_w=ffn_conv_w, ffn_conv_b=ffn_conv_b, w_ffn_down=w_ffn_down,
    )
    depth = w_in.shape[0]
    yp, ys = x_prompt, x_sample
    for layer in range(depth):
        p = {k: v[layer] for k, v in params.items()}
        assert yp.shape[1] == ys.shape[1]
        w = _prepare_weights(p, yp.shape[1])
        yp = _apply_layer(yp, mem_prompt, w)
        ys = _apply_layer(ys, mem_sample, w)
    return (yp, ys)
```

```python
import functools
import math

import jax
import jax.numpy as jnp
from jax import lax
from jax.experimental import pallas as pl
from jax.experimental.pallas import tpu as pltpu

F32 = jnp.float32
BF16 = jnp.bfloat16

EPS = 1e-6
D_MODEL = 2048
GLA_HEADS = 4
GLA_DK = 128
GLA_DV = 256
GLA_GATE_RANK = 16
GLA_TAU = 16.0
GLA_CHUNK = 64
GLA_QK = GLA_HEADS * GLA_DK
GLA_WIDTH = GLA_HEADS * GLA_DV
MLA_HEADS = 8
MLA_RANK = 512
MLA_NOPE = 128
MLA_ROPE = 64
MLA_V = 128
MLA_QK = MLA_NOPE + MLA_ROPE
MLA_WIDTH = MLA_HEADS * MLA_V
ROPE_BASE = 10000.0
MEM_HEADS = 4
MEM_HEAD_DIM = D_MODEL // MEM_HEADS
D_FF = 5632
CONV_WIDTH = 3
FFN_SUB = 256

LANES = 128
SUBLANES = 8
BF16_ROWS = 16
VMEM_LIMIT_BYTES = 56 * 1024 * 1024
LOG2E = 1.4426950408889634

IN_GQ = 0
IN_GK = IN_GQ + GLA_QK
IN_GV = IN_GK + GLA_QK
IN_GR = IN_GV + GLA_WIDTH
IN_CQ = IN_GR + GLA_WIDTH
IN_CKV = IN_CQ + MLA_RANK
IN_GATE = IN_CKV + MLA_RANK
IN_KR = IN_GATE + LANES
IN_WIDTH_PADDED = IN_KR + LANES


def _cparams(*semantics):
    return pltpu.CompilerParams(dimension_semantics=semantics, vmem_limit_bytes=VMEM_LIMIT_BYTES)


def _resident(shape):
    zeros = (0,) * len(shape)
    return pl.BlockSpec(shape, lambda *_: zeros, pipeline_mode=pl.Buffered(1))


def _rms(x, g):
    return x * lax.rsqrt(jnp.mean(x * x, axis=-1, keepdims=True) + EPS) * g


def _dot(a, b):
    return jnp.dot(a, b, preferred_element_type=F32)


def _dot_nt(a, b):
    return lax.dot_general(a, b, (((1,), (1,)), ((), ())), preferred_element_type=F32)


def _dot_tn(a, b):
    return lax.dot_general(a, b, (((0,), (0,)), ((), ())), preferred_element_type=F32)


def _col_chunk(n):
    for c in (512, 256, LANES):
        if n % c == 0:
            return c
    raise ValueError(f"width {n} is not a multiple of {LANES}")


def _norm_matmul_body(x_ref, g_ref, w_ref, o_ref):
    h = _rms(x_ref[...], g_ref[...]).astype(BF16)
    n = o_ref.shape[-1]
    nc = _col_chunk(n)
    for c in range(0, n, nc):
        o_ref[:, c:c + nc] = _dot(h, w_ref[:, c:c + nc]).astype(o_ref.dtype)


def _norm_matmul(x, g, w, tm):
    t, d = x.shape
    n = w.shape[1]
    return pl.pallas_call(
        _norm_matmul_body,
        out_shape=jax.ShapeDtypeStruct((t, n), BF16),
        grid=(t // tm,),
        in_specs=[pl.BlockSpec((tm, d), lambda i: (i, 0)), _resident((1, d)), _resident((d, n))],
        out_specs=pl.BlockSpec((tm, n), lambda i: (i, 0)),
        compiler_params=_cparams("parallel"),
        name="norm_matmul",
    )(x, g, w)


def _log_sigmoid(x):
    return jnp.minimum(x, 0.0) - jnp.log1p(jnp.exp(-jnp.abs(x)))


def _gla_tables(reverse):
    c = GLA_CHUNK
    row = lax.broadcasted_iota(jnp.int32, (c, c), 0)
    col = lax.broadcasted_iota(jnp.int32, (c, c), 1)
    if reverse:
        cum = (col >= row)
        keep = col > row
    else:
        cum = (col <= row)
        keep = col <= row
    return jnp.where(cum, 1.0, 0.0).astype(BF16), keep


def _gla_log_decay(gate_ref, w2_ref, b2_ref, la_ref):
    logits = _dot(gate_ref[0], w2_ref[...]) + b2_ref[...]
    la_ref[...] = _log_sigmoid(logits) * (1.0 / GLA_TAU)


def _gla_chunk_head(q, k, v, la, state_t, cum_mat, keep, reverse):
    c = GLA_CHUNK
    la_hi = la.astype(BF16)
    la_lo = (la - la_hi.astype(F32)).astype(BF16)
    cum = _dot(cum_mat, la_hi) + _dot(cum_mat, la_lo)
    if reverse:
        mid = cum[c // 2 - 1:c // 2, :]
        last = cum[0:1, :]
    else:
        mid = cum[c // 2:c // 2 + 1, :]
        last = cum[c - 1:c, :]
    qs = q.astype(F32) * (GLA_DK ** -0.5)
    kf = k.astype(F32)
    qi = (qs * jnp.exp(cum - mid)).astype(BF16)
    ki = (kf * jnp.exp(mid - cum)).astype(BF16)
    att = jnp.where(keep, _dot_nt(qi, ki), 0.0).astype(BF16)
    o = _dot(att, v)
    o = o + _dot_nt((qs * jnp.exp(cum)).astype(BF16), state_t.astype(BF16))
    kd = (kf * jnp.exp(last - cum)).astype(BF16)
    new_state = state_t * jnp.exp(last) + _dot_tn(v, kd)
    return o, new_state


def _gla_fwd_body(q_ref, k_ref, v_ref, gate_ref, w2_ref, b2_ref, o_ref, state_ref, la_ref):
    @pl.when(pl.program_id(1) == 0)
    def _():
        state_ref[...] = jnp.zeros_like(state_ref)

    _gla_log_decay(gate_ref, w2_ref, b2_ref, la_ref)
    cum_mat, keep = _gla_tables(False)
    n_chunks = q_ref.shape[1] // GLA_CHUNK

    def chunk(ci, carry):
        rows = pl.ds(pl.multiple_of(ci * GLA_CHUNK, GLA_CHUNK), GLA_CHUNK)
        for h in range(GLA_HEADS):
            kcols = slice(h * GLA_DK, (h + 1) * GLA_DK)
            vcols = slice(h * GLA_DV, (h + 1) * GLA_DV)
            o, st = _gla_chunk_head(q_ref[0, rows, kcols], k_ref[0, rows, kcols], v_ref[0, rows, vcols],
                                    la_ref[rows, kcols], state_ref[h], cum_mat, keep, False)
            state_ref[h] = st
            o_ref[0, rows, vcols] = o
        return carry

    lax.fori_loop(0, n_chunks, chunk, 0)


def _gla_bwd_body(q_ref, k_ref, v_ref, gate_ref, w2_ref, b2_ref, of_ref, r_ref, gn_ref,
                  o_ref, state_ref, la_ref):
    @pl.when(pl.program_id(1) == 0)
    def _():
        state_ref[...] = jnp.zeros_like(state_ref)

    _gla_log_decay(gate_ref, w2_ref, b2_ref, la_ref)
    cum_mat, keep = _gla_tables(True)
    n_chunks = q_ref.shape[1] // GLA_CHUNK
    gn = gn_ref[...]

    def chunk(step, carry):
        ci = n_chunks - 1 - step
        rows = pl.ds(pl.multiple_of(ci * GLA_CHUNK, GLA_CHUNK), GLA_CHUNK)
        for h in range(GLA_HEADS):
            kcols = slice(h * GLA_DK, (h + 1) * GLA_DK)
            vcols = slice(h * GLA_DV, (h + 1) * GLA_DV)
            o, st = _gla_chunk_head(q_ref[0, rows, kcols], k_ref[0, rows, kcols], v_ref[0, rows, vcols],
                                    la_ref[rows, kcols], state_ref[h], cum_mat, keep, True)
            state_ref[h] = st
            o = o + of_ref[0, rows, vcols]
            r = r_ref[0, rows, vcols].astype(F32)
            silu = r * (1.0 / (1.0 + jnp.exp(-r)))
            o_ref[0, rows, vcols] = (_rms(o, gn) * silu).astype(o_ref.dtype)
        return carry

    lax.fori_loop(0, n_chunks, chunk, 0)


def _gla(proj, w2f, b2f, w2b, b2b, out_norm, ts):
    b, s, _ = proj.shape
    ns = s // ts

    def col_spec(width, offset, rev):
        blk = offset // width
        if rev:
            return pl.BlockSpec((1, ts, width), lambda bi, si: (bi, ns - 1 - si, blk))
        return pl.BlockSpec((1, ts, width), lambda bi, si: (bi, si, blk))

    def common_specs(rev):
        return [col_spec(GLA_QK, IN_GQ, rev), col_spec(GLA_QK, IN_GK, rev), col_spec(GLA_WIDTH, IN_GV, rev),
                col_spec(LANES, IN_GATE, rev), _resident((LANES, GLA_QK)), _resident((1, GLA_QK))]

    scratch = [pltpu.VMEM((GLA_HEADS, GLA_DV, GLA_DK), F32), pltpu.VMEM((ts, GLA_QK), F32)]
    o_fwd = pl.pallas_call(
        _gla_fwd_body,
        out_shape=jax.ShapeDtypeStruct((b, s, GLA_WIDTH), F32),
        grid=(b, ns),
        in_specs=common_specs(False),
        out_specs=pl.BlockSpec((1, ts, GLA_WIDTH), lambda bi, si: (bi, si, 0)),
        scratch_shapes=scratch,
        compiler_params=_cparams("parallel", "arbitrary"),
        name="gla_fwd",
    )(proj, proj, proj, proj, w2f, b2f)
    return pl.pallas_call(
        _gla_bwd_body,
        out_shape=jax.ShapeDtypeStruct((b, s, GLA_WIDTH), BF16),
        grid=(b, ns),
        in_specs=common_specs(True) + [
            pl.BlockSpec((1, ts, GLA_WIDTH), lambda bi, si: (bi, ns - 1 - si, 0)),
            col_spec(GLA_WIDTH, IN_GR, True),
            _resident((1, GLA_DV)),
        ],
        out_specs=pl.BlockSpec((1, ts, GLA_WIDTH), lambda bi, si: (bi, ns - 1 - si, 0)),
        scratch_shapes=scratch,
        compiler_params=_cparams("parallel", "arbitrary"),
        name="gla_bwd",
    )(proj, proj, proj, proj, w2b, b2b, o_fwd, proj, out_norm)


def _mla_prep_body(cq_ref, ckv_ref, kr_ref, qn_ref, kvn_ref, wq_nope_ref, wq_rope_ref, wq_rot_ref, wkv_ref,
                   cos_ref, sin_ref, q_out, k_out, vt_out):
    hq = _rms(cq_ref[0].astype(F32), qn_ref[...]).astype(BF16)
    hkv = _rms(ckv_ref[0].astype(F32), kvn_ref[...]).astype(BF16)
    cos = cos_ref[...]
    sin = sin_ref[...]
    q_scale = MLA_QK ** -0.5 * LOG2E
    q_nope = _dot(hq, wq_nope_ref[...]) * q_scale
    q_pe = (_dot(hq, wq_rope_ref[...]) * cos + _dot(hq, wq_rot_ref[...]) * sin) * q_scale
    kv = _dot(hkv, wkv_ref[...])
    kr = kr_ref[0].astype(F32)
    k_pe = (kr[:, :MLA_ROPE] * cos[:, :MLA_ROPE] + kr[:, MLA_ROPE:] * sin[:, :MLA_ROPE]).astype(BF16)
    for h in range(MLA_HEADS):
        q_out[0, h, :, :MLA_NOPE] = q_nope[:, h * MLA_NOPE:(h + 1) * MLA_NOPE].astype(BF16)
        q_out[0, h, :, MLA_NOPE:] = q_pe[:, h * MLA_ROPE:(h + 1) * MLA_ROPE].astype(BF16)
        base = h * (MLA_NOPE + MLA_V)
        k_out[0, h, :, :MLA_NOPE] = kv[:, base:base + MLA_NOPE].astype(BF16)
        k_out[0, h, :, MLA_NOPE:] = k_pe
        vt_out[0, h] = kv[:, base + MLA_NOPE:base + MLA_NOPE + MLA_V].T.astype(BF16)


def _reduce_rows(op, x, groups=8):
    rows, n = x.shape
    partial = op(x.reshape(groups, rows // groups, n), axis=0)
    return op(partial, axis=0, keepdims=True)


def _mla_attn_body(q_ref, k_ref, vt_ref, o_ref, s_ref, *, tk):
    q = q_ref[0, 0]
    tq = q.shape[0]
    n_chunks = k_ref.shape[2] // tk

    n_slots = s_ref.shape[0]

    def scores(c):
        s_ref[c % n_slots] = _dot_nt(k_ref[0, 0, c * tk:(c + 1) * tk, :], q)

    m = jnp.full((1, tq), -jnp.inf, F32)
    l = jnp.zeros((1, tq), F32)
    acc = jnp.zeros((MLA_V, tq), F32)
    for c in range(min(n_slots - 1, n_chunks)):
        scores(c)
    for c in range(n_chunks):
        if c + n_slots - 1 < n_chunks:
            scores(c + n_slots - 1)
        s = s_ref[c % n_slots]
        m_new = jnp.maximum(m, _reduce_rows(jnp.max, s))
        alpha = jnp.exp2(m - m_new)
        p = jnp.exp2(s - m_new)
        l = alpha * l + _reduce_rows(jnp.sum, p)
        acc = alpha * acc + _dot(vt_ref[0, 0, :, c * tk:(c + 1) * tk], p.astype(BF16))
        m = m_new
    o_ref[0] = (acc / l).T.astype(o_ref.dtype)


def _mla(proj, q_norm, kv_norm, wq_nope, wq_rope, wq_rot, wkv, cos_t, sin_t, ts, tq, tk):
    b, s, _ = proj.shape
    rope_w = MLA_HEADS * MLA_ROPE
    q, k, vt = pl.pallas_call(
        _mla_prep_body,
        out_shape=(jax.ShapeDtypeStruct((b, MLA_HEADS, s, MLA_QK), BF16),
                   jax.ShapeDtypeStruct((b, MLA_HEADS, s, MLA_QK), BF16),
                   jax.ShapeDtypeStruct((b, MLA_HEADS, MLA_V, s), BF16)),
        grid=(b, s // ts),
        in_specs=[
            pl.BlockSpec((1, ts, MLA_RANK), lambda bi, si: (bi, si, IN_CQ // MLA_RANK)),
            pl.BlockSpec((1, ts, MLA_RANK), lambda bi, si: (bi, si, IN_CKV // MLA_RANK)),
            pl.BlockSpec((1, ts, LANES), lambda bi, si: (bi, si, IN_KR // LANES)),
            _resident((1, MLA_RANK)), _resident((1, MLA_RANK)),
            _resident((MLA_RANK, MLA_HEADS * MLA_NOPE)), _resident((MLA_RANK, rope_w)),
            _resident((MLA_RANK, rope_w)), _resident((MLA_RANK, MLA_HEADS * (MLA_NOPE + MLA_V))),
            pl.BlockSpec((ts, rope_w), lambda bi, si: (si, 0)),
            pl.BlockSpec((ts, rope_w), lambda bi, si: (si, 0)),
        ],
        out_specs=(pl.BlockSpec((1, MLA_HEADS, ts, MLA_QK), lambda bi, si: (bi, 0, si, 0)),
                   pl.BlockSpec((1, MLA_HEADS, ts, MLA_QK), lambda bi, si: (bi, 0, si, 0)),
                   pl.BlockSpec((1, MLA_HEADS, MLA_V, ts), lambda bi, si: (bi, 0, 0, si))),
        compiler_params=_cparams("parallel", "parallel"),
        name="mla_prep",
    )(proj, proj, proj, q_norm, kv_norm, wq_nope, wq_rope, wq_rot, wkv, cos_t, sin_t)
    return pl.pallas_call(
        functools.partial(_mla_attn_body, tk=tk),
        out_shape=jax.ShapeDtypeStruct((b, s, MLA_WIDTH), BF16),
        grid=(b, MLA_HEADS, s // tq),
        in_specs=[
            pl.BlockSpec((1, 1, tq, MLA_QK), lambda bi, hi, qi: (bi, hi, qi, 0)),
            pl.BlockSpec((1, 1, s, MLA_QK), lambda bi, hi, qi: (bi, hi, 0, 0)),
            pl.BlockSpec((1, 1, MLA_V, s), lambda bi, hi, qi: (bi, hi, 0, 0)),
        ],
        out_specs=pl.BlockSpec((1, tq, MLA_V), lambda bi, hi, qi: (bi, qi, hi)),
        scratch_shapes=[pltpu.VMEM((3, tk, tq), F32)],
        compiler_params=_cparams("parallel", "parallel", "parallel"),
        name="mla_attn",
    )(q, k, vt)


def _residual_norm_store(x, o_ref, sumsq, g):
    scale = lax.rsqrt(sumsq * (1.0 / o_ref.shape[-1]) + EPS)
    o_ref[...] = x + o_ref[...] * scale * g


def _mix_out_body(a1_ref, a2_ref, w1_ref, w2_ref, x_ref, g_ref, o_ref):
    a1 = a1_ref[...]
    a2 = a2_ref[...]
    n = o_ref.shape[-1]
    nc = _col_chunk(n)
    sumsq = jnp.zeros((o_ref.shape[0], 1), F32)
    for c in range(0, n, nc):
        y = _dot(a1, w1_ref[:, c:c + nc]) + _dot(a2, w2_ref[:, c:c + nc])
        sumsq = sumsq + jnp.sum(y * y, axis=-1, keepdims=True)
        o_ref[:, c:c + nc] = y
    _residual_norm_store(x_ref[...], o_ref, sumsq, g_ref[...])


def _mix_out(o_gla, o_mla, w1, w2, x, g, tm):
    t, d = x.shape
    return pl.pallas_call(
        _mix_out_body,
        out_shape=jax.ShapeDtypeStruct((t, d), F32),
        grid=(t // tm,),
        in_specs=[
            pl.BlockSpec((tm, GLA_WIDTH), lambda i: (i, 0)),
            pl.BlockSpec((tm, MLA_WIDTH), lambda i: (i, 0)),
            _resident((GLA_WIDTH, d)), _resident((MLA_WIDTH, d)),
            pl.BlockSpec((tm, d), lambda i: (i, 0)),
            _resident((1, d)),
        ],
        out_specs=pl.BlockSpec((tm, d), lambda i: (i, 0)),
        compiler_params=_cparams("parallel"),
        name="mix_out",
    )(o_gla, o_mla, w1, w2, x, g)


def _mem_attn_body(x_ref, gpre_ref, wq_ref, kv_ref, wo_ref, gpost_ref, o_ref, a_ref):
    x = x_ref[0]
    h = _rms(x, gpre_ref[...]).astype(BF16)
    q_scale = MEM_HEAD_DIM ** -0.5 * LOG2E
    for hd in range(MEM_HEADS):
        cols = slice(hd * MEM_HEAD_DIM, (hd + 1) * MEM_HEAD_DIM)
        vcols = slice(D_MODEL + hd * MEM_HEAD_DIM, D_MODEL + (hd + 1) * MEM_HEAD_DIM)
        q = (_dot(h, wq_ref[:, cols]) * q_scale).astype(BF16)
        s = _dot_nt(q, kv_ref[0, :, cols])
        m = jnp.max(s, axis=-1, keepdims=True)
        p = jnp.exp2(s - m)
        l = jnp.sum(p, axis=-1, keepdims=True)
        a_ref[:, cols] = (_dot(p.astype(BF16), kv_ref[0, :, vcols]) / l).astype(BF16)
    a = a_ref[...]
    sumsq = jnp.zeros((x.shape[0], 1), F32)
    nc = MEM_HEAD_DIM
    for c in range(0, D_MODEL, nc):
        y = _dot(a, wo_ref[:, c:c + nc])
        sumsq = sumsq + jnp.sum(y * y, axis=-1, keepdims=True)
        o_ref[0, :, c:c + nc] = y
    _residual_norm_store(x, o_ref.at[0], sumsq, gpost_ref[...])


def _mem_attn(x, gpre, wq, kv, wo, gpost, tm):
    b, s, d = x.shape
    m = kv.shape[1]
    return pl.pallas_call(
        _mem_attn_body,
        out_shape=jax.ShapeDtypeStruct((b, s, d), F32),
        grid=(b, s // tm),
        in_specs=[
            pl.BlockSpec((1, tm, d), lambda bi, si: (bi, si, 0)),
            _resident((1, d)), _resident((d, d)),
            pl.BlockSpec((1, m, 2 * d), lambda bi, si: (bi, 0, 0)),
            _resident((d, d)), _resident((1, d)),
        ],
        out_specs=pl.BlockSpec((1, tm, d), lambda bi, si: (bi, si, 0)),
        scratch_shapes=[pltpu.VMEM((tm, d), BF16)],
        compiler_params=_cparams("parallel", "parallel"),
        name="mem_attn",
    )(x, gpre, wq, kv, wo, gpost)


def _gelu_tanh_times(x, v_half):
    c = math.sqrt(2.0 / math.pi)
    inner = x * (c + (c * 0.044715) * (x * x))
    return x * (1.0 + jnp.tanh(inner)) * v_half


def _ffn_body(x_ref, xprev_ref, xnext_ref, gpre_ref, wg_ref, wv_ref, cwg_ref, cwv_ref, cbg_ref, cbv_ref,
              wd_ref, gpost_ref, o_ref, h_ref, acc_ref, u_ref):
    si = pl.program_id(1)
    j = pl.program_id(2)
    tm = x_ref.shape[1]
    sub = FFN_SUB
    hl = SUBLANES
    n_lane_tiles = 2 * sub // LANES

    @pl.when(j == 0)
    def _():
        g = gpre_ref[...]
        h_ref[:tm] = _rms(x_ref[0], g).astype(BF16)
        nxt = jnp.where(si == pl.num_programs(1) - 1, 0.0, _rms(xnext_ref[0], g))
        prv = jnp.where(si == 0, 0.0, _rms(xprev_ref[0], g))
        h_ref[tm:] = jnp.concatenate([nxt, prv], axis=0).astype(BF16)
        acc_ref[...] = jnp.zeros_like(acc_ref)

    h = h_ref[...]
    n_sub = wd_ref.shape[0] // sub
    cols = [slice(c * sub, (c + 1) * sub) for c in range(n_sub)]

    def gate_value(g_ref, v_ref, c):
        return jnp.concatenate([g_ref[:, cols[c]], v_ref[:, cols[c]]], axis=1)

    def up_project(c):
        u = _dot(h, gate_value(wg_ref, wv_ref, c))
        for k in range(n_lane_tiles):
            lanes = slice(k * LANES, (k + 1) * LANES)
            u_ref[c % 2, k, hl:hl + tm] = u[:tm, lanes]
            u_ref[c % 2, k, :hl] = u[tm + hl:, lanes]
            u_ref[c % 2, k, hl + tm:] = u[tm:tm + hl, lanes]

    def window(c, offset):
        return jnp.concatenate([u_ref[c % 2, k, hl + offset:hl + offset + tm] for k in range(n_lane_tiles)],
                               axis=1)

    def gate_and_down(c):
        cw = gate_value(cwg_ref, cwv_ref, c)
        cu = (window(c, -1) * cw[0:1] + window(c, 0) * cw[1:2] + window(c, 1) * cw[2:3]
              + gate_value(cbg_ref, cbv_ref, c))
        act = _gelu_tanh_times(cu[:, :sub], cu[:, sub:]).astype(BF16)
        acc_ref[...] += _dot(act, wd_ref[cols[c], :])

    up_project(0)
    for c in range(n_sub):
        if c + 1 < n_sub:
            up_project(c + 1)
        gate_and_down(c)

    @pl.when(j == pl.num_programs(2) - 1)
    def _():
        o_ref[0] = x_ref[0] + _rms(acc_ref[...], gpost_ref[...])


def _ffn(x, gpre, w_up, conv_w, conv_b, w_down, gpost, tm, fc):
    b, s, d = x.shape
    nf = D_FF // fc
    halo = SUBLANES
    tb = tm // halo
    last_blk = s // halo - 1
    return pl.pallas_call(
        _ffn_body,
        out_shape=jax.ShapeDtypeStruct((b, s, d), F32),
        grid=(b, s // tm, nf),
        in_specs=[
            pl.BlockSpec((1, tm, d), lambda bi, si, j: (bi, si, 0)),
            pl.BlockSpec((1, halo, d), lambda bi, si, j: (bi, jnp.maximum(si * tb - 1, 0), 0)),
            pl.BlockSpec((1, halo, d), lambda bi, si, j: (bi, jnp.minimum((si + 1) * tb, last_blk), 0)),
            _resident((1, d)),
            pl.BlockSpec((d, fc), lambda bi, si, j: (0, j)),
            pl.BlockSpec((d, fc), lambda bi, si, j: (0, nf + j)),
            pl.BlockSpec((CONV_WIDTH, fc), lambda bi, si, j: (0, j)),
            pl.BlockSpec((CONV_WIDTH, fc), lambda bi, si, j: (0, nf + j)),
            pl.BlockSpec((1, fc), lambda bi, si, j: (0, j)),
            pl.BlockSpec((1, fc), lambda bi, si, j: (0, nf + j)),
            pl.BlockSpec((fc, d), lambda bi, si, j: (j, 0)),
            _resident((1, d)),
        ],
        out_specs=pl.BlockSpec((1, tm, d), lambda bi, si, j: (bi, si, 0)),
        scratch_shapes=[pltpu.VMEM((tm + 2 * halo, d), BF16), pltpu.VMEM((tm, d), F32),
                        pltpu.VMEM((2, 2 * FFN_SUB // LANES, tm + 2 * halo, LANES), F32)],
        compiler_params=_cparams("parallel", "parallel", "arbitrary"),
        name="conv_ffn",
    )(x, x, x, gpre, w_up, w_up, conv_w, conv_w, conv_b, conv_b, w_down, gpost)


def _rotate_half_columns(w):
    k, n = w.shape
    g = w.reshape(k, n // MLA_ROPE, 2, MLA_ROPE // 2)
    return jnp.concatenate([-g[:, :, 1:], g[:, :, :1]], axis=2).reshape(k, n)


def _prepare_weights(p, seq):
    row = lambda v: v.reshape(1, -1).astype(F32)
    w_in = p["w_in"]
    offs, acc = [], 0
    for wdt in (GLA_QK, GLA_QK, GLA_WIDTH, GLA_WIDTH, GLA_GATE_RANK, GLA_GATE_RANK, MLA_RANK, MLA_RANK, MLA_ROPE):
        offs.append((acc, acc + wdt))
        acc += wdt
    part = [w_in[:, a:b] for a, b in offs]
    gq, gk, gv, gr, ggf, ggb, cq, ckv, kr = part
    d = w_in.shape[0]
    gate_pad = jnp.zeros((d, LANES - 2 * GLA_GATE_RANK), w_in.dtype)
    w_in_p = jnp.concatenate([gq, gk, gv, gr, cq, ckv, ggf, ggb, gate_pad, kr, _rotate_half_columns(kr)],
                             axis=1).astype(BF16)

    def gate_w2(w2, slot):
        full = jnp.zeros((LANES, GLA_QK), F32)
        return full.at[slot * GLA_GATE_RANK:(slot + 1) * GLA_GATE_RANK].set(w2).astype(BF16)

    wq = p["mla_w_q_up"].reshape(MLA_RANK, MLA_HEADS, MLA_QK)
    wq_nope = wq[:, :, :MLA_NOPE].reshape(MLA_RANK, MLA_HEADS * MLA_NOPE)
    wq_rope = wq[:, :, MLA_NOPE:].reshape(MLA_RANK, MLA_HEADS * MLA_ROPE)

    half = MLA_ROPE // 2
    freqs = ROPE_BASE ** (-jnp.arange(half, dtype=F32) / half)
    ang = jnp.arange(seq, dtype=F32)[:, None] * freqs[None, :]
    cos_t = jnp.tile(jnp.cos(ang), (1, 2 * MLA_HEADS))
    sin_t = jnp.tile(jnp.sin(ang), (1, 2 * MLA_HEADS))

    value_half = jnp.concatenate([jnp.ones((1, D_FF), F32), jnp.full((1, D_FF), 0.5, F32)], axis=1)

    return dict(
        g_mix_pre=row(p["norm_mix_pre"]), g_mix_post=row(p["norm_mix_post"]), w_in=w_in_p,
        w2f=gate_w2(p["gla_gate_w2_fwd"], 0), b2f=row(p["gla_gate_b_fwd"]),
        w2b=gate_w2(p["gla_gate_w2_bwd"], 1), b2b=row(p["gla_gate_b_bwd"]),
        gla_out_norm=row(p["gla_out_norm"]),
        mla_q_norm=row(p["mla_q_norm"]), mla_kv_norm=row(p["mla_kv_norm"]),
        wq_nope=wq_nope.astype(BF16), wq_rope=wq_rope.astype(BF16),
        wq_rot=_rotate_half_columns(wq_rope).astype(BF16), wkv=p["mla_w_kv_up"].astype(BF16),
        cos=cos_t, sin=sin_t,
        w_out_gla=p["w_out"][:GLA_WIDTH].astype(BF16), w_out_mla=p["w_out"][GLA_WIDTH:].astype(BF16),
        g_mem_pre=row(p["norm_mem_pre"]), g_mem_post=row(p["norm_mem_post"]), g_mem_kv=row(p["mem_kv_norm"]),
        w_mem_q=p["w_mem_q"].astype(BF16), w_mem_o=p["w_mem_o"].astype(BF16),
        w_mem_kv=jnp.concatenate([p["w_mem_k"], p["w_mem_v"]], axis=1).astype(BF16),
        g_ffn_pre=row(p["norm_ffn_pre"]), g_ffn_post=row(p["norm_ffn_post"]),
        w_ffn_up=p["w_ffn_up"].astype(BF16),
        ffn_conv_w=p["ffn_conv_w"].astype(F32) * value_half,
        ffn_conv_b=row(p["ffn_conv_b"]) * value_half,
        w_ffn_down=p["w_ffn_down"].astype(BF16),
    )


def _tiles(seq):
    tm = min(512, seq)
    return dict(tm=tm, gla_ts=tm, mla_ts=tm, mla_tq=tm, mla_tk=tm, ffn_fc=512)


def _apply_layer(x, mem, w):
    b, s, d = x.shape
    t = _tiles(s)
    tm = t["tm"]
    x2 = x.reshape(b * s, d)
    proj = _norm_matmul(x2, w["g_mix_pre"], w["w_in"], tm).reshape(b, s, IN_WIDTH_PADDED)
    o_gla = _gla(proj, w["w2f"], w["b2f"], w["w2b"], w["b2b"], w["gla_out_norm"], t["gla_ts"])
    o_mla = _mla(proj, w["mla_q_norm"], w["mla_kv_norm"], w["wq_nope"], w["wq_rope"], w["wq_rot"], w["wkv"],
                 w["cos"], w["sin"], t["mla_ts"], t["mla_tq"], t["mla_tk"])
    x2 = _mix_out(o_gla.reshape(b * s, GLA_WIDTH), o_mla.reshape(b * s, MLA_WIDTH),
                  w["w_out_gla"], w["w_out_mla"], x2, w["g_mix_post"], tm)
    mtok = mem.shape[1]
    kv = _norm_matmul(mem.reshape(b * mtok, d), w["g_mem_kv"], w["w_mem_kv"], min(tm, b * mtok))
    x3 = _mem_attn(x2.reshape(b, s, d), w["g_mem_pre"], w["w_mem_q"], kv.reshape(b, mtok, 2 * d),
                   w["w_mem_o"], w["g_mem_post"], tm)
    return _ffn(x3, w["g_ffn_pre"], w["w_ffn_up"], w["ffn_conv_w"], w["ffn_conv_b"], w["w_ffn_down"],
                w["g_ffn_post"], tm, t["ffn_fc"])


def kernel(x_prompt, x_sample, mem_prompt, mem_sample, norm_mix_pre, norm_mix_post, w_in, gla_gate_w2_fwd, gla_gate_b_fwd, gla_gate_w2_bwd, gla_gate_b_bwd, gla_out_norm, mla_q_norm, mla_w_q_up, mla_kv_norm, mla_w_kv_up, w_out, norm_mem_pre, norm_mem_post, mem_kv_norm, w_mem_q, w_mem_k, w_mem_v, w_mem_o, norm_ffn_pre, norm_ffn_post, w_ffn_up, ffn_conv_w, ffn_conv_b, w_ffn_down):
    params = dict(
        norm_mix_pre=norm_mix_pre, norm_mix_post=norm_mix_post, w_in=w_in,
        gla_gate_w2_fwd=gla_gate_w2_fwd, gla_gate_b_fwd=gla_gate_b_fwd,
        gla_gate_w2_bwd=gla_gate_w2_bwd, gla_gate_b_bwd=gla_gate_b_bwd, gla_out_norm=gla_out_norm,
        mla_q_norm=mla_q_norm, mla_w_q_up=mla_w_q_up, mla_kv_norm=mla_kv_norm, mla_w_kv_up=mla_w_kv_up,
        w_out=w_out, norm_mem_pre=norm_mem_pre, norm_mem_post=norm_mem_post, mem_kv_norm=mem_kv_norm,
        w_mem_q=w_mem_q, w_mem_k=w_mem_k, w_mem_v=w_mem_v, w_mem_o=w_mem_o,
        norm_ffn_pre=norm_ffn_pre, norm_ffn_post=norm_ffn_post, w_ffn_up=w_ffn_up,
        ffn_conv_w=ffn_conv_w, ffn_conv_b=ffn_conv_b, w_ffn_down=w_ffn_down,
    )
    depth = w_in.shape[0]
    yp, ys = x_prompt, x_sample
    for layer in range(depth):
        p = {k: v[layer] for k, v in params.items()}
        assert yp.shape[1] == ys.shape[1]
        w = _prepare_weights(p, yp.shape[1])
        yp = _apply_layer(yp, mem_prompt, w)
        ys = _apply_layer(ys, mem_sample, w)
    return (yp, ys)
```

```python
import functools
import math

import jax
import jax.numpy as jnp
from jax import lax
from jax.experimental import pallas as pl
from jax.experimental.pallas import tpu as pltpu

F32 = jnp.float32
BF16 = jnp.bfloat16

EPS = 1e-6
D_MODEL = 2048
GLA_HEADS = 4
GLA_DK = 128
GLA_DV = 256
GLA_GATE_RANK = 16
GLA_TAU = 16.0
GLA_CHUNK = 64
GLA_SUB = 256
GLA_QK = GLA_HEADS * GLA_DK
GLA_WIDTH = GLA_HEADS * GLA_DV
MLA_HEADS = 8
MLA_RANK = 512
MLA_NOPE = 128
MLA_ROPE = 64
MLA_V = 128
MLA_QK = MLA_NOPE + MLA_ROPE
MLA_WIDTH = MLA_HEADS * MLA_V
ROPE_BASE = 10000.0
MEM_HEADS = 4
MEM_HEAD_DIM = D_MODEL // MEM_HEADS
D_FF = 5632
CONV_WIDTH = 3
FFN_SUB = 256

LANES = 128
SUBLANES = 8
BF16_ROWS = 16
VMEM_LIMIT_BYTES = 56 * 1024 * 1024
LOG2E = 1.4426950408889634

IN_GQ = 0
IN_GK = IN_GQ + GLA_QK
IN_GV = IN_GK + GLA_QK
IN_GR = IN_GV + GLA_WIDTH
IN_CQ = IN_GR + GLA_WIDTH
IN_CKV = IN_CQ + MLA_RANK
IN_GATE = IN_CKV + MLA_RANK
IN_KR = IN_GATE + LANES
IN_WIDTH_PADDED = IN_KR + LANES


def _cparams(*semantics):
    return pltpu.CompilerParams(dimension_semantics=semantics, vmem_limit_bytes=VMEM_LIMIT_BYTES)


def _resident(shape):
    zeros = (0,) * len(shape)
    return pl.BlockSpec(shape, lambda *_: zeros, pipeline_mode=pl.Buffered(1))


def _rms(x, g):
    return x * lax.rsqrt(jnp.mean(x * x, axis=-1, keepdims=True) + EPS) * g


def _dot(a, b):
    return jnp.dot(a, b, preferred_element_type=F32)


def _dot_nt(a, b):
    return lax.dot_general(a, b, (((1,), (1,)), ((), ())), preferred_element_type=F32)


def _dot_tn(a, b):
    return lax.dot_general(a, b, (((0,), (0,)), ((), ())), preferred_element_type=F32)


def _col_chunk(n):
    for c in (512, 256, LANES):
        if n % c == 0:
            return c
    raise ValueError(f"width {n} is not a multiple of {LANES}")


def _norm_matmul_body(x_ref, g_ref, w_ref, o_ref):
    h = _rms(x_ref[...], g_ref[...]).astype(BF16)
    n = o_ref.shape[-1]
    nc = _col_chunk(n)
    for c in range(0, n, nc):
        o_ref[:, c:c + nc] = _dot(h, w_ref[:, c:c + nc]).astype(o_ref.dtype)


def _norm_matmul(x, g, w, tm):
    t, d = x.shape
    n = w.shape[1]
    return pl.pallas_call(
        _norm_matmul_body,
        out_shape=jax.ShapeDtypeStruct((t, n), BF16),
        grid=(t // tm,),
        in_specs=[pl.BlockSpec((tm, d), lambda i: (i, 0)), _resident((1, d)), _resident((d, n))],
        out_specs=pl.BlockSpec((tm, n), lambda i: (i, 0)),
        compiler_params=_cparams("parallel"),
        name="norm_matmul",
    )(x, g, w)


def _log_sigmoid(x):
    return jnp.minimum(x, 0.0) - jnp.log(1.0 + jnp.exp(-jnp.abs(x)))


def _gla_tables(reverse):
    shift = GLA_CHUNK.bit_length() - 1
    row = lax.broadcasted_iota(jnp.int32, (GLA_SUB, GLA_SUB), 0)
    col = lax.broadcasted_iota(jnp.int32, (GLA_SUB, GLA_SUB), 1)
    same_chunk = lax.shift_right_logical(row, shift) == lax.shift_right_logical(col, shift)
    in_chunk = lambda cond: jnp.where(same_chunk, jnp.where(cond, 1.0, 0.0), 0.0)
    if reverse:
        cum = in_chunk(col >= row)
        keep = in_chunk(col > row) > 0.5
    else:
        cum = in_chunk(col <= row)
        keep = cum > 0.5
    return cum.astype(BF16), keep


def _gla_log_decay(gate_ref, w2_ref, b2_ref, la_ref):
    logits = _dot(gate_ref[0], w2_ref[...]) + b2_ref[...]
    la_ref[...] = _log_sigmoid(logits) * (1.0 / GLA_TAU)


def _gla_block(q_ref, k_ref, v_ref, la_ref, state_ref, reverse):
    ts = q_ref.shape[1]
    c = GLA_CHUNK
    n_chunks = ts // c
    cum_mat, keep = _gla_tables(reverse)
    la = la_ref[...]
    la_hi = la.astype(BF16)
    la_lo = (la - la_hi.astype(F32)).astype(BF16)
    cum = jnp.concatenate(
        [_dot(cum_mat, la_hi[r:r + GLA_SUB]) + _dot(cum_mat, la_lo[r:r + GLA_SUB])
         for r in range(0, ts, GLA_SUB)], axis=0).reshape(n_chunks, c, GLA_QK)
    mid_row, last_row = (c // 2 - 1, 0) if reverse else (c // 2, c - 1)
    mid = cum[:, mid_row:mid_row + 1, :]
    last = cum[:, last_row:last_row + 1, :]
    q = q_ref[0].astype(F32).reshape(n_chunks, c, GLA_QK) * (GLA_DK ** -0.5)
    k = k_ref[0].astype(F32).reshape(n_chunks, c, GLA_QK)
    qi_f = q * jnp.exp(cum - mid)
    ki_f = k * jnp.exp(mid - cum)
    flat = lambda t: t.reshape(ts, GLA_QK).astype(BF16)
    qi = flat(qi_f)
    ki = flat(ki_f)
    qd = flat(qi_f * jnp.exp(mid))
    kd = flat(ki_f * jnp.exp(last - mid))
    chunk_decay = jnp.exp(last)
    order = range(n_chunks - 1, -1, -1) if reverse else range(n_chunks)
    outs = []
    for h in range(GLA_HEADS):
        kc = slice(h * GLA_DK, (h + 1) * GLA_DK)
        v = v_ref[0, :, h * GLA_DV:(h + 1) * GLA_DV]
        intra = []
        for r in range(0, ts, GLA_SUB):
            rows = slice(r, r + GLA_SUB)
            att = jnp.where(keep, _dot_nt(qi[rows, kc], ki[rows, kc]), 0.0).astype(BF16)
            intra.append(_dot(att, v[rows]))
        state = state_ref[h]
        inter = [None] * n_chunks
        for n in order:
            rows = slice(n * c, (n + 1) * c)
            inter[n] = _dot_nt(qd[rows, kc], state.astype(BF16))
            state = state * chunk_decay[n, :, kc] + _dot_tn(v[rows], kd[rows, kc])
        state_ref[h] = state
        outs.append(jnp.concatenate(intra, axis=0) + jnp.concatenate(inter, axis=0))
    return outs


def _gla_fwd_body(q_ref, k_ref, v_ref, gate_ref, w2_ref, b2_ref, o_ref, state_ref, la_ref):
    @pl.when(pl.program_id(1) == 0)
    def _():
        state_ref[...] = jnp.zeros_like(state_ref)

    _gla_log_decay(gate_ref, w2_ref, b2_ref, la_ref)
    outs = _gla_block(q_ref, k_ref, v_ref, la_ref, state_ref, False)
    for h in range(GLA_HEADS):
        o_ref[0, :, h * GLA_DV:(h + 1) * GLA_DV] = outs[h]


def _gla_bwd_body(q_ref, k_ref, v_ref, gate_ref, w2_ref, b2_ref, of_ref, r_ref, gn_ref,
                  o_ref, state_ref, la_ref):
    @pl.when(pl.program_id(1) == 0)
    def _():
        state_ref[...] = jnp.zeros_like(state_ref)

    _gla_log_decay(gate_ref, w2_ref, b2_ref, la_ref)
    outs = _gla_block(q_ref, k_ref, v_ref, la_ref, state_ref, True)
    gn = gn_ref[...]
    for h in range(GLA_HEADS):
        vcols = slice(h * GLA_DV, (h + 1) * GLA_DV)
        o = outs[h] + of_ref[0, :, vcols]
        r = r_ref[0, :, vcols].astype(F32)
        silu = r * (1.0 / (1.0 + jnp.exp(-r)))
        o_ref[0, :, vcols] = (_rms(o, gn) * silu).astype(o_ref.dtype)


def _gla(proj, w2f, b2f, w2b, b2b, out_norm, ts):
    b, s, _ = proj.shape
    ns = s // ts

    def col_spec(width, offset, rev):
        blk = offset // width
        if rev:
            return pl.BlockSpec((1, ts, width), lambda bi, si: (bi, ns - 1 - si, blk))
        return pl.BlockSpec((1, ts, width), lambda bi, si: (bi, si, blk))

    def common_specs(rev):
        return [col_spec(GLA_QK, IN_GQ, rev), col_spec(GLA_QK, IN_GK, rev), col_spec(GLA_WIDTH, IN_GV, rev),
                col_spec(LANES, IN_GATE, rev), _resident((LANES, GLA_QK)), _resident((1, GLA_QK))]

    scratch = [pltpu.VMEM((GLA_HEADS, GLA_DV, GLA_DK), F32), pltpu.VMEM((ts, GLA_QK), F32)]
    o_fwd = pl.pallas_call(
        _gla_fwd_body,
        out_shape=jax.ShapeDtypeStruct((b, s, GLA_WIDTH), F32),
        grid=(b, ns),
        in_specs=common_specs(False),
        out_specs=pl.BlockSpec((1, ts, GLA_WIDTH), lambda bi, si: (bi, si, 0)),
        scratch_shapes=scratch,
        compiler_params=_cparams("parallel", "arbitrary"),
        name="gla_fwd",
    )(proj, proj, proj, proj, w2f, b2f)
    return pl.pallas_call(
        _gla_bwd_body,
        out_shape=jax.ShapeDtypeStruct((b, s, GLA_WIDTH), BF16),
        grid=(b, ns),
        in_specs=common_specs(True) + [
            pl.BlockSpec((1, ts, GLA_WIDTH), lambda bi, si: (bi, ns - 1 - si, 0)),
            col_spec(GLA_WIDTH, IN_GR, True),
            _resident((1, GLA_DV)),
        ],
        out_specs=pl.BlockSpec((1, ts, GLA_WIDTH), lambda bi, si: (bi, ns - 1 - si, 0)),
        scratch_shapes=scratch,
        compiler_params=_cparams("parallel", "arbitrary"),
        name="gla_bwd",
    )(proj, proj, proj, proj, w2b, b2b, o_fwd, proj, out_norm)


def _mla_prep_body(cq_ref, ckv_ref, kr_ref, qn_ref, kvn_ref, wq_nope_ref, wq_rope_ref, wq_rot_ref, wkv_ref,
                   cos_ref, sin_ref, q_out, k_out, vt_out):
    hq = _rms(cq_ref[0].astype(F32), qn_ref[...]).astype(BF16)
    hkv = _rms(ckv_ref[0].astype(F32), kvn_ref[...]).astype(BF16)
    cos = cos_ref[...]
    sin = sin_ref[...]
    q_scale = MLA_QK ** -0.5 * LOG2E
    q_nope = _dot(hq, wq_nope_ref[...]) * q_scale
    q_pe = (_dot(hq, wq_rope_ref[...]) * cos + _dot(hq, wq_rot_ref[...]) * sin) * q_scale
    kv = _dot(hkv, wkv_ref[...])
    kr = kr_ref[0].astype(F32)
    k_pe = (kr[:, :MLA_ROPE] * cos[:, :MLA_ROPE] + kr[:, MLA_ROPE:] * sin[:, :MLA_ROPE]).astype(BF16)
    for h in range(MLA_HEADS):
        q_out[0, h, :, :MLA_NOPE] = q_nope[:, h * MLA_NOPE:(h + 1) * MLA_NOPE].astype(BF16)
        q_out[0, h, :, MLA_NOPE:] = q_pe[:, h * MLA_ROPE:(h + 1) * MLA_ROPE].astype(BF16)
        base = h * (MLA_NOPE + MLA_V)
        k_out[0, h, :, :MLA_NOPE] = kv[:, base:base + MLA_NOPE].astype(BF16)
        k_out[0, h, :, MLA_NOPE:] = k_pe
        vt_out[0, h] = kv[:, base + MLA_NOPE:base + MLA_NOPE + MLA_V].T.astype(BF16)


def _reduce_rows(op, x, groups=8):
    rows, n = x.shape
    partial = op(x.reshape(groups, rows // groups, n), axis=0)
    return op(partial, axis=0, keepdims=True)


def _mla_attn_body(q_ref, k_ref, vt_ref, o_ref, s_ref, *, tk):
    q = q_ref[0, 0]
    tq = q.shape[0]
    n_chunks = k_ref.shape[2] // tk

    n_slots = s_ref.shape[0]

    def scores(c):
        s_ref[c % n_slots] = _dot_nt(k_ref[0, 0, c * tk:(c + 1) * tk, :], q)

    m = jnp.full((1, tq), -jnp.inf, F32)
    l = jnp.zeros((1, tq), F32)
    acc = jnp.zeros((MLA_V, tq), F32)
    for c in range(min(n_slots - 1, n_chunks)):
        scores(c)
    for c in range(n_chunks):
        if c + n_slots - 1 < n_chunks:
            scores(c + n_slots - 1)
        s = s_ref[c % n_slots]
        m_new = jnp.maximum(m, _reduce_rows(jnp.max, s))
        alpha = jnp.exp2(m - m_new)
        p = jnp.exp2(s - m_new)
        l = alpha * l + _reduce_rows(jnp.sum, p)
        acc = alpha * acc + _dot(vt_ref[0, 0, :, c * tk:(c + 1) * tk], p.astype(BF16))
        m = m_new
    o_ref[0] = (acc / l).T.astype(o_ref.dtype)


def _mla(proj, q_norm, kv_norm, wq_nope, wq_rope, wq_rot, wkv, cos_t, sin_t, ts, tq, tk):
    b, s, _ = proj.shape
    rope_w = MLA_HEADS * MLA_ROPE
    q, k, vt = pl.pallas_call(
        _mla_prep_body,
        out_shape=(jax.ShapeDtypeStruct((b, MLA_HEADS, s, MLA_QK), BF16),
                   jax.ShapeDtypeStruct((b, MLA_HEADS, s, MLA_QK), BF16),
                   jax.ShapeDtypeStruct((b, MLA_HEADS, MLA_V, s), BF16)),
        grid=(b, s // ts),
        in_specs=[
            pl.BlockSpec((1, ts, MLA_RANK), lambda bi, si: (bi, si, IN_CQ // MLA_RANK)),
            pl.BlockSpec((1, ts, MLA_RANK), lambda bi, si: (bi, si, IN_CKV // MLA_RANK)),
            pl.BlockSpec((1, ts, LANES), lambda bi, si: (bi, si, IN_KR // LANES)),
            _resident((1, MLA_RANK)), _resident((1, MLA_RANK)),
            _resident((MLA_RANK, MLA_HEADS * MLA_NOPE)), _resident((MLA_RANK, rope_w)),
            _resident((MLA_RANK, rope_w)), _resident((MLA_RANK, MLA_HEADS * (MLA_NOPE + MLA_V))),
            pl.BlockSpec((ts, rope_w), lambda bi, si: (si, 0)),
            pl.BlockSpec((ts, rope_w), lambda bi, si: (si, 0)),
        ],
        out_specs=(pl.BlockSpec((1, MLA_HEADS, ts, MLA_QK), lambda bi, si: (bi, 0, si, 0)),
                   pl.BlockSpec((1, MLA_HEADS, ts, MLA_QK), lambda bi, si: (bi, 0, si, 0)),
                   pl.BlockSpec((1, MLA_HEADS, MLA_V, ts), lambda bi, si: (bi, 0, 0, si))),
        compiler_params=_cparams("parallel", "parallel"),
        name="mla_prep",
    )(proj, proj, proj, q_norm, kv_norm, wq_nope, wq_rope, wq_rot, wkv, cos_t, sin_t)
    return pl.pallas_call(
        functools.partial(_mla_attn_body, tk=tk),
        out_shape=jax.ShapeDtypeStruct((b, s, MLA_WIDTH), BF16),
        grid=(b, MLA_HEADS, s // tq),
        in_specs=[
            pl.BlockSpec((1, 1, tq, MLA_QK), lambda bi, hi, qi: (bi, hi, qi, 0)),
            pl.BlockSpec((1, 1, s, MLA_QK), lambda bi, hi, qi: (bi, hi, 0, 0)),
            pl.BlockSpec((1, 1, MLA_V, s), lambda bi, hi, qi: (bi, hi, 0, 0)),
        ],
        out_specs=pl.BlockSpec((1, tq, MLA_V), lambda bi, hi, qi: (bi, qi, hi)),
        scratch_shapes=[pltpu.VMEM((3, tk, tq), F32)],
        compiler_params=_cparams("parallel", "parallel", "parallel"),
        name="mla_attn",
    )(q, k, vt)


def _residual_norm_store(x, o_ref, sumsq, g):
    scale = lax.rsqrt(sumsq * (1.0 / o_ref.shape[-1]) + EPS)
    o_ref[...] = x + o_ref[...] * scale * g


def _mix_out_body(a1_ref, a2_ref, w1_ref, w2_ref, x_ref, g_ref, o_ref):
    a1 = a1_ref[...]
    a2 = a2_ref[...]
    n = o_ref.shape[-1]
    nc = _col_chunk(n)
    sumsq = jnp.zeros((o_ref.shape[0], 1), F32)
    for c in range(0, n, nc):
        y = _dot(a1, w1_ref[:, c:c + nc]) + _dot(a2, w2_ref[:, c:c + nc])
        sumsq = sumsq + jnp.sum(y * y, axis=-1, keepdims=True)
        o_ref[:, c:c + nc] = y
    _residual_norm_store(x_ref[...], o_ref, sumsq, g_ref[...])


def _mix_out(o_gla, o_mla, w1, w2, x, g, tm):
    t, d = x.shape
    return pl.pallas_call(
        _mix_out_body,
        out_shape=jax.ShapeDtypeStruct((t, d), F32),
        grid=(t // tm,),
        in_specs=[
            pl.BlockSpec((tm, GLA_WIDTH), lambda i: (i, 0)),
            pl.BlockSpec((tm, MLA_WIDTH), lambda i: (i, 0)),
            _resident((GLA_WIDTH, d)), _resident((MLA_WIDTH, d)),
            pl.BlockSpec((tm, d), lambda i: (i, 0)),
            _resident((1, d)),
        ],
        out_specs=pl.BlockSpec((tm, d), lambda i: (i, 0)),
        compiler_params=_cparams("parallel"),
        name="mix_out",
    )(o_gla, o_mla, w1, w2, x, g)


def _mem_attn_body(x_ref, gpre_ref, wq_ref, kv_ref, wo_ref, gpost_ref, o_ref, a_ref):
    x = x_ref[0]
    h = _rms(x, gpre_ref[...]).astype(BF16)
    q_scale = MEM_HEAD_DIM ** -0.5 * LOG2E
    for hd in range(MEM_HEADS):
        cols = slice(hd * MEM_HEAD_DIM, (hd + 1) * MEM_HEAD_DIM)
        vcols = slice(D_MODEL + hd * MEM_HEAD_DIM, D_MODEL + (hd + 1) * MEM_HEAD_DIM)
        q = (_dot(h, wq_ref[:, cols]) * q_scale).astype(BF16)
        s = _dot_nt(q, kv_ref[0, :, cols])
        m = jnp.max(s, axis=-1, keepdims=True)
        p = jnp.exp2(s - m)
        l = jnp.sum(p, axis=-1, keepdims=True)
        a_ref[:, cols] = (_dot(p.astype(BF16), kv_ref[0, :, vcols]) / l).astype(BF16)
    a = a_ref[...]
    sumsq = jnp.zeros((x.shape[0], 1), F32)
    nc = MEM_HEAD_DIM
    for c in range(0, D_MODEL, nc):
        y = _dot(a, wo_ref[:, c:c + nc])
        sumsq = sumsq + jnp.sum(y * y, axis=-1, keepdims=True)
        o_ref[0, :, c:c + nc] = y
    _residual_norm_store(x, o_ref.at[0], sumsq, gpost_ref[...])


def _mem_attn(x, gpre, wq, kv, wo, gpost, tm):
    b, s, d = x.shape
    m = kv.shape[1]
    return pl.pallas_call(
        _mem_attn_body,
        out_shape=jax.ShapeDtypeStruct((b, s, d), F32),
        grid=(b, s // tm),
        in_specs=[
            pl.BlockSpec((1, tm, d), lambda bi, si: (bi, si, 0)),
            _resident((1, d)), _resident((d, d)),
            pl.BlockSpec((1, m, 2 * d), lambda bi, si: (bi, 0, 0)),
            _resident((d, d)), _resident((1, d)),
        ],
        out_specs=pl.BlockSpec((1, tm, d), lambda bi, si: (bi, si, 0)),
        scratch_shapes=[pltpu.VMEM((tm, d), BF16)],
        compiler_params=_cparams("parallel", "parallel"),
        name="mem_attn",
    )(x, gpre, wq, kv, wo, gpost)


def _gelu_tanh_times(x, v_half):
    c = math.sqrt(2.0 / math.pi)
    inner = x * (c + (c * 0.044715) * (x * x))
    return x * (1.0 + jnp.tanh(inner)) * v_half


def _ffn_body(x_ref, xprev_ref, xnext_ref, gpre_ref, wg_ref, wv_ref, cwg_ref, cwv_ref, cbg_ref, cbv_ref,
              wd_ref, gpost_ref, o_ref, h_ref, acc_ref, u_ref):
    si = pl.program_id(1)
    j = pl.program_id(2)
    tm = x_ref.shape[1]
    sub = FFN_SUB
    hl = SUBLANES
    n_lane_tiles = 2 * sub // LANES

    @pl.when(j == 0)
    def _():
        g = gpre_ref[...]
        h_ref[:tm] = _rms(x_ref[0], g).astype(BF16)
        nxt = jnp.where(si == pl.num_programs(1) - 1, 0.0, _rms(xnext_ref[0], g))
        prv = jnp.where(si == 0, 0.0, _rms(xprev_ref[0], g))
        h_ref[tm:] = jnp.concatenate([nxt, prv], axis=0).astype(BF16)
        acc_ref[...] = jnp.zeros_like(acc_ref)

    h = h_ref[...]
    n_sub = wd_ref.shape[0] // sub
    cols = [slice(c * sub, (c + 1) * sub) for c in range(n_sub)]

    def gate_value(g_ref, v_ref, c):
        return jnp.concatenate([g_ref[:, cols[c]], v_ref[:, cols[c]]], axis=1)

    def up_project(c):
        u = _dot(h, gate_value(wg_ref, wv_ref, c))
        for k in range(n_lane_tiles):
            lanes = slice(k * LANES, (k + 1) * LANES)
            u_ref[c % 2, k, hl:hl + tm] = u[:tm, lanes]
            u_ref[c % 2, k, :hl] = u[tm + hl:, lanes]
            u_ref[c % 2, k, hl + tm:] = u[tm:tm + hl, lanes]

    def window(c, offset):
        return jnp.concatenate([u_ref[c % 2, k, hl + offset:hl + offset + tm] for k in range(n_lane_tiles)],
                               axis=1)

    def gate_and_down(c):
        cw = gate_value(cwg_ref, cwv_ref, c)
        cu = (window(c, -1) * cw[0:1] + window(c, 0) * cw[1:2] + window(c, 1) * cw[2:3]
              + gate_value(cbg_ref, cbv_ref, c))
        act = _gelu_tanh_times(cu[:, :sub], cu[:, sub:]).astype(BF16)
        acc_ref[...] += _dot(act, wd_ref[cols[c], :])

    up_project(0)
    for c in range(n_sub):
        if c + 1 < n_sub:
            up_project(c + 1)
        gate_and_down(c)

    @pl.when(j == pl.num_programs(2) - 1)
    def _():
        o_ref[0] = x_ref[0] + _rms(acc_ref[...], gpost_ref[...])


def _ffn(x, gpre, w_up, conv_w, conv_b, w_down, gpost, tm, fc):
    b, s, d = x.shape
    nf = D_FF // fc
    halo = SUBLANES
    tb = tm // halo
    last_blk = s // halo - 1
    return pl.pallas_call(
        _ffn_body,
        out_shape=jax.ShapeDtypeStruct((b, s, d), F32),
        grid=(b, s // tm, nf),
        in_specs=[
            pl.BlockSpec((1, tm, d), lambda bi, si, j: (bi, si, 0)),
            pl.BlockSpec((1, halo, d), lambda bi, si, j: (bi, jnp.maximum(si * tb - 1, 0), 0)),
            pl.BlockSpec((1, halo, d), lambda bi, si, j: (bi, jnp.minimum((si + 1) * tb, last_blk), 0)),
            _resident((1, d)),
            pl.BlockSpec((d, fc), lambda bi, si, j: (0, j)),
            pl.BlockSpec((d, fc), lambda bi, si, j: (0, nf + j)),
            pl.BlockSpec((CONV_WIDTH, fc), lambda bi, si, j: (0, j)),
            pl.BlockSpec((CONV_WIDTH, fc), lambda bi, si, j: (0, nf + j)),
            pl.BlockSpec((1, fc), lambda bi, si, j: (0, j)),
            pl.BlockSpec((1, fc), lambda bi, si, j: (0, nf + j)),
            pl.BlockSpec((fc, d), lambda bi, si, j: (j, 0)),
            _resident((1, d)),
        ],
        out_specs=pl.BlockSpec((1, tm, d), lambda bi, si, j: (bi, si, 0)),
        scratch_shapes=[pltpu.VMEM((tm + 2 * halo, d), BF16), pltpu.VMEM((tm, d), F32),
                        pltpu.VMEM((2, 2 * FFN_SUB // LANES, tm + 2 * halo, LANES), F32)],
        compiler_params=_cparams("parallel", "parallel", "arbitrary"),
        name="conv_ffn",
    )(x, x, x, gpre, w_up, w_up, conv_w, conv_w, conv_b, conv_b, w_down, gpost)


def _rotate_half_columns(w):
    k, n = w.shape
    g = w.reshape(k, n // MLA_ROPE, 2, MLA_ROPE // 2)
    return jnp.concatenate([-g[:, :, 1:], g[:, :, :1]], axis=2).reshape(k, n)


def _prepare_weights(p, seq):
    row = lambda v: v.reshape(1, -1).astype(F32)
    w_in = p["w_in"]
    offs, acc = [], 0
    for wdt in (GLA_QK, GLA_QK, GLA_WIDTH, GLA_WIDTH, GLA_GATE_RANK, GLA_GATE_RANK, MLA_RANK, MLA_RANK, MLA_ROPE):
        offs.append((acc, acc + wdt))
        acc += wdt
    part = [w_in[:, a:b] for a, b in offs]
    gq, gk, gv, gr, ggf, ggb, cq, ckv, kr = part
    d = w_in.shape[0]
    gate_pad = jnp.zeros((d, LANES - 2 * GLA_GATE_RANK), w_in.dtype)
    w_in_p = jnp.concatenate([gq, gk, gv, gr, cq, ckv, ggf, ggb, gate_pad, kr, _rotate_half_columns(kr)],
                             axis=1).astype(BF16)

    def gate_w2(w2, slot):
        full = jnp.zeros((LANES, GLA_QK), F32)
        return full.at[slot * GLA_GATE_RANK:(slot + 1) * GLA_GATE_RANK].set(w2).astype(BF16)

    wq = p["mla_w_q_up"].reshape(MLA_RANK, MLA_HEADS, MLA_QK)
    wq_nope = wq[:, :, :MLA_NOPE].reshape(MLA_RANK, MLA_HEADS * MLA_NOPE)
    wq_rope = wq[:, :, MLA_NOPE:].reshape(MLA_RANK, MLA_HEADS * MLA_ROPE)

    half = MLA_ROPE // 2
    freqs = ROPE_BASE ** (-jnp.arange(half, dtype=F32) / half)
    ang = jnp.arange(seq, dtype=F32)[:, None] * freqs[None, :]
    cos_t = jnp.tile(jnp.cos(ang), (1, 2 * MLA_HEADS))
    sin_t = jnp.tile(jnp.sin(ang), (1, 2 * MLA_HEADS))

    value_half = jnp.concatenate([jnp.ones((1, D_FF), F32), jnp.full((1, D_FF), 0.5, F32)], axis=1)

    return dict(
        g_mix_pre=row(p["norm_mix_pre"]), g_mix_post=row(p["norm_mix_post"]), w_in=w_in_p,
        w2f=gate_w2(p["gla_gate_w2_fwd"], 0), b2f=row(p["gla_gate_b_fwd"]),
        w2b=gate_w2(p["gla_gate_w2_bwd"], 1), b2b=row(p["gla_gate_b_bwd"]),
        gla_out_norm=row(p["gla_out_norm"]),
        mla_q_norm=row(p["mla_q_norm"]), mla_kv_norm=row(p["mla_kv_norm"]),
        wq_nope=wq_nope.astype(BF16), wq_rope=wq_rope.astype(BF16),
        wq_rot=_rotate_half_columns(wq_rope).astype(BF16), wkv=p["mla_w_kv_up"].astype(BF16),
        cos=cos_t, sin=sin_t,
        w_out_gla=p["w_out"][:GLA_WIDTH].astype(BF16), w_out_mla=p["w_out"][GLA_WIDTH:].astype(BF16),
        g_mem_pre=row(p["norm_mem_pre"]), g_mem_post=row(p["norm_mem_post"]), g_mem_kv=row(p["mem_kv_norm"]),
        w_mem_q=p["w_mem_q"].astype(BF16), w_mem_o=p["w_mem_o"].astype(BF16),
        w_mem_kv=jnp.concatenate([p["w_mem_k"], p["w_mem_v"]], axis=1).astype(BF16),
        g_ffn_pre=row(p["norm_ffn_pre"]), g_ffn_post=row(p["norm_ffn_post"]),
        w_ffn_up=p["w_ffn_up"].astype(BF16),
        ffn_conv_w=p["ffn_conv_w"].astype(F32) * value_half,
        ffn_conv_b=row(p["ffn_conv_b"]) * value_half,
        w_ffn_down=p["w_ffn_down"].astype(BF16),
    )


def _tiles(seq):
    tm = min(512, seq)
    return dict(tm=tm, gla_ts=tm, mla_ts=tm, mla_tq=tm, mla_tk=tm, ffn_fc=512)


def _apply_layer(x, mem, w):
    b, s, d = x.shape
    t = _tiles(s)
    tm = t["tm"]
    x2 = x.reshape(b * s, d)
    proj = _norm_matmul(x2, w["g_mix_pre"], w["w_in"], tm).reshape(b, s, IN_WIDTH_PADDED)
    o_gla = _gla(proj, w["w2f"], w["b2f"], w["w2b"], w["b2b"], w["gla_out_norm"], t["gla_ts"])
    o_mla = _mla(proj, w["mla_q_norm"], w["mla_kv_norm"], w["wq_nope"], w["wq_rope"], w["wq_rot"], w["wkv"],
                 w["cos"], w["sin"], t["mla_ts"], t["mla_tq"], t["mla_tk"])
    x2 = _mix_out(o_gla.reshape(b * s, GLA_WIDTH), o_mla.reshape(b * s, MLA_WIDTH),
                  w["w_out_gla"], w["w_out_mla"], x2, w["g_mix_post"], tm)
    mtok = mem.shape[1]
    kv = _norm_matmul(mem.reshape(b * mtok, d), w["g_mem_kv"], w["w_mem_kv"], min(tm, b * mtok))
    x3 = _mem_attn(x2.reshape(b, s, d), w["g_mem_pre"], w["w_mem_q"], kv.reshape(b, mtok, 2 * d),
                   w["w_mem_o"], w["g_mem_post"], tm)
    return _ffn(x3, w["g_ffn_pre"], w["w_ffn_up"], w["ffn_conv_w"], w["ffn_conv_b"], w["w_ffn_down"],
                w["g_ffn_post"], tm, t["ffn_fc"])


def kernel(x_prompt, x_sample, mem_prompt, mem_sample, norm_mix_pre, norm_mix_post, w_in, gla_gate_w2_fwd, gla_gate_b_fwd, gla_gate_w2_bwd, gla_gate_b_bwd, gla_out_norm, mla_q_norm, mla_w_q_up, mla_kv_norm, mla_w_kv_up, w_out, norm_mem_pre, norm_mem_post, mem_kv_norm, w_mem_q, w_mem_k, w_mem_v, w_mem_o, norm_ffn_pre, norm_ffn_post, w_ffn_up, ffn_conv_w, ffn_conv_b, w_ffn_down):
    params = dict(
        norm_mix_pre=norm_mix_pre, norm_mix_post=norm_mix_post, w_in=w_in,
        gla_gate_w2_fwd=gla_gate_w2_fwd, gla_gate_b_fwd=gla_gate_b_fwd,
        gla_gate_w2_bwd=gla_gate_w2_bwd, gla_gate_b_bwd=gla_gate_b_bwd, gla_out_norm=gla_out_norm,
        mla_q_norm=mla_q_norm, mla_w_q_up=mla_w_q_up, mla_kv_norm=mla_kv_norm, mla_w_kv_up=mla_w_kv_up,
        w_out=w_out, norm_mem_pre=norm_mem_pre, norm_mem_post=norm_mem_post, mem_kv_norm=mem_kv_norm,
        w_mem_q=w_mem_q, w_mem_k=w_mem_k, w_mem_v=w_mem_v, w_mem_o=w_mem_o,
        norm_ffn_pre=norm_ffn_pre, norm_ffn_post=norm_ffn_post, w_ffn_up=w_ffn_up,
        ffn_conv_w=ffn_conv_w, ffn_conv_b=ffn_conv_b, w_ffn_down=w_ffn_down,
    )
    depth = w_in.shape[0]
    yp, ys = x_prompt, x_sample
    for layer in range(depth):
        p = {k: v[layer] for k, v in params.items()}
        assert yp.shape[1] == ys.shape[1]
        w = _prepare_weights(p, yp.shape[1])
        yp = _apply_layer(yp, mem_prompt, w)
        ys = _apply_layer(ys, mem_sample, w)
    return (yp, ys)
```

```python
import functools
import math

import jax
import jax.numpy as jnp
from jax import lax
from jax.experimental import pallas as pl
from jax.experimental.pallas import tpu as pltpu

F32 = jnp.float32
BF16 = jnp.bfloat16

EPS = 1e-6
D_MODEL = 2048
GLA_HEADS = 4
GLA_DK = 128
GLA_DV = 256
GLA_GATE_RANK = 16
GLA_TAU = 16.0
GLA_CHUNK = 64
GLA_SUB = 256
GLA_QK = GLA_HEADS * GLA_DK
GLA_WIDTH = GLA_HEADS * GLA_DV
MLA_HEADS = 8
MLA_RANK = 512
MLA_NOPE = 128
MLA_ROPE = 64
MLA_V = 128
MLA_QK = MLA_NOPE + MLA_ROPE
MLA_WIDTH = MLA_HEADS * MLA_V
ROPE_BASE = 10000.0
MEM_HEADS = 4
MEM_HEAD_DIM = D_MODEL // MEM_HEADS
D_FF = 5632
CONV_WIDTH = 3
FFN_SUB = 256

LANES = 128
SUBLANES = 8
BF16_ROWS = 16
VMEM_LIMIT_BYTES = 56 * 1024 * 1024
MLA_VT_ROWS = MLA_V + BF16_ROWS
LOG2E = 1.4426950408889634

IN_GQ = 0
IN_GK = IN_GQ + GLA_QK
IN_GV = IN_GK + GLA_QK
IN_GR = IN_GV + GLA_WIDTH
IN_CQ = IN_GR + GLA_WIDTH
IN_CKV = IN_CQ + MLA_RANK
IN_GATE = IN_CKV + MLA_RANK
IN_KR = IN_GATE + LANES
IN_WIDTH_PADDED = IN_KR + LANES


def _cparams(*semantics):
    return pltpu.CompilerParams(dimension_semantics=semantics, vmem_limit_bytes=VMEM_LIMIT_BYTES)


def _resident(shape):
    zeros = (0,) * len(shape)
    return pl.BlockSpec(shape, lambda *_: zeros, pipeline_mode=pl.Buffered(1))


def _rms(x, g):
    return x * lax.rsqrt(jnp.mean(x * x, axis=-1, keepdims=True) + EPS) * g


def _dot(a, b):
    return jnp.dot(a, b, preferred_element_type=F32)


def _dot_nt(a, b):
    return lax.dot_general(a, b, (((1,), (1,)), ((), ())), preferred_element_type=F32)


def _dot_tn(a, b):
    return lax.dot_general(a, b, (((0,), (0,)), ((), ())), preferred_element_type=F32)


def _col_chunk(n):
    for c in (512, 256, LANES):
        if n % c == 0:
            return c
    raise ValueError(f"width {n} is not a multiple of {LANES}")


def _norm_matmul_body(x_ref, g_ref, w_ref, o_ref):
    h = _rms(x_ref[...], g_ref[...]).astype(BF16)
    n = o_ref.shape[-1]
    nc = _col_chunk(n)
    for c in range(0, n, nc):
        o_ref[:, c:c + nc] = _dot(h, w_ref[:, c:c + nc]).astype(o_ref.dtype)


def _norm_matmul(x, g, w, tm):
    t, d = x.shape
    n = w.shape[1]
    return pl.pallas_call(
        _norm_matmul_body,
        out_shape=jax.ShapeDtypeStruct((t, n), BF16),
        grid=(t // tm,),
        in_specs=[pl.BlockSpec((tm, d), lambda i: (i, 0)), _resident((1, d)), _resident((d, n))],
        out_specs=pl.BlockSpec((tm, n), lambda i: (i, 0)),
        compiler_params=_cparams("parallel"),
        name="norm_matmul",
    )(x, g, w)


def _log_sigmoid(x):
    return jnp.minimum(x, 0.0) - jnp.log(1.0 + jnp.exp(-jnp.abs(x)))


def _gla_tables(reverse):
    shift = GLA_CHUNK.bit_length() - 1
    row = lax.broadcasted_iota(jnp.int32, (GLA_SUB, GLA_SUB), 0)
    col = lax.broadcasted_iota(jnp.int32, (GLA_SUB, GLA_SUB), 1)
    same_chunk = lax.shift_right_logical(row, shift) == lax.shift_right_logical(col, shift)
    in_chunk = lambda cond: jnp.where(same_chunk, jnp.where(cond, 1.0, 0.0), 0.0)
    if reverse:
        cum = in_chunk(col >= row)
        keep = in_chunk(col > row) > 0.5
    else:
        cum = in_chunk(col <= row)
        keep = cum > 0.5
    return cum.astype(BF16), keep


def _gla_log_decay(gate_ref, w2_ref, b2_ref, la_ref):
    logits = _dot(gate_ref[0], w2_ref[...]) + b2_ref[...]
    la_ref[...] = _log_sigmoid(logits) * (1.0 / GLA_TAU)


def _gla_block(q_ref, k_ref, v_ref, la_ref, state_ref, reverse):
    ts = q_ref.shape[1]
    c = GLA_CHUNK
    n_chunks = ts // c
    cum_mat, keep = _gla_tables(reverse)
    la = la_ref[...]
    la_hi = la.astype(BF16)
    la_lo = (la - la_hi.astype(F32)).astype(BF16)
    cum = jnp.concatenate(
        [_dot(cum_mat, la_hi[r:r + GLA_SUB]) + _dot(cum_mat, la_lo[r:r + GLA_SUB])
         for r in range(0, ts, GLA_SUB)], axis=0).reshape(n_chunks, c, GLA_QK)
    mid_row, last_row = (c // 2 - 1, 0) if reverse else (c // 2, c - 1)
    mid = cum[:, mid_row:mid_row + 1, :]
    last = cum[:, last_row:last_row + 1, :]
    q = q_ref[0].astype(F32).reshape(n_chunks, c, GLA_QK) * (GLA_DK ** -0.5)
    k = k_ref[0].astype(F32).reshape(n_chunks, c, GLA_QK)
    qi_f = q * jnp.exp(cum - mid)
    ki_f = k * jnp.exp(mid - cum)
    flat = lambda t: t.reshape(ts, GLA_QK).astype(BF16)
    qi = flat(qi_f)
    ki = flat(ki_f)
    qd = flat(qi_f * jnp.exp(mid))
    kd = flat(ki_f * jnp.exp(last - mid))
    chunk_decay = jnp.exp(last)
    order = range(n_chunks - 1, -1, -1) if reverse else range(n_chunks)
    outs = []
    for h in range(GLA_HEADS):
        kc = slice(h * GLA_DK, (h + 1) * GLA_DK)
        v = v_ref[0, :, h * GLA_DV:(h + 1) * GLA_DV]
        intra = []
        for r in range(0, ts, GLA_SUB):
            rows = slice(r, r + GLA_SUB)
            att = jnp.where(keep, _dot_nt(qi[rows, kc], ki[rows, kc]), 0.0).astype(BF16)
            intra.append(_dot(att, v[rows]))
        state = state_ref[h]
        inter = [None] * n_chunks
        for n in order:
            rows = slice(n * c, (n + 1) * c)
            inter[n] = _dot_nt(qd[rows, kc], state.astype(BF16))
            state = state * chunk_decay[n, :, kc] + _dot_tn(v[rows], kd[rows, kc])
        state_ref[h] = state
        outs.append(jnp.concatenate(intra, axis=0) + jnp.concatenate(inter, axis=0))
    return outs


def _gla_fwd_body(q_ref, k_ref, v_ref, gate_ref, w2_ref, b2_ref, o_ref, state_ref, la_ref):
    @pl.when(pl.program_id(1) == 0)
    def _():
        state_ref[...] = jnp.zeros_like(state_ref)

    _gla_log_decay(gate_ref, w2_ref, b2_ref, la_ref)
    outs = _gla_block(q_ref, k_ref, v_ref, la_ref, state_ref, False)
    for h in range(GLA_HEADS):
        o_ref[0, :, h * GLA_DV:(h + 1) * GLA_DV] = outs[h]


def _gla_bwd_body(q_ref, k_ref, v_ref, gate_ref, w2_ref, b2_ref, of_ref, r_ref, gn_ref,
                  o_ref, state_ref, la_ref):
    @pl.when(pl.program_id(1) == 0)
    def _():
        state_ref[...] = jnp.zeros_like(state_ref)

    _gla_log_decay(gate_ref, w2_ref, b2_ref, la_ref)
    outs = _gla_block(q_ref, k_ref, v_ref, la_ref, state_ref, True)
    gn = gn_ref[...]
    for h in range(GLA_HEADS):
        vcols = slice(h * GLA_DV, (h + 1) * GLA_DV)
        o = outs[h] + of_ref[0, :, vcols]
        r = r_ref[0, :, vcols].astype(F32)
        silu = r * (1.0 / (1.0 + jnp.exp(-r)))
        o_ref[0, :, vcols] = (_rms(o, gn) * silu).astype(o_ref.dtype)


def _gla(proj, w2f, b2f, w2b, b2b, out_norm, ts):
    b, s, _ = proj.shape
    ns = s // ts

    def col_spec(width, offset, rev):
        blk = offset // width
        if rev:
            return pl.BlockSpec((1, ts, width), lambda bi, si: (bi, ns - 1 - si, blk))
        return pl.BlockSpec((1, ts, width), lambda bi, si: (bi, si, blk))

    def common_specs(rev):
        return [col_spec(GLA_QK, IN_GQ, rev), col_spec(GLA_QK, IN_GK, rev), col_spec(GLA_WIDTH, IN_GV, rev),
                col_spec(LANES, IN_GATE, rev), _resident((LANES, GLA_QK)), _resident((1, GLA_QK))]

    scratch = [pltpu.VMEM((GLA_HEADS, GLA_DV, GLA_DK), F32), pltpu.VMEM((ts, GLA_QK), F32)]
    o_fwd = pl.pallas_call(
        _gla_fwd_body,
        out_shape=jax.ShapeDtypeStruct((b, s, GLA_WIDTH), F32),
        grid=(b, ns),
        in_specs=common_specs(False),
        out_specs=pl.BlockSpec((1, ts, GLA_WIDTH), lambda bi, si: (bi, si, 0)),
        scratch_shapes=scratch,
        compiler_params=_cparams("parallel", "arbitrary"),
        name="gla_fwd",
    )(proj, proj, proj, proj, w2f, b2f)
    return pl.pallas_call(
        _gla_bwd_body,
        out_shape=jax.ShapeDtypeStruct((b, s, GLA_WIDTH), BF16),
        grid=(b, ns),
        in_specs=common_specs(True) + [
            pl.BlockSpec((1, ts, GLA_WIDTH), lambda bi, si: (bi, ns - 1 - si, 0)),
            col_spec(GLA_WIDTH, IN_GR, True),
            _resident((1, GLA_DV)),
        ],
        out_specs=pl.BlockSpec((1, ts, GLA_WIDTH), lambda bi, si: (bi, ns - 1 - si, 0)),
        scratch_shapes=scratch,
        compiler_params=_cparams("parallel", "arbitrary"),
        name="gla_bwd",
    )(proj, proj, proj, proj, w2b, b2b, o_fwd, proj, out_norm)


def _mla_prep_body(cq_ref, ckv_ref, kr_ref, qn_ref, kvn_ref, wq_nope_ref, wq_rope_ref, wq_rot_ref, wkv_ref,
                   cos_ref, sin_ref, q_out, k_out, vt_out):
    hq = _rms(cq_ref[0].astype(F32), qn_ref[...]).astype(BF16)
    hkv = _rms(ckv_ref[0].astype(F32), kvn_ref[...]).astype(BF16)
    cos = cos_ref[...]
    sin = sin_ref[...]
    q_scale = MLA_QK ** -0.5 * LOG2E
    q_nope = _dot(hq, wq_nope_ref[...]) * q_scale
    q_pe = (_dot(hq, wq_rope_ref[...]) * cos + _dot(hq, wq_rot_ref[...]) * sin) * q_scale
    kv = _dot(hkv, wkv_ref[...])
    kr = kr_ref[0].astype(F32)
    k_pe = (kr[:, :MLA_ROPE] * cos[:, :MLA_ROPE] + kr[:, MLA_ROPE:] * sin[:, :MLA_ROPE]).astype(BF16)
    pad_rows = (MLA_VT_ROWS - MLA_V, cq_ref.shape[1])
    ones_row = jnp.where(lax.broadcasted_iota(jnp.int32, pad_rows, 0) == 0, 1.0, 0.0).astype(BF16)
    for h in range(MLA_HEADS):
        q_out[0, h, :, :MLA_NOPE] = q_nope[:, h * MLA_NOPE:(h + 1) * MLA_NOPE].astype(BF16)
        q_out[0, h, :, MLA_NOPE:] = q_pe[:, h * MLA_ROPE:(h + 1) * MLA_ROPE].astype(BF16)
        base = h * (MLA_NOPE + MLA_V)
        k_out[0, h, :, :MLA_NOPE] = kv[:, base:base + MLA_NOPE].astype(BF16)
        k_out[0, h, :, MLA_NOPE:] = k_pe
        vt_out[0, h, :MLA_V] = kv[:, base + MLA_NOPE:base + MLA_NOPE + MLA_V].T.astype(BF16)
        vt_out[0, h, MLA_V:] = ones_row


def _reduce_rows(op, x, groups=8):
    rows, n = x.shape
    partial = op(x.reshape(groups, rows // groups, n), axis=0)
    return op(partial, axis=0, keepdims=True)


def _mla_attn_body(q_ref, k_ref, vt_ref, o_ref, s_ref, *, tk):
    q = q_ref[0, 0]
    n_chunks = k_ref.shape[2] // tk

    s_ref[...] = _dot_nt(k_ref[0, 0], q)
    m = acc = None
    for c in range(n_chunks):
        s = s_ref[c * tk:(c + 1) * tk]
        chunk_max = _reduce_rows(jnp.max, s)
        m_new = chunk_max if m is None else jnp.maximum(m, chunk_max)
        p = jnp.exp2(s - m_new).astype(BF16)
        pv = _dot(vt_ref[0, 0, :, c * tk:(c + 1) * tk], p)
        acc = pv if acc is None else jnp.exp2(m - m_new) * acc + pv
        m = m_new
    o_ref[0] = (acc[:MLA_V] / acc[MLA_V:MLA_V + 1]).T.astype(o_ref.dtype)


def _mla(proj, q_norm, kv_norm, wq_nope, wq_rope, wq_rot, wkv, cos_t, sin_t, ts, tq, tk):
    b, s, _ = proj.shape
    rope_w = MLA_HEADS * MLA_ROPE
    q, k, vt = pl.pallas_call(
        _mla_prep_body,
        out_shape=(jax.ShapeDtypeStruct((b, MLA_HEADS, s, MLA_QK), BF16),
                   jax.ShapeDtypeStruct((b, MLA_HEADS, s, MLA_QK), BF16),
                   jax.ShapeDtypeStruct((b, MLA_HEADS, MLA_VT_ROWS, s), BF16)),
        grid=(b, s // ts),
        in_specs=[
            pl.BlockSpec((1, ts, MLA_RANK), lambda bi, si: (bi, si, IN_CQ // MLA_RANK)),
            pl.BlockSpec((1, ts, MLA_RANK), lambda bi, si: (bi, si, IN_CKV // MLA_RANK)),
            pl.BlockSpec((1, ts, LANES), lambda bi, si: (bi, si, IN_KR // LANES)),
            _resident((1, MLA_RANK)), _resident((1, MLA_RANK)),
            _resident((MLA_RANK, MLA_HEADS * MLA_NOPE)), _resident((MLA_RANK, rope_w)),
            _resident((MLA_RANK, rope_w)), _resident((MLA_RANK, MLA_HEADS * (MLA_NOPE + MLA_V))),
            pl.BlockSpec((ts, rope_w), lambda bi, si: (si, 0)),
            pl.BlockSpec((ts, rope_w), lambda bi, si: (si, 0)),
        ],
        out_specs=(pl.BlockSpec((1, MLA_HEADS, ts, MLA_QK), lambda bi, si: (bi, 0, si, 0)),
                   pl.BlockSpec((1, MLA_HEADS, ts, MLA_QK), lambda bi, si: (bi, 0, si, 0)),
                   pl.BlockSpec((1, MLA_HEADS, MLA_VT_ROWS, ts), lambda bi, si: (bi, 0, 0, si))),
        compiler_params=_cparams("parallel", "parallel"),
        name="mla_prep",
    )(proj, proj, proj, q_norm, kv_norm, wq_nope, wq_rope, wq_rot, wkv, cos_t, sin_t)
    return pl.pallas_call(
        functools.partial(_mla_attn_body, tk=tk),
        out_shape=jax.ShapeDtypeStruct((b, s, MLA_WIDTH), BF16),
        grid=(b, MLA_HEADS, s // tq),
        in_specs=[
            pl.BlockSpec((1, 1, tq, MLA_QK), lambda bi, hi, qi: (bi, hi, qi, 0)),
            pl.BlockSpec((1, 1, s, MLA_QK), lambda bi, hi, qi: (bi, hi, 0, 0)),
            pl.BlockSpec((1, 1, MLA_VT_ROWS, s), lambda bi, hi, qi: (bi, hi, 0, 0)),
        ],
        out_specs=pl.BlockSpec((1, tq, MLA_V), lambda bi, hi, qi: (bi, qi, hi)),
        scratch_shapes=[pltpu.VMEM((s, tq), F32)],
        compiler_params=_cparams("parallel", "parallel", "parallel"),
        name="mla_attn",
    )(q, k, vt)


def _residual_norm_store(x, o_ref, sumsq, g):
    scale = lax.rsqrt(sumsq * (1.0 / o_ref.shape[-1]) + EPS)
    o_ref[...] = x + o_ref[...] * scale * g


def _row_halves(n):
    return (slice(0, n // 2), slice(n // 2, n))


def _project_rows(o_ref, lhs_and_weights):
    n = o_ref.shape[-1]
    nc = _col_chunk(n)
    sumsq = jnp.zeros((o_ref.shape[0], 1), F32)
    for c in range(0, n, nc):
        y = sum(_dot(a, w_ref[:, c:c + nc]) for a, w_ref in lhs_and_weights)
        sumsq = sumsq + jnp.sum(y * y, axis=-1, keepdims=True)
        o_ref[:, c:c + nc] = y
    return sumsq


def _mix_out_body(a1_ref, a2_ref, w1_ref, w2_ref, x_ref, g_ref, o_ref):
    halves = _row_halves(o_ref.shape[0])
    sumsq = [_project_rows(o_ref.at[rows], [(a1_ref[rows], w1_ref), (a2_ref[rows], w2_ref)]) for rows in halves]
    for rows, ss in zip(halves, sumsq):
        _residual_norm_store(x_ref[rows], o_ref.at[rows], ss, g_ref[...])


def _mix_out(o_gla, o_mla, w1, w2, x, g, tm):
    t, d = x.shape
    return pl.pallas_call(
        _mix_out_body,
        out_shape=jax.ShapeDtypeStruct((t, d), F32),
        grid=(t // tm,),
        in_specs=[
            pl.BlockSpec((tm, GLA_WIDTH), lambda i: (i, 0)),
            pl.BlockSpec((tm, MLA_WIDTH), lambda i: (i, 0)),
            _resident((GLA_WIDTH, d)), _resident((MLA_WIDTH, d)),
            pl.BlockSpec((tm, d), lambda i: (i, 0)),
            _resident((1, d)),
        ],
        out_specs=pl.BlockSpec((tm, d), lambda i: (i, 0)),
        compiler_params=_cparams("parallel"),
        name="mix_out",
    )(o_gla, o_mla, w1, w2, x, g)


def _mem_attn_body(x_ref, gpre_ref, wq_ref, kv_ref, wo_ref, gpost_ref, o_ref, a_ref):
    q_scale = MEM_HEAD_DIM ** -0.5 * LOG2E
    halves = _row_halves(x_ref.shape[1])

    def attend(rows):
        h = _rms(x_ref[0, rows], gpre_ref[...]).astype(BF16)
        for hd in range(MEM_HEADS):
            cols = slice(hd * MEM_HEAD_DIM, (hd + 1) * MEM_HEAD_DIM)
            vcols = slice(D_MODEL + hd * MEM_HEAD_DIM, D_MODEL + (hd + 1) * MEM_HEAD_DIM)
            q = (_dot(h, wq_ref[:, cols]) * q_scale).astype(BF16)
            s = _dot_nt(q, kv_ref[0, :, cols])
            m = jnp.max(s, axis=-1, keepdims=True)
            p = jnp.exp2(s - m)
            l = jnp.sum(p, axis=-1, keepdims=True)
            a_ref[rows, cols] = (_dot(p.astype(BF16), kv_ref[0, :, vcols]) / l).astype(BF16)

    for rows in halves:
        attend(rows)
    sumsq = [_project_rows(o_ref.at[0, rows], [(a_ref[rows], wo_ref)]) for rows in halves]
    for rows, ss in zip(halves, sumsq):
        _residual_norm_store(x_ref[0, rows], o_ref.at[0, rows], ss, gpost_ref[...])


def _mem_attn(x, gpre, wq, kv, wo, gpost, tm):
    b, s, d = x.shape
    m = kv.shape[1]
    return pl.pallas_call(
        _mem_attn_body,
        out_shape=jax.ShapeDtypeStruct((b, s, d), F32),
        grid=(b, s // tm),
        in_specs=[
            pl.BlockSpec((1, tm, d), lambda bi, si: (bi, si, 0)),
            _resident((1, d)), _resident((d, d)),
            pl.BlockSpec((1, m, 2 * d), lambda bi, si: (bi, 0, 0)),
            _resident((d, d)), _resident((1, d)),
        ],
        out_specs=pl.BlockSpec((1, tm, d), lambda bi, si: (bi, si, 0)),
        scratch_shapes=[pltpu.VMEM((tm, d), BF16)],
        compiler_params=_cparams("parallel", "parallel"),
        name="mem_attn",
    )(x, gpre, wq, kv, wo, gpost)


def _gelu_tanh_times(x, v_half):
    c = math.sqrt(2.0 / math.pi)
    inner = x * (c + (c * 0.044715) * (x * x))
    return x * (1.0 + jnp.tanh(inner)) * v_half


def _ffn_body(x_ref, xprev_ref, xnext_ref, gpre_ref, wg_ref, wv_ref, cwg_ref, cwv_ref, cbg_ref, cbv_ref,
              wd_ref, gpost_ref, o_ref, h_ref, acc_ref, u_ref):
    si = pl.program_id(1)
    j = pl.program_id(2)
    tm = x_ref.shape[1]
    sub = FFN_SUB
    hl = SUBLANES
    n_lane_tiles = 2 * sub // LANES

    @pl.when(j == 0)
    def _():
        g = gpre_ref[...]
        h_ref[:tm] = _rms(x_ref[0], g).astype(BF16)
        nxt = jnp.where(si == pl.num_programs(1) - 1, 0.0, _rms(xnext_ref[0], g))
        prv = jnp.where(si == 0, 0.0, _rms(xprev_ref[0], g))
        h_ref[tm:] = jnp.concatenate([nxt, prv], axis=0).astype(BF16)
        acc_ref[...] = jnp.zeros_like(acc_ref)

    h = h_ref[...]
    n_sub = wd_ref.shape[0] // sub
    cols = [slice(c * sub, (c + 1) * sub) for c in range(n_sub)]

    def gate_value(g_ref, v_ref, c):
        return jnp.concatenate([g_ref[:, cols[c]], v_ref[:, cols[c]]], axis=1)

    def up_project(c):
        u = _dot(h, gate_value(wg_ref, wv_ref, c))
        for k in range(n_lane_tiles):
            lanes = slice(k * LANES, (k + 1) * LANES)
            u_ref[c % 2, k, hl:hl + tm] = u[:tm, lanes]
            u_ref[c % 2, k, :hl] = u[tm + hl:, lanes]
            u_ref[c % 2, k, hl + tm:] = u[tm:tm + hl, lanes]

    def window(c, offset):
        return jnp.concatenate([u_ref[c % 2, k, hl + offset:hl + offset + tm] for k in range(n_lane_tiles)],
                               axis=1)

    def gate_and_down(c):
        cw = gate_value(cwg_ref, cwv_ref, c)
        cu = (window(c, -1) * cw[0:1] + window(c, 0) * cw[1:2] + window(c, 1) * cw[2:3]
              + gate_value(cbg_ref, cbv_ref, c))
        act = _gelu_tanh_times(cu[:, :sub], cu[:, sub:]).astype(BF16)
        acc_ref[...] += _dot(act, wd_ref[cols[c], :])

    up_project(0)
    for c in range(n_sub):
        if c + 1 < n_sub:
            up_project(c + 1)
        gate_and_down(c)

    @pl.when(j == pl.num_programs(2) - 1)
    def _():
        o_ref[0] = x_ref[0] + _rms(acc_ref[...], gpost_ref[...])


def _ffn(x, gpre, w_up, conv_w, conv_b, w_down, gpost, tm, fc):
    b, s, d = x.shape
    nf = D_FF // fc
    halo = SUBLANES
    tb = tm // halo
    last_blk = s // halo - 1
    return pl.pallas_call(
        _ffn_body,
        out_shape=jax.ShapeDtypeStruct((b, s, d), F32),
        grid=(b, s // tm, nf),
        in_specs=[
            pl.BlockSpec((1, tm, d), lambda bi, si, j: (bi, si, 0)),
            pl.BlockSpec((1, halo, d), lambda bi, si, j: (bi, jnp.maximum(si * tb - 1, 0), 0)),
            pl.BlockSpec((1, halo, d), lambda bi, si, j: (bi, jnp.minimum((si + 1) * tb, last_blk), 0)),
            _resident((1, d)),
            pl.BlockSpec((d, fc), lambda bi, si, j: (0, j)),
            pl.BlockSpec((d, fc), lambda bi, si, j: (0, nf + j)),
            pl.BlockSpec((CONV_WIDTH, fc), lambda bi, si, j: (0, j)),
            pl.BlockSpec((CONV_WIDTH, fc), lambda bi, si, j: (0, nf + j)),
            pl.BlockSpec((1, fc), lambda bi, si, j: (0, j)),
            pl.BlockSpec((1, fc), lambda bi, si, j: (0, nf + j)),
            pl.BlockSpec((fc, d), lambda bi, si, j: (j, 0)),
            _resident((1, d)),
        ],
        out_specs=pl.BlockSpec((1, tm, d), lambda bi, si, j: (bi, si, 0)),
        scratch_shapes=[pltpu.VMEM((tm + 2 * halo, d), BF16), pltpu.VMEM((tm, d), F32),
                        pltpu.VMEM((2, 2 * FFN_SUB // LANES, tm + 2 * halo, LANES), F32)],
        compiler_params=_cparams("parallel", "parallel", "arbitrary"),
        name="conv_ffn",
    )(x, x, x, gpre, w_up, w_up, conv_w, conv_w, conv_b, conv_b, w_down, gpost)


def _rotate_half_columns(w):
    k, n = w.shape
    g = w.reshape(k, n // MLA_ROPE, 2, MLA_ROPE // 2)
    return jnp.concatenate([-g[:, :, 1:], g[:, :, :1]], axis=2).reshape(k, n)


def _prepare_weights(p, seq):
    row = lambda v: v.reshape(1, -1).astype(F32)
    w_in = p["w_in"]
    offs, acc = [], 0
    for wdt in (GLA_QK, GLA_QK, GLA_WIDTH, GLA_WIDTH, GLA_GATE_RANK, GLA_GATE_RANK, MLA_RANK, MLA_RANK, MLA_ROPE):
        offs.append((acc, acc + wdt))
        acc += wdt
    part = [w_in[:, a:b] for a, b in offs]
    gq, gk, gv, gr, ggf, ggb, cq, ckv, kr = part
    d = w_in.shape[0]
    gate_pad = jnp.zeros((d, LANES - 2 * GLA_GATE_RANK), w_in.dtype)
    w_in_p = jnp.concatenate([gq, gk, gv, gr, cq, ckv, ggf, ggb, gate_pad, kr, _rotate_half_columns(kr)],
                             axis=1).astype(BF16)

    def gate_w2(w2, slot):
        full = jnp.zeros((LANES, GLA_QK), F32)
        return full.at[slot * GLA_GATE_RANK:(slot + 1) * GLA_GATE_RANK].set(w2).astype(BF16)

    wq = p["mla_w_q_up"].reshape(MLA_RANK, MLA_HEADS, MLA_QK)
    wq_nope = wq[:, :, :MLA_NOPE].reshape(MLA_RANK, MLA_HEADS * MLA_NOPE)
    wq_rope = wq[:, :, MLA_NOPE:].reshape(MLA_RANK, MLA_HEADS * MLA_ROPE)

    half = MLA_ROPE // 2
    freqs = ROPE_BASE ** (-jnp.arange(half, dtype=F32) / half)
    ang = jnp.arange(seq, dtype=F32)[:, None] * freqs[None, :]
    cos_t = jnp.tile(jnp.cos(ang), (1, 2 * MLA_HEADS))
    sin_t = jnp.tile(jnp.sin(ang), (1, 2 * MLA_HEADS))

    value_half = jnp.concatenate([jnp.ones((1, D_FF), F32), jnp.full((1, D_FF), 0.5, F32)], axis=1)

    return dict(
        g_mix_pre=row(p["norm_mix_pre"]), g_mix_post=row(p["norm_mix_post"]), w_in=w_in_p,
        w2f=gate_w2(p["gla_gate_w2_fwd"], 0), b2f=row(p["gla_gate_b_fwd"]),
        w2b=gate_w2(p["gla_gate_w2_bwd"], 1), b2b=row(p["gla_gate_b_bwd"]),
        gla_out_norm=row(p["gla_out_norm"]),
        mla_q_norm=row(p["mla_q_norm"]), mla_kv_norm=row(p["mla_kv_norm"]),
        wq_nope=wq_nope.astype(BF16), wq_rope=wq_rope.astype(BF16),
        wq_rot=_rotate_half_columns(wq_rope).astype(BF16), wkv=p["mla_w_kv_up"].astype(BF16),
        cos=cos_t, sin=sin_t,
        w_out_gla=p["w_out"][:GLA_WIDTH].astype(BF16), w_out_mla=p["w_out"][GLA_WIDTH:].astype(BF16),
        g_mem_pre=row(p["norm_mem_pre"]), g_mem_post=row(p["norm_mem_post"]), g_mem_kv=row(p["mem_kv_norm"]),
        w_mem_q=p["w_mem_q"].astype(BF16), w_mem_o=p["w_mem_o"].astype(BF16),
        w_mem_kv=jnp.concatenate([p["w_mem_k"], p["w_mem_v"]], axis=1).astype(BF16),
        g_ffn_pre=row(p["norm_ffn_pre"]), g_ffn_post=row(p["norm_ffn_post"]),
        w_ffn_up=p["w_ffn_up"].astype(BF16),
        ffn_conv_w=p["ffn_conv_w"].astype(F32) * value_half,
        ffn_conv_b=row(p["ffn_conv_b"]) * value_half,
        w_ffn_down=p["w_ffn_down"].astype(BF16),
    )


def _tiles(seq):
    tm = min(512, seq)
    return dict(tm=tm, gla_ts=tm, mla_ts=tm, mla_tq=tm, mla_tk=tm, ffn_fc=512)


def _apply_layer(x, mem, w):
    b, s, d = x.shape
    t = _tiles(s)
    tm = t["tm"]
    x2 = x.reshape(b * s, d)
    proj = _norm_matmul(x2, w["g_mix_pre"], w["w_in"], tm).reshape(b, s, IN_WIDTH_PADDED)
    o_gla = _gla(proj, w["w2f"], w["b2f"], w["w2b"], w["b2b"], w["gla_out_norm"], t["gla_ts"])
    o_mla = _mla(proj, w["mla_q_norm"], w["mla_kv_norm"], w["wq_nope"], w["wq_rope"], w["wq_rot"], w["wkv"],
                 w["cos"], w["sin"], t["mla_ts"], t["mla_tq"], t["mla_tk"])
    x2 = _mix_out(o_gla.reshape(b * s, GLA_WIDTH), o_mla.reshape(b * s, MLA_WIDTH),
                  w["w_out_gla"], w["w_out_mla"], x2, w["g_mix_post"], tm)
    mtok = mem.shape[1]
    kv = _norm_matmul(mem.reshape(b * mtok, d), w["g_mem_kv"], w["w_mem_kv"], min(tm, b * mtok))
    x3 = _mem_attn(x2.reshape(b, s, d), w["g_mem_pre"], w["w_mem_q"], kv.reshape(b, mtok, 2 * d),
                   w["w_mem_o"], w["g_mem_post"], tm)
    return _ffn(x3, w["g_ffn_pre"], w["w_ffn_up"], w["ffn_conv_w"], w["ffn_conv_b"], w["w_ffn_down"],
                w["g_ffn_post"], tm, t["ffn_fc"])


def kernel(x_prompt, x_sample, mem_prompt, mem_sample, norm_mix_pre, norm_mix_post, w_in, gla_gate_w2_fwd, gla_gate_b_fwd, gla_gate_w2_bwd, gla_gate_b_bwd, gla_out_norm, mla_q_norm, mla_w_q_up, mla_kv_norm, mla_w_kv_up, w_out, norm_mem_pre, norm_mem_post, mem_kv_norm, w_mem_q, w_mem_k, w_mem_v, w_mem_o, norm_ffn_pre, norm_ffn_post, w_ffn_up, ffn_conv_w, ffn_conv_b, w_ffn_down):
    params = dict(
        norm_mix_pre=norm_mix_pre, norm_mix_post=norm_mix_post, w_in=w_in,
        gla_gate_w2_fwd=gla_gate_w2_fwd, gla_gate_b_fwd=gla_gate_b_fwd,
        gla_gate_w2_bwd=gla_gate_w2_bwd, gla_gate_b_bwd=gla_gate_b_bwd, gla_out_norm=gla_out_norm,
        mla_q_norm=mla_q_norm, mla_w_q_up=mla_w_q_up, mla_kv_norm=mla_kv_norm, mla_w_kv_up=mla_w_kv_up,
        w_out=w_out, norm_mem_pre=norm_mem_pre, norm_mem_post=norm_mem_post, mem_kv_norm=mem_kv_norm,
        w_mem_q=w_mem_q, w_mem_k=w_mem_k, w_mem_v=w_mem_v, w_mem_o=w_mem_o,
        norm_ffn_pre=norm_ffn_pre, norm_ffn_post=norm_ffn_post, w_ffn_up=w_ffn_up,
        ffn_conv_w=ffn_conv_w, ffn_conv_b=ffn_conv_b, w_ffn_down=w_ffn_down,
    )
    depth = w_in.shape[0]
    yp, ys = x_prompt, x_sample
    for layer in range(depth):
        p = {k: v[layer] for k, v in params.items()}
        assert yp.shape[1] == ys.shape[1]
        w = _prepare_weights(p, yp.shape[1])
        yp = _apply_layer(yp, mem_prompt, w)
        ys = _apply_layer(ys, mem_sample, w)
    return (yp, ys)
```

```python
import functools
import math

import jax
import jax.numpy as jnp
from jax import lax
from jax.experimental import pallas as pl
from jax.experimental.pallas import tpu as pltpu

F32 = jnp.float32
BF16 = jnp.bfloat16

EPS = 1e-6
D_MODEL = 2048
GLA_HEADS = 4
GLA_DK = 128
GLA_DV = 256
GLA_GATE_RANK = 16
GLA_TAU = 16.0
GLA_CHUNK = 64
GLA_SUB = 256
GLA_QK = GLA_HEADS * GLA_DK
GLA_WIDTH = GLA_HEADS * GLA_DV
MLA_HEADS = 8
MLA_RANK = 512
MLA_NOPE = 128
MLA_ROPE = 64
MLA_V = 128
MLA_QK = MLA_NOPE + MLA_ROPE
MLA_WIDTH = MLA_HEADS * MLA_V
ROPE_BASE = 10000.0
MEM_HEADS = 4
MEM_HEAD_DIM = D_MODEL // MEM_HEADS
D_FF = 5632
CONV_WIDTH = 3
FFN_SUB = 256

LANES = 128
SUBLANES = 8
BF16_ROWS = 16
VMEM_LIMIT_BYTES = 56 * 1024 * 1024
MLA_VT_ROWS = MLA_V + BF16_ROWS
LOG2E = 1.4426950408889634

IN_GQ = 0
IN_GK = IN_GQ + GLA_QK
IN_GV = IN_GK + GLA_QK
IN_GR = IN_GV + GLA_WIDTH
IN_CQ = IN_GR + GLA_WIDTH
IN_CKV = IN_CQ + MLA_RANK
IN_GATE = IN_CKV + MLA_RANK
IN_KR = IN_GATE + LANES
IN_WIDTH_PADDED = IN_KR + LANES


def _cparams(*semantics):
    return pltpu.CompilerParams(dimension_semantics=semantics, vmem_limit_bytes=VMEM_LIMIT_BYTES)


def _resident(shape):
    zeros = (0,) * len(shape)
    return pl.BlockSpec(shape, lambda *_: zeros, pipeline_mode=pl.Buffered(1))


def _rms(x, g):
    return x * lax.rsqrt(jnp.mean(x * x, axis=-1, keepdims=True) + EPS) * g


def _dot(a, b):
    return jnp.dot(a, b, preferred_element_type=F32)


def _dot_nt(a, b):
    return lax.dot_general(a, b, (((1,), (1,)), ((), ())), preferred_element_type=F32)


def _dot_tn(a, b):
    return lax.dot_general(a, b, (((0,), (0,)), ((), ())), preferred_element_type=F32)


def _col_chunk(n):
    for c in (512, 256, LANES):
        if n % c == 0:
            return c
    raise ValueError(f"width {n} is not a multiple of {LANES}")


def _norm_matmul_body(x_ref, g_ref, w_ref, o_ref):
    h = _rms(x_ref[...], g_ref[...]).astype(BF16)
    n = o_ref.shape[-1]
    nc = _col_chunk(n)
    for c in range(0, n, nc):
        o_ref[:, c:c + nc] = _dot(h, w_ref[:, c:c + nc]).astype(o_ref.dtype)


def _norm_matmul(x, g, w, tm):
    t, d = x.shape
    n = w.shape[1]
    return pl.pallas_call(
        _norm_matmul_body,
        out_shape=jax.ShapeDtypeStruct((t, n), BF16),
        grid=(t // tm,),
        in_specs=[pl.BlockSpec((tm, d), lambda i: (i, 0)), _resident((1, d)), _resident((d, n))],
        out_specs=pl.BlockSpec((tm, n), lambda i: (i, 0)),
        compiler_params=_cparams("parallel"),
        name="norm_matmul",
    )(x, g, w)


def _log_sigmoid(x):
    return jnp.minimum(x, 0.0) - jnp.log(1.0 + jnp.exp(-jnp.abs(x)))


def _gla_tables(reverse):
    shift = GLA_CHUNK.bit_length() - 1
    row = lax.broadcasted_iota(jnp.int32, (GLA_SUB, GLA_SUB), 0)
    col = lax.broadcasted_iota(jnp.int32, (GLA_SUB, GLA_SUB), 1)
    same_chunk = lax.shift_right_logical(row, shift) == lax.shift_right_logical(col, shift)
    in_chunk = lambda cond: jnp.where(same_chunk, jnp.where(cond, 1.0, 0.0), 0.0)
    if reverse:
        cum = in_chunk(col >= row)
        keep = in_chunk(col > row) > 0.5
    else:
        cum = in_chunk(col <= row)
        keep = cum > 0.5
    return cum.astype(BF16), keep


def _gla_log_decay(gate_ref, w2_ref, b2_ref, la_ref):
    logits = _dot(gate_ref[0], w2_ref[...]) + b2_ref[...]
    la_ref[...] = _log_sigmoid(logits) * (1.0 / GLA_TAU)


def _gla_block(q_ref, k_ref, v_ref, la_ref, state_ref, reverse):
    ts = q_ref.shape[1]
    c = GLA_CHUNK
    n_chunks = ts // c
    cum_mat, keep = _gla_tables(reverse)
    la = la_ref[...]
    la_hi = la.astype(BF16)
    la_lo = (la - la_hi.astype(F32)).astype(BF16)
    cum = jnp.concatenate(
        [_dot(cum_mat, la_hi[r:r + GLA_SUB]) + _dot(cum_mat, la_lo[r:r + GLA_SUB])
         for r in range(0, ts, GLA_SUB)], axis=0).reshape(n_chunks, c, GLA_QK)
    mid_row, last_row = (c // 2 - 1, 0) if reverse else (c // 2, c - 1)
    mid = cum[:, mid_row:mid_row + 1, :]
    last = cum[:, last_row:last_row + 1, :]
    q = q_ref[0].astype(F32).reshape(n_chunks, c, GLA_QK) * (GLA_DK ** -0.5)
    k = k_ref[0].astype(F32).reshape(n_chunks, c, GLA_QK)
    qi_f = q * jnp.exp(cum - mid)
    ki_f = k * jnp.exp(mid - cum)
    flat = lambda t: t.reshape(ts, GLA_QK).astype(BF16)
    qi = flat(qi_f)
    ki = flat(ki_f)
    qd = flat(qi_f * jnp.exp(mid))
    kd = flat(ki_f * jnp.exp(last - mid))
    chunk_decay = jnp.exp(last)
    order = range(n_chunks - 1, -1, -1) if reverse else range(n_chunks)
    outs = []
    for h in range(GLA_HEADS):
        kc = slice(h * GLA_DK, (h + 1) * GLA_DK)
        v = v_ref[0, :, h * GLA_DV:(h + 1) * GLA_DV]
        intra = []
        for r in range(0, ts, GLA_SUB):
            rows = slice(r, r + GLA_SUB)
            att = jnp.where(keep, _dot_nt(qi[rows, kc], ki[rows, kc]), 0.0).astype(BF16)
            intra.append(_dot(att, v[rows]))
        state = state_ref[h]
        inter = [None] * n_chunks
        for n in order:
            rows = slice(n * c, (n + 1) * c)
            inter[n] = _dot_nt(qd[rows, kc], state.astype(BF16))
            state = state * chunk_decay[n, :, kc] + _dot_tn(v[rows], kd[rows, kc])
        state_ref[h] = state
        outs.append(jnp.concatenate(intra, axis=0) + jnp.concatenate(inter, axis=0))
    return outs


def _gla_fwd_body(q_ref, k_ref, v_ref, gate_ref, w2_ref, b2_ref, o_ref, state_ref, la_ref):
    @pl.when(pl.program_id(1) == 0)
    def _():
        state_ref[...] = jnp.zeros_like(state_ref)

    _gla_log_decay(gate_ref, w2_ref, b2_ref, la_ref)
    outs = _gla_block(q_ref, k_ref, v_ref, la_ref, state_ref, False)
    for h in range(GLA_HEADS):
        o_ref[0, :, h * GLA_DV:(h + 1) * GLA_DV] = outs[h]


def _gla_bwd_body(q_ref, k_ref, v_ref, gate_ref, w2_ref, b2_ref, of_ref, r_ref, gn_ref,
                  o_ref, state_ref, la_ref):
    @pl.when(pl.program_id(1) == 0)
    def _():
        state_ref[...] = jnp.zeros_like(state_ref)

    _gla_log_decay(gate_ref, w2_ref, b2_ref, la_ref)
    outs = _gla_block(q_ref, k_ref, v_ref, la_ref, state_ref, True)
    gn = gn_ref[...]
    for h in range(GLA_HEADS):
        vcols = slice(h * GLA_DV, (h + 1) * GLA_DV)
        o = outs[h] + of_ref[0, :, vcols]
        r = r_ref[0, :, vcols].astype(F32)
        silu = r * (1.0 / (1.0 + jnp.exp(-r)))
        o_ref[0, :, vcols] = (_rms(o, gn) * silu).astype(o_ref.dtype)


def _gla(proj, w2f, b2f, w2b, b2b, out_norm, ts):
    b, s, _ = proj.shape
    ns = s // ts

    def col_spec(width, offset, rev):
        blk = offset // width
        if rev:
            return pl.BlockSpec((1, ts, width), lambda bi, si: (bi, ns - 1 - si, blk))
        return pl.BlockSpec((1, ts, width), lambda bi, si: (bi, si, blk))

    def common_specs(rev):
        return [col_spec(GLA_QK, IN_GQ, rev), col_spec(GLA_QK, IN_GK, rev), col_spec(GLA_WIDTH, IN_GV, rev),
                col_spec(LANES, IN_GATE, rev), _resident((LANES, GLA_QK)), _resident((1, GLA_QK))]

    scratch = [pltpu.VMEM((GLA_HEADS, GLA_DV, GLA_DK), F32), pltpu.VMEM((ts, GLA_QK), F32)]
    o_fwd = pl.pallas_call(
        _gla_fwd_body,
        out_shape=jax.ShapeDtypeStruct((b, s, GLA_WIDTH), F32),
        grid=(b, ns),
        in_specs=common_specs(False),
        out_specs=pl.BlockSpec((1, ts, GLA_WIDTH), lambda bi, si: (bi, si, 0)),
        scratch_shapes=scratch,
        compiler_params=_cparams("parallel", "arbitrary"),
        name="gla_fwd",
    )(proj, proj, proj, proj, w2f, b2f)
    return pl.pallas_call(
        _gla_bwd_body,
        out_shape=jax.ShapeDtypeStruct((b, s, GLA_WIDTH), BF16),
        grid=(b, ns),
        in_specs=common_specs(True) + [
            pl.BlockSpec((1, ts, GLA_WIDTH), lambda bi, si: (bi, ns - 1 - si, 0)),
            col_spec(GLA_WIDTH, IN_GR, True),
            _resident((1, GLA_DV)),
        ],
        out_specs=pl.BlockSpec((1, ts, GLA_WIDTH), lambda bi, si: (bi, ns - 1 - si, 0)),
        scratch_shapes=scratch,
        compiler_params=_cparams("parallel", "arbitrary"),
        name="gla_bwd",
    )(proj, proj, proj, proj, w2b, b2b, o_fwd, proj, out_norm)


def _mla_prep_body(cq_ref, ckv_ref, kr_ref, qn_ref, kvn_ref, wq_nope_ref, wq_rope_ref, wq_rot_ref, wkv_ref,
                   cos_ref, sin_ref, q_out, k_out, vt_out):
    hq = _rms(cq_ref[0].astype(F32), qn_ref[...]).astype(BF16)
    hkv = _rms(ckv_ref[0].astype(F32), kvn_ref[...]).astype(BF16)
    cos = cos_ref[...]
    sin = sin_ref[...]
    q_scale = MLA_QK ** -0.5 * LOG2E
    q_nope = _dot(hq, wq_nope_ref[...]) * q_scale
    q_pe = (_dot(hq, wq_rope_ref[...]) * cos + _dot(hq, wq_rot_ref[...]) * sin) * q_scale
    kv = _dot(hkv, wkv_ref[...])
    kr = kr_ref[0].astype(F32)
    k_pe = (kr[:, :MLA_ROPE] * cos[:, :MLA_ROPE] + kr[:, MLA_ROPE:] * sin[:, :MLA_ROPE]).astype(BF16)
    pad_rows = (MLA_VT_ROWS - MLA_V, cq_ref.shape[1])
    ones_row = jnp.where(lax.broadcasted_iota(jnp.int32, pad_rows, 0) == 0, 1.0, 0.0).astype(BF16)
    for h in range(MLA_HEADS):
        q_out[0, h, :, :MLA_NOPE] = q_nope[:, h * MLA_NOPE:(h + 1) * MLA_NOPE].astype(BF16)
        q_out[0, h, :, MLA_NOPE:] = q_pe[:, h * MLA_ROPE:(h + 1) * MLA_ROPE].astype(BF16)
        base = h * (MLA_NOPE + MLA_V)
        k_out[0, h, :, :MLA_NOPE] = kv[:, base:base + MLA_NOPE].astype(BF16)
        k_out[0, h, :, MLA_NOPE:] = k_pe
        vt_out[0, h, :MLA_V] = kv[:, base + MLA_NOPE:base + MLA_NOPE + MLA_V].T.astype(BF16)
        vt_out[0, h, MLA_V:] = ones_row


def _reduce_rows(op, x, groups=8):
    rows, n = x.shape
    partial = op(x.reshape(groups, rows // groups, n), axis=0)
    return op(partial, axis=0, keepdims=True)


def _mla_attn_body(q_ref, k_ref, vt_ref, o_ref, s_ref, *, tk):
    q = q_ref[0, 0]
    n_chunks = k_ref.shape[2] // tk

    s_ref[...] = _dot_nt(k_ref[0, 0], q)
    m = acc = None
    for c in range(n_chunks):
        s = s_ref[c * tk:(c + 1) * tk]
        chunk_max = _reduce_rows(jnp.max, s)
        m_new = chunk_max if m is None else jnp.maximum(m, chunk_max)
        p = jnp.exp2(s - m_new).astype(BF16)
        pv = _dot(vt_ref[0, 0, :, c * tk:(c + 1) * tk], p)
        acc = pv if acc is None else jnp.exp2(m - m_new) * acc + pv
        m = m_new
    o_ref[0] = (acc[:MLA_V] / acc[MLA_V:MLA_V + 1]).T.astype(o_ref.dtype)


def _mla(proj, q_norm, kv_norm, wq_nope, wq_rope, wq_rot, wkv, cos_t, sin_t, ts, tq, tk):
    b, s, _ = proj.shape
    rope_w = MLA_HEADS * MLA_ROPE
    q, k, vt = pl.pallas_call(
        _mla_prep_body,
        out_shape=(jax.ShapeDtypeStruct((b, MLA_HEADS, s, MLA_QK), BF16),
                   jax.ShapeDtypeStruct((b, MLA_HEADS, s, MLA_QK), BF16),
                   jax.ShapeDtypeStruct((b, MLA_HEADS, MLA_VT_ROWS, s), BF16)),
        grid=(b, s // ts),
        in_specs=[
            pl.BlockSpec((1, ts, MLA_RANK), lambda bi, si: (bi, si, IN_CQ // MLA_RANK)),
            pl.BlockSpec((1, ts, MLA_RANK), lambda bi, si: (bi, si, IN_CKV // MLA_RANK)),
            pl.BlockSpec((1, ts, LANES), lambda bi, si: (bi, si, IN_KR // LANES)),
            _resident((1, MLA_RANK)), _resident((1, MLA_RANK)),
            _resident((MLA_RANK, MLA_HEADS * MLA_NOPE)), _resident((MLA_RANK, rope_w)),
            _resident((MLA_RANK, rope_w)), _resident((MLA_RANK, MLA_HEADS * (MLA_NOPE + MLA_V))),
            pl.BlockSpec((ts, rope_w), lambda bi, si: (si, 0)),
            pl.BlockSpec((ts, rope_w), lambda bi, si: (si, 0)),
        ],
        out_specs=(pl.BlockSpec((1, MLA_HEADS, ts, MLA_QK), lambda bi, si: (bi, 0, si, 0)),
                   pl.BlockSpec((1, MLA_HEADS, ts, MLA_QK), lambda bi, si: (bi, 0, si, 0)),
                   pl.BlockSpec((1, MLA_HEADS, MLA_VT_ROWS, ts), lambda bi, si: (bi, 0, 0, si))),
        compiler_params=_cparams("parallel", "parallel"),
        name="mla_prep",
    )(proj, proj, proj, q_norm, kv_norm, wq_nope, wq_rope, wq_rot, wkv, cos_t, sin_t)
    return pl.pallas_call(
        functools.partial(_mla_attn_body, tk=tk),
        out_shape=jax.ShapeDtypeStruct((b, s, MLA_WIDTH), BF16),
        grid=(b, MLA_HEADS, s // tq),
        in_specs=[
            pl.BlockSpec((1, 1, tq, MLA_QK), lambda bi, hi, qi: (bi, hi, qi, 0)),
            pl.BlockSpec((1, 1, s, MLA_QK), lambda bi, hi, qi: (bi, hi, 0, 0)),
            pl.BlockSpec((1, 1, MLA_VT_ROWS, s), lambda bi, hi, qi: (bi, hi, 0, 0)),
        ],
        out_specs=pl.BlockSpec((1, tq, MLA_V), lambda bi, hi, qi: (bi, qi, hi)),
        scratch_shapes=[pltpu.VMEM((s, tq), F32)],
        compiler_params=_cparams("parallel", "parallel", "parallel"),
        name="mla_attn",
    )(q, k, vt)


def _residual_norm_store(x, o_ref, sumsq, g):
    scale = lax.rsqrt(sumsq * (1.0 / o_ref.shape[-1]) + EPS)
    o_ref[...] = x + o_ref[...] * scale * g


def _project_rows(o_ref, lhs_and_weights):
    n = o_ref.shape[-1]
    nc = _col_chunk(n)
    sumsq = jnp.zeros((o_ref.shape[0], 1), F32)
    for c in range(0, n, nc):
        y = sum(_dot(a, w_ref[:, c:c + nc]) for a, w_ref in lhs_and_weights)
        sumsq = sumsq + jnp.sum(y * y, axis=-1, keepdims=True)
        o_ref[:, c:c + nc] = y
    return sumsq


def _mix_out_body(a1_ref, a2_ref, w1_ref, w2_ref, x_ref, g_ref, o_ref):
    sumsq = _project_rows(o_ref, [(a1_ref[...], w1_ref), (a2_ref[...], w2_ref)])
    _residual_norm_store(x_ref[...], o_ref, sumsq, g_ref[...])


def _mix_out(o_gla, o_mla, w1, w2, x, g, tm):
    t, d = x.shape
    return pl.pallas_call(
        _mix_out_body,
        out_shape=jax.ShapeDtypeStruct((t, d), F32),
        grid=(t // tm,),
        in_specs=[
            pl.BlockSpec((tm, GLA_WIDTH), lambda i: (i, 0)),
            pl.BlockSpec((tm, MLA_WIDTH), lambda i: (i, 0)),
            _resident((GLA_WIDTH, d)), _resident((MLA_WIDTH, d)),
            pl.BlockSpec((tm, d), lambda i: (i, 0)),
            _resident((1, d)),
        ],
        out_specs=pl.BlockSpec((tm, d), lambda i: (i, 0)),
        compiler_params=_cparams("parallel"),
        name="mix_out",
    )(o_gla, o_mla, w1, w2, x, g)


def _mem_attn_body(x_ref, gpre_ref, wq_ref, kv_ref, wo_ref, gpost_ref, o_ref, a_ref):
    q_scale = MEM_HEAD_DIM ** -0.5 * LOG2E
    x = x_ref[0]
    h = _rms(x, gpre_ref[...]).astype(BF16)
    for hd in range(MEM_HEADS):
        cols = slice(hd * MEM_HEAD_DIM, (hd + 1) * MEM_HEAD_DIM)
        vcols = slice(D_MODEL + hd * MEM_HEAD_DIM, D_MODEL + (hd + 1) * MEM_HEAD_DIM)
        q = (_dot(h, wq_ref[:, cols]) * q_scale).astype(BF16)
        s = _dot_nt(q, kv_ref[0, :, cols])
        m = jnp.max(s, axis=-1, keepdims=True)
        p = jnp.exp2(s - m)
        l = jnp.sum(p, axis=-1, keepdims=True)
        a_ref[:, cols] = (_dot(p.astype(BF16), kv_ref[0, :, vcols]) / l).astype(BF16)
    sumsq = _project_rows(o_ref.at[0], [(a_ref[...], wo_ref)])
    _residual_norm_store(x, o_ref.at[0], sumsq, gpost_ref[...])


def _mem_attn(x, gpre, wq, kv, wo, gpost, tm):
    b, s, d = x.shape
    m = kv.shape[1]
    return pl.pallas_call(
        _mem_attn_body,
        out_shape=jax.ShapeDtypeStruct((b, s, d), F32),
        grid=(b, s // tm),
        in_specs=[
            pl.BlockSpec((1, tm, d), lambda bi, si: (bi, si, 0)),
            _resident((1, d)), _resident((d, d)),
            pl.BlockSpec((1, m, 2 * d), lambda bi, si: (bi, 0, 0)),
            _resident((d, d)), _resident((1, d)),
        ],
        out_specs=pl.BlockSpec((1, tm, d), lambda bi, si: (bi, si, 0)),
        scratch_shapes=[pltpu.VMEM((tm, d), BF16)],
        compiler_params=_cparams("parallel", "parallel"),
        name="mem_attn",
    )(x, gpre, wq, kv, wo, gpost)


def _gelu_tanh_times(x, v_half):
    c = math.sqrt(2.0 / math.pi)
    inner = x * (c + (c * 0.044715) * (x * x))
    return x * (1.0 + jnp.tanh(inner)) * v_half


def _ffn_body(x_ref, xprev_ref, xnext_ref, gpre_ref, wg_ref, wv_ref, cwg_ref, cwv_ref, cbg_ref, cbv_ref,
              wd_ref, gpost_ref, o_ref, h_ref, u_ref):
    si = pl.program_id(1)
    j = pl.program_id(2)
    tm = x_ref.shape[1]
    sub = FFN_SUB
    hl = SUBLANES
    n_lane_tiles = 2 * sub // LANES

    @pl.when(j == 0)
    def _():
        g = gpre_ref[...]
        h_ref[:tm] = _rms(x_ref[0], g).astype(BF16)
        nxt = jnp.where(si == pl.num_programs(1) - 1, 0.0, _rms(xnext_ref[0], g))
        prv = jnp.where(si == 0, 0.0, _rms(xprev_ref[0], g))
        h_ref[tm:] = jnp.concatenate([nxt, prv], axis=0).astype(BF16)
        o_ref[...] = jnp.zeros_like(o_ref)

    h = h_ref[...]
    n_sub = wd_ref.shape[0] // sub
    cols = [slice(c * sub, (c + 1) * sub) for c in range(n_sub)]

    def gate_value(g_ref, v_ref, c):
        return jnp.concatenate([g_ref[:, cols[c]], v_ref[:, cols[c]]], axis=1)

    def up_project(c):
        u = _dot(h, gate_value(wg_ref, wv_ref, c))
        for k in range(n_lane_tiles):
            lanes = slice(k * LANES, (k + 1) * LANES)
            u_ref[c % 2, k, hl:hl + tm] = u[:tm, lanes]
            u_ref[c % 2, k, :hl] = u[tm + hl:, lanes]
            u_ref[c % 2, k, hl + tm:] = u[tm:tm + hl, lanes]

    def window(c, offset):
        return jnp.concatenate([u_ref[c % 2, k, hl + offset:hl + offset + tm] for k in range(n_lane_tiles)],
                               axis=1)

    def gate_and_down(c):
        cw = gate_value(cwg_ref, cwv_ref, c)
        cu = (window(c, -1) * cw[0:1] + window(c, 0) * cw[1:2] + window(c, 1) * cw[2:3]
              + gate_value(cbg_ref, cbv_ref, c))
        act = _gelu_tanh_times(cu[:, :sub], cu[:, sub:]).astype(BF16)
        n = o_ref.shape[-1]
        nc = _col_chunk(n)
        for k in range(0, n, nc):
            o_ref[0, :, k:k + nc] += _dot(act, wd_ref[cols[c], k:k + nc])

    up_project(0)
    for c in range(n_sub):
        if c + 1 < n_sub:
            up_project(c + 1)
        gate_and_down(c)

    @pl.when(j == pl.num_programs(2) - 1)
    def _():
        o_ref[0] = x_ref[0] + _rms(o_ref[0], gpost_ref[...])


def _ffn(x, gpre, w_up, conv_w, conv_b, w_down, gpost, tm, fc):
    b, s, d = x.shape
    nf = D_FF // fc
    halo = SUBLANES
    tb = tm // halo
    last_blk = s // halo - 1
    return pl.pallas_call(
        _ffn_body,
        out_shape=jax.ShapeDtypeStruct((b, s, d), F32),
        grid=(b, s // tm, nf),
        in_specs=[
            pl.BlockSpec((1, tm, d), lambda bi, si, j: (bi, si, 0)),
            pl.BlockSpec((1, halo, d), lambda bi, si, j: (bi, jnp.maximum(si * tb - 1, 0), 0)),
            pl.BlockSpec((1, halo, d), lambda bi, si, j: (bi, jnp.minimum((si + 1) * tb, last_blk), 0)),
            _resident((1, d)),
            pl.BlockSpec((d, fc), lambda bi, si, j: (0, j)),
            pl.BlockSpec((d, fc), lambda bi, si, j: (0, nf + j)),
            pl.BlockSpec((CONV_WIDTH, fc), lambda bi, si, j: (0, j)),
            pl.BlockSpec((CONV_WIDTH, fc), lambda bi, si, j: (0, nf + j)),
            pl.BlockSpec((1, fc), lambda bi, si, j: (0, j)),
            pl.BlockSpec((1, fc), lambda bi, si, j: (0, nf + j)),
            pl.BlockSpec((fc, d), lambda bi, si, j: (j, 0)),
            _resident((1, d)),
        ],
        out_specs=pl.BlockSpec((1, tm, d), lambda bi, si, j: (bi, si, 0)),
        scratch_shapes=[pltpu.VMEM((tm + 2 * halo, d), BF16),
                        pltpu.VMEM((2, 2 * FFN_SUB // LANES, tm + 2 * halo, LANES), F32)],
        compiler_params=_cparams("parallel", "parallel", "arbitrary"),
        name="conv_ffn",
    )(x, x, x, gpre, w_up, w_up, conv_w, conv_w, conv_b, conv_b, w_down, gpost)


def _rotate_half_columns(w):
    k, n = w.shape
    g = w.reshape(k, n // MLA_ROPE, 2, MLA_ROPE // 2)
    return jnp.concatenate([-g[:, :, 1:], g[:, :, :1]], axis=2).reshape(k, n)


def _prepare_weights(p, seq):
    row = lambda v: v.reshape(1, -1).astype(F32)
    w_in = p["w_in"]
    offs, acc = [], 0
    for wdt in (GLA_QK, GLA_QK, GLA_WIDTH, GLA_WIDTH, GLA_GATE_RANK, GLA_GATE_RANK, MLA_RANK, MLA_RANK, MLA_ROPE):
        offs.append((acc, acc + wdt))
        acc += wdt
    part = [w_in[:, a:b] for a, b in offs]
    gq, gk, gv, gr, ggf, ggb, cq, ckv, kr = part
    d = w_in.shape[0]
    gate_pad = jnp.zeros((d, LANES - 2 * GLA_GATE_RANK), w_in.dtype)
    w_in_p = jnp.concatenate([gq, gk, gv, gr, cq, ckv, ggf, ggb, gate_pad, kr, _rotate_half_columns(kr)],
                             axis=1).astype(BF16)

    def gate_w2(w2, slot):
        full = jnp.zeros((LANES, GLA_QK), F32)
        return full.at[slot * GLA_GATE_RANK:(slot + 1) * GLA_GATE_RANK].set(w2).astype(BF16)

    wq = p["mla_w_q_up"].reshape(MLA_RANK, MLA_HEADS, MLA_QK)
    wq_nope = wq[:, :, :MLA_NOPE].reshape(MLA_RANK, MLA_HEADS * MLA_NOPE)
    wq_rope = wq[:, :, MLA_NOPE:].reshape(MLA_RANK, MLA_HEADS * MLA_ROPE)

    half = MLA_ROPE // 2
    freqs = ROPE_BASE ** (-jnp.arange(half, dtype=F32) / half)
    ang = jnp.arange(seq, dtype=F32)[:, None] * freqs[None, :]
    cos_t = jnp.tile(jnp.cos(ang), (1, 2 * MLA_HEADS))
    sin_t = jnp.tile(jnp.sin(ang), (1, 2 * MLA_HEADS))

    value_half = jnp.concatenate([jnp.ones((1, D_FF), F32), jnp.full((1, D_FF), 0.5, F32)], axis=1)

    return dict(
        g_mix_pre=row(p["norm_mix_pre"]), g_mix_post=row(p["norm_mix_post"]), w_in=w_in_p,
        w2f=gate_w2(p["gla_gate_w2_fwd"], 0), b2f=row(p["gla_gate_b_fwd"]),
        w2b=gate_w2(p["gla_gate_w2_bwd"], 1), b2b=row(p["gla_gate_b_bwd"]),
        gla_out_norm=row(p["gla_out_norm"]),
        mla_q_norm=row(p["mla_q_norm"]), mla_kv_norm=row(p["mla_kv_norm"]),
        wq_nope=wq_nope.astype(BF16), wq_rope=wq_rope.astype(BF16),
        wq_rot=_rotate_half_columns(wq_rope).astype(BF16), wkv=p["mla_w_kv_up"].astype(BF16),
        cos=cos_t, sin=sin_t,
        w_out_gla=p["w_out"][:GLA_WIDTH].astype(BF16), w_out_mla=p["w_out"][GLA_WIDTH:].astype(BF16),
        g_mem_pre=row(p["norm_mem_pre"]), g_mem_post=row(p["norm_mem_post"]), g_mem_kv=row(p["mem_kv_norm"]),
        w_mem_q=p["w_mem_q"].astype(BF16), w_mem_o=p["w_mem_o"].astype(BF16),
        w_mem_kv=jnp.concatenate([p["w_mem_k"], p["w_mem_v"]], axis=1).astype(BF16),
        g_ffn_pre=row(p["norm_ffn_pre"]), g_ffn_post=row(p["norm_ffn_post"]),
        w_ffn_up=p["w_ffn_up"].astype(BF16),
        ffn_conv_w=p["ffn_conv_w"].astype(F32) * value_half,
        ffn_conv_b=row(p["ffn_conv_b"]) * value_half,
        w_ffn_down=p["w_ffn_down"].astype(BF16),
    )


def _tiles(seq):
    tm = min(512, seq)
    return dict(tm=tm, gla_ts=tm, mla_ts=tm, mla_tq=min(1024, seq), mla_tk=tm, ffn_tm=tm, ffn_fc=512)


def _apply_layer(x, mem, w):
    b, s, d = x.shape
    t = _tiles(s)
    tm = t["tm"]
    x2 = x.reshape(b * s, d)
    proj = _norm_matmul(x2, w["g_mix_pre"], w["w_in"], tm).reshape(b, s, IN_WIDTH_PADDED)
    o_gla = _gla(proj, w["w2f"], w["b2f"], w["w2b"], w["b2b"], w["gla_out_norm"], t["gla_ts"])
    o_mla = _mla(proj, w["mla_q_norm"], w["mla_kv_norm"], w["wq_nope"], w["wq_rope"], w["wq_rot"], w["wkv"],
                 w["cos"], w["sin"], t["mla_ts"], t["mla_tq"], t["mla_tk"])
    x2 = _mix_out(o_gla.reshape(b * s, GLA_WIDTH), o_mla.reshape(b * s, MLA_WIDTH),
                  w["w_out_gla"], w["w_out_mla"], x2, w["g_mix_post"], tm)
    mtok = mem.shape[1]
    kv = _norm_matmul(mem.reshape(b * mtok, d), w["g_mem_kv"], w["w_mem_kv"], min(tm, b * mtok))
    x3 = _mem_attn(x2.reshape(b, s, d), w["g_mem_pre"], w["w_mem_q"], kv.reshape(b, mtok, 2 * d),
                   w["w_mem_o"], w["g_mem_post"], tm)
    return _ffn(x3, w["g_ffn_pre"], w["w_ffn_up"], w["ffn_conv_w"], w["ffn_conv_b"], w["w_ffn_down"],
                w["g_ffn_post"], t["ffn_tm"], t["ffn_fc"])


def kernel(x_prompt, x_sample, mem_prompt, mem_sample, norm_mix_pre, norm_mix_post, w_in, gla_gate_w2_fwd, gla_gate_b_fwd, gla_gate_w2_bwd, gla_gate_b_bwd, gla_out_norm, mla_q_norm, mla_w_q_up, mla_kv_norm, mla_w_kv_up, w_out, norm_mem_pre, norm_mem_post, mem_kv_norm, w_mem_q, w_mem_k, w_mem_v, w_mem_o, norm_ffn_pre, norm_ffn_post, w_ffn_up, ffn_conv_w, ffn_conv_b, w_ffn_down):
    params = dict(
        norm_mix_pre=norm_mix_pre, norm_mix_post=norm_mix_post, w_in=w_in,
        gla_gate_w2_fwd=gla_gate_w2_fwd, gla_gate_b_fwd=gla_gate_b_fwd,
        gla_gate_w2_bwd=gla_gate_w2_bwd, gla_gate_b_bwd=gla_gate_b_bwd, gla_out_norm=gla_out_norm,
        mla_q_norm=mla_q_norm, mla_w_q_up=mla_w_q_up, mla_kv_norm=mla_kv_norm, mla_w_kv_up=mla_w_kv_up,
        w_out=w_out, norm_mem_pre=norm_mem_pre, norm_mem_post=norm_mem_post, mem_kv_norm=mem_kv_norm,
        w_mem_q=w_mem_q, w_mem_k=w_mem_k, w_mem_v=w_mem_v, w_mem_o=w_mem_o,
        norm_ffn_pre=norm_ffn_pre, norm_ffn_post=norm_ffn_post, w_ffn_up=w_ffn_up,
        ffn_conv_w=ffn_conv_w, ffn_conv_b=ffn_conv_b, w_ffn_down=w_ffn_down,
    )
    depth = w_in.shape[0]
    yp, ys = x_prompt, x_sample
    for layer in range(depth):
        p = {k: v[layer] for k, v in params.items()}
        assert yp.shape[1] == ys.shape[1]
        w = _prepare_weights(p, yp.shape[1])
        yp = _apply_layer(yp, mem_prompt, w)
        ys = _apply_layer(ys, mem_sample, w)
    return (yp, ys)
```

```python
import functools
import math

import jax
import jax.numpy as jnp
from jax import lax
from jax.experimental import pallas as pl
from jax.experimental.pallas import tpu as pltpu

F32 = jnp.float32
BF16 = jnp.bfloat16

EPS = 1e-6
D_MODEL = 2048
GLA_HEADS = 4
GLA_DK = 128
GLA_DV = 256
GLA_GATE_RANK = 16
GLA_TAU = 16.0
GLA_CHUNK = 64
GLA_SUB = 256
GLA_QK = GLA_HEADS * GLA_DK
GLA_WIDTH = GLA_HEADS * GLA_DV
MLA_HEADS = 8
MLA_RANK = 512
MLA_NOPE = 128
MLA_ROPE = 64
MLA_V = 128
MLA_QK = MLA_NOPE + MLA_ROPE
MLA_WIDTH = MLA_HEADS * MLA_V
ROPE_BASE = 10000.0
MEM_HEADS = 4
MEM_HEAD_DIM = D_MODEL // MEM_HEADS
D_FF = 5632
CONV_WIDTH = 3
FFN_SUB = 256

LANES = 128
SUBLANES = 8
BF16_ROWS = 16
VMEM_LIMIT_BYTES = 56 * 1024 * 1024
MLA_VT_ROWS = MLA_V + BF16_ROWS
LOG2E = 1.4426950408889634

IN_GQ = 0
IN_GK = IN_GQ + GLA_QK
IN_GV = IN_GK + GLA_QK
IN_GR = IN_GV + GLA_WIDTH
IN_CQ = IN_GR + GLA_WIDTH
IN_CKV = IN_CQ + MLA_RANK
IN_GATE = IN_CKV + MLA_RANK
IN_KR = IN_GATE + LANES
IN_WIDTH_PADDED = IN_KR + LANES


def _cparams(*semantics):
    return pltpu.CompilerParams(dimension_semantics=semantics, vmem_limit_bytes=VMEM_LIMIT_BYTES)


def _resident(shape):
    zeros = (0,) * len(shape)
    return pl.BlockSpec(shape, lambda *_: zeros, pipeline_mode=pl.Buffered(1))


def _rms(x, g):
    return x * lax.rsqrt(jnp.mean(x * x, axis=-1, keepdims=True) + EPS) * g


def _dot(a, b):
    return jnp.dot(a, b, preferred_element_type=F32)


def _dot_nt(a, b):
    return lax.dot_general(a, b, (((1,), (1,)), ((), ())), preferred_element_type=F32)


def _dot_tn(a, b):
    return lax.dot_general(a, b, (((0,), (0,)), ((), ())), preferred_element_type=F32)


def _col_chunk(n):
    for c in (512, 256, LANES):
        if n % c == 0:
            return c
    raise ValueError(f"width {n} is not a multiple of {LANES}")


def _norm_matmul_body(x_ref, g_ref, w_ref, o_ref):
    h = _rms(x_ref[...], g_ref[...]).astype(BF16)
    n = o_ref.shape[-1]
    nc = _col_chunk(n)
    for c in range(0, n, nc):
        o_ref[:, c:c + nc] = _dot(h, w_ref[:, c:c + nc]).astype(o_ref.dtype)


def _norm_matmul(x, g, w, tm):
    t, d = x.shape
    n = w.shape[1]
    return pl.pallas_call(
        _norm_matmul_body,
        out_shape=jax.ShapeDtypeStruct((t, n), BF16),
        grid=(t // tm,),
        in_specs=[pl.BlockSpec((tm, d), lambda i: (i, 0)), _resident((1, d)), _resident((d, n))],
        out_specs=pl.BlockSpec((tm, n), lambda i: (i, 0)),
        compiler_params=_cparams("parallel"),
        name="norm_matmul",
    )(x, g, w)


def _log_sigmoid(x):
    return jnp.minimum(x, 0.0) - jnp.log(1.0 + jnp.exp(-jnp.abs(x)))


def _gla_tables(reverse):
    shift = GLA_CHUNK.bit_length() - 1
    row = lax.broadcasted_iota(jnp.int32, (GLA_SUB, GLA_SUB), 0)
    col = lax.broadcasted_iota(jnp.int32, (GLA_SUB, GLA_SUB), 1)
    same_chunk = lax.shift_right_logical(row, shift) == lax.shift_right_logical(col, shift)
    in_chunk = lambda cond: jnp.where(same_chunk, jnp.where(cond, 1.0, 0.0), 0.0)
    if reverse:
        cum = in_chunk(col >= row)
        keep = in_chunk(col > row) > 0.5
    else:
        cum = in_chunk(col <= row)
        keep = cum > 0.5
    return cum.astype(BF16), keep


def _gla_log_decay(gate_ref, w2_ref, b2_ref, la_ref):
    logits = _dot(gate_ref[0], w2_ref[...]) + b2_ref[...]
    la_ref[...] = _log_sigmoid(logits) * (1.0 / GLA_TAU)


def _gla_block(q_ref, k_ref, v_ref, la_ref, state_ref, reverse):
    ts = q_ref.shape[1]
    c = GLA_CHUNK
    n_chunks = ts // c
    cum_mat, keep = _gla_tables(reverse)
    la = la_ref[...]
    la_hi = la.astype(BF16)
    la_lo = (la - la_hi.astype(F32)).astype(BF16)
    cum = jnp.concatenate(
        [_dot(cum_mat, la_hi[r:r + GLA_SUB]) + _dot(cum_mat, la_lo[r:r + GLA_SUB])
         for r in range(0, ts, GLA_SUB)], axis=0).reshape(n_chunks, c, GLA_QK)
    mid_row, last_row = (c // 2 - 1, 0) if reverse else (c // 2, c - 1)
    mid = cum[:, mid_row:mid_row + 1, :]
    last = cum[:, last_row:last_row + 1, :]
    q = q_ref[0].astype(F32).reshape(n_chunks, c, GLA_QK) * (GLA_DK ** -0.5)
    k = k_ref[0].astype(F32).reshape(n_chunks, c, GLA_QK)
    qi_f = q * jnp.exp(cum - mid)
    ki_f = k * jnp.exp(mid - cum)
    flat = lambda t: t.reshape(ts, GLA_QK).astype(BF16)
    qi = flat(qi_f)
    ki = flat(ki_f)
    qd = flat(qi_f * jnp.exp(mid))
    kd = flat(ki_f * jnp.exp(last - mid))
    chunk_decay = jnp.exp(last)
    order = range(n_chunks - 1, -1, -1) if reverse else range(n_chunks)
    outs = []
    for h in range(GLA_HEADS):
        kc = slice(h * GLA_DK, (h + 1) * GLA_DK)
        v = v_ref[0, :, h * GLA_DV:(h + 1) * GLA_DV]
        intra = []
        for r in range(0, ts, GLA_SUB):
            rows = slice(r, r + GLA_SUB)
            att = jnp.where(keep, _dot_nt(qi[rows, kc], ki[rows, kc]), 0.0).astype(BF16)
            intra.append(_dot(att, v[rows]))
        state = state_ref[h]
        inter = [None] * n_chunks
        for n in order:
            rows = slice(n * c, (n + 1) * c)
            inter[n] = _dot_nt(qd[rows, kc], state.astype(BF16))
            state = state * chunk_decay[n, :, kc] + _dot_tn(v[rows], kd[rows, kc])
        state_ref[h] = state
        outs.append(jnp.concatenate(intra, axis=0) + jnp.concatenate(inter, axis=0))
    return outs


def _gla_fwd_body(q_ref, k_ref, v_ref, gate_ref, w2_ref, b2_ref, o_ref, state_ref, la_ref):
    @pl.when(pl.program_id(1) == 0)
    def _():
        state_ref[...] = jnp.zeros_like(state_ref)

    _gla_log_decay(gate_ref, w2_ref, b2_ref, la_ref)
    outs = _gla_block(q_ref, k_ref, v_ref, la_ref, state_ref, False)
    for h in range(GLA_HEADS):
        o_ref[0, :, h * GLA_DV:(h + 1) * GLA_DV] = outs[h]


def _gla_bwd_body(q_ref, k_ref, v_ref, gate_ref, w2_ref, b2_ref, of_ref, r_ref, gn_ref,
                  o_ref, state_ref, la_ref):
    @pl.when(pl.program_id(1) == 0)
    def _():
        state_ref[...] = jnp.zeros_like(state_ref)

    _gla_log_decay(gate_ref, w2_ref, b2_ref, la_ref)
    outs = _gla_block(q_ref, k_ref, v_ref, la_ref, state_ref, True)
    gn = gn_ref[...]
    for h in range(GLA_HEADS):
        vcols = slice(h * GLA_DV, (h + 1) * GLA_DV)
        o = outs[h] + of_ref[0, :, vcols]
        r = r_ref[0, :, vcols].astype(F32)
        silu = r * (1.0 / (1.0 + jnp.exp(-r)))
        o_ref[0, :, vcols] = (_rms(o, gn) * silu).astype(o_ref.dtype)


def _gla(proj, w2f, b2f, w2b, b2b, out_norm, ts):
    b, s, _ = proj.shape
    ns = s // ts

    def col_spec(width, offset, rev):
        blk = offset // width
        if rev:
            return pl.BlockSpec((1, ts, width), lambda bi, si: (bi, ns - 1 - si, blk))
        return pl.BlockSpec((1, ts, width), lambda bi, si: (bi, si, blk))

    def common_specs(rev):
        return [col_spec(GLA_QK, IN_GQ, rev), col_spec(GLA_QK, IN_GK, rev), col_spec(GLA_WIDTH, IN_GV, rev),
                col_spec(LANES, IN_GATE, rev), _resident((LANES, GLA_QK)), _resident((1, GLA_QK))]

    scratch = [pltpu.VMEM((GLA_HEADS, GLA_DV, GLA_DK), F32), pltpu.VMEM((ts, GLA_QK), F32)]
    o_fwd = pl.pallas_call(
        _gla_fwd_body,
        out_shape=jax.ShapeDtypeStruct((b, s, GLA_WIDTH), F32),
        grid=(b, ns),
        in_specs=common_specs(False),
        out_specs=pl.BlockSpec((1, ts, GLA_WIDTH), lambda bi, si: (bi, si, 0)),
        scratch_shapes=scratch,
        compiler_params=_cparams("parallel", "arbitrary"),
        name="gla_fwd",
    )(proj, proj, proj, proj, w2f, b2f)
    return pl.pallas_call(
        _gla_bwd_body,
        out_shape=jax.ShapeDtypeStruct((b, s, GLA_WIDTH), BF16),
        grid=(b, ns),
        in_specs=common_specs(True) + [
            pl.BlockSpec((1, ts, GLA_WIDTH), lambda bi, si: (bi, ns - 1 - si, 0)),
            col_spec(GLA_WIDTH, IN_GR, True),
            _resident((1, GLA_DV)),
        ],
        out_specs=pl.BlockSpec((1, ts, GLA_WIDTH), lambda bi, si: (bi, ns - 1 - si, 0)),
        scratch_shapes=scratch,
        compiler_params=_cparams("parallel", "arbitrary"),
        name="gla_bwd",
    )(proj, proj, proj, proj, w2b, b2b, o_fwd, proj, out_norm)


def _mla_prep_body(cq_ref, ckv_ref, kr_ref, qn_ref, kvn_ref, wq_nope_ref, wq_rope_ref, wq_rot_ref, wkv_ref,
                   cos_ref, sin_ref, q_out, k_out, vt_out):
    hq = _rms(cq_ref[0].astype(F32), qn_ref[...]).astype(BF16)
    hkv = _rms(ckv_ref[0].astype(F32), kvn_ref[...]).astype(BF16)
    cos = cos_ref[...]
    sin = sin_ref[...]
    q_scale = MLA_QK ** -0.5 * LOG2E
    q_nope = _dot(hq, wq_nope_ref[...]) * q_scale
    q_pe = (_dot(hq, wq_rope_ref[...]) * cos + _dot(hq, wq_rot_ref[...]) * sin) * q_scale
    kv = _dot(hkv, wkv_ref[...])
    kr = kr_ref[0].astype(F32)
    k_pe = (kr[:, :MLA_ROPE] * cos[:, :MLA_ROPE] + kr[:, MLA_ROPE:] * sin[:, :MLA_ROPE]).astype(BF16)
    pad_rows = (MLA_VT_ROWS - MLA_V, cq_ref.shape[1])
    ones_row = jnp.where(lax.broadcasted_iota(jnp.int32, pad_rows, 0) == 0, 1.0, 0.0).astype(BF16)
    for h in range(MLA_HEADS):
        q_out[0, h, :, :MLA_NOPE] = q_nope[:, h * MLA_NOPE:(h + 1) * MLA_NOPE].astype(BF16)
        q_out[0, h, :, MLA_NOPE:] = q_pe[:, h * MLA_ROPE:(h + 1) * MLA_ROPE].astype(BF16)
        base = h * (MLA_NOPE + MLA_V)
        k_out[0, h, :, :MLA_NOPE] = kv[:, base:base + MLA_NOPE].astype(BF16)
        k_out[0, h, :, MLA_NOPE:] = k_pe
        vt_out[0, h, :MLA_V] = kv[:, base + MLA_NOPE:base + MLA_NOPE + MLA_V].T.astype(BF16)
        vt_out[0, h, MLA_V:] = ones_row


def _reduce_rows(op, x, groups=8):
    rows, n = x.shape
    partial = op(x.reshape(groups, rows // groups, n), axis=0)
    return op(partial, axis=0, keepdims=True)


def _mla_attn_body(q_ref, k_ref, vt_ref, o_ref, s_ref, *, tk):
    q = q_ref[0, 0]
    n_chunks = k_ref.shape[2] // tk

    s_ref[...] = _dot_nt(k_ref[0, 0], q)
    m = acc = None
    for c in range(n_chunks):
        s = s_ref[c * tk:(c + 1) * tk]
        chunk_max = _reduce_rows(jnp.max, s)
        m_new = chunk_max if m is None else jnp.maximum(m, chunk_max)
        p = jnp.exp2(s - m_new).astype(BF16)
        pv = _dot(vt_ref[0, 0, :, c * tk:(c + 1) * tk], p)
        acc = pv if acc is None else jnp.exp2(m - m_new) * acc + pv
        m = m_new
    o_ref[0] = (acc[:MLA_V] / acc[MLA_V:MLA_V + 1]).T.astype(o_ref.dtype)


def _mla(proj, q_norm, kv_norm, wq_nope, wq_rope, wq_rot, wkv, cos_t, sin_t, ts, tq, tk):
    b, s, _ = proj.shape
    rope_w = MLA_HEADS * MLA_ROPE
    q, k, vt = pl.pallas_call(
        _mla_prep_body,
        out_shape=(jax.ShapeDtypeStruct((b, MLA_HEADS, s, MLA_QK), BF16),
                   jax.ShapeDtypeStruct((b, MLA_HEADS, s, MLA_QK), BF16),
                   jax.ShapeDtypeStruct((b, MLA_HEADS, MLA_VT_ROWS, s), BF16)),
        grid=(b, s // ts),
        in_specs=[
            pl.BlockSpec((1, ts, MLA_RANK), lambda bi, si: (bi, si, IN_CQ // MLA_RANK)),
            pl.BlockSpec((1, ts, MLA_RANK), lambda bi, si: (bi, si, IN_CKV // MLA_RANK)),
            pl.BlockSpec((1, ts, LANES), lambda bi, si: (bi, si, IN_KR // LANES)),
            _resident((1, MLA_RANK)), _resident((1, MLA_RANK)),
            _resident((MLA_RANK, MLA_HEADS * MLA_NOPE)), _resident((MLA_RANK, rope_w)),
            _resident((MLA_RANK, rope_w)), _resident((MLA_RANK, MLA_HEADS * (MLA_NOPE + MLA_V))),
            pl.BlockSpec((ts, rope_w), lambda bi, si: (si, 0)),
            pl.BlockSpec((ts, rope_w), lambda bi, si: (si, 0)),
        ],
        out_specs=(pl.BlockSpec((1, MLA_HEADS, ts, MLA_QK), lambda bi, si: (bi, 0, si, 0)),
                   pl.BlockSpec((1, MLA_HEADS, ts, MLA_QK), lambda bi, si: (bi, 0, si, 0)),
                   pl.BlockSpec((1, MLA_HEADS, MLA_VT_ROWS, ts), lambda bi, si: (bi, 0, 0, si))),
        compiler_params=_cparams("parallel", "parallel"),
        name="mla_prep",
    )(proj, proj, proj, q_norm, kv_norm, wq_nope, wq_rope, wq_rot, wkv, cos_t, sin_t)
    return pl.pallas_call(
        functools.partial(_mla_attn_body, tk=tk),
        out_shape=jax.ShapeDtypeStruct((b, s, MLA_WIDTH), BF16),
        grid=(b, MLA_HEADS, s // tq),
        in_specs=[
            pl.BlockSpec((1, 1, tq, MLA_QK), lambda bi, hi, qi: (bi, hi, qi, 0)),
            pl.BlockSpec((1, 1, s, MLA_QK), lambda bi, hi, qi: (bi, hi, 0, 0)),
            pl.BlockSpec((1, 1, MLA_VT_ROWS, s), lambda bi, hi, qi: (bi, hi, 0, 0)),
        ],
        out_specs=pl.BlockSpec((1, tq, MLA_V), lambda bi, hi, qi: (bi, qi, hi)),
        scratch_shapes=[pltpu.VMEM((s, tq), F32)],
        compiler_params=_cparams("parallel", "parallel", "parallel"),
        name="mla_attn",
    )(q, k, vt)


def _residual_norm_store(x, o_ref, sumsq, g):
    scale = lax.rsqrt(sumsq * (1.0 / o_ref.shape[-1]) + EPS)
    o_ref[...] = x + o_ref[...] * scale * g


def _project_rows(o_ref, lhs_and_weights):
    n = o_ref.shape[-1]
    nc = _col_chunk(n)
    sumsq = jnp.zeros((o_ref.shape[0], 1), F32)
    for c in range(0, n, nc):
        y = sum(_dot(a, w_ref[:, c:c + nc]) for a, w_ref in lhs_and_weights)
        sumsq = sumsq + jnp.sum(y * y, axis=-1, keepdims=True)
        o_ref[:, c:c + nc] = y
    return sumsq


def _mix_out_body(a1_ref, a2_ref, w1_ref, w2_ref, x_ref, g_ref, o_ref):
    sumsq = _project_rows(o_ref, [(a1_ref[...], w1_ref), (a2_ref[...], w2_ref)])
    _residual_norm_store(x_ref[...], o_ref, sumsq, g_ref[...])


def _mix_out(o_gla, o_mla, w1, w2, x, g, tm):
    t, d = x.shape
    return pl.pallas_call(
        _mix_out_body,
        out_shape=jax.ShapeDtypeStruct((t, d), F32),
        grid=(t // tm,),
        in_specs=[
            pl.BlockSpec((tm, GLA_WIDTH), lambda i: (i, 0)),
            pl.BlockSpec((tm, MLA_WIDTH), lambda i: (i, 0)),
            _resident((GLA_WIDTH, d)), _resident((MLA_WIDTH, d)),
            pl.BlockSpec((tm, d), lambda i: (i, 0)),
            _resident((1, d)),
        ],
        out_specs=pl.BlockSpec((tm, d), lambda i: (i, 0)),
        compiler_params=_cparams("parallel"),
        name="mix_out",
    )(o_gla, o_mla, w1, w2, x, g)


def _mem_attn_body(x_ref, gpre_ref, wq_ref, kv_ref, wo_ref, gpost_ref, o_ref, a_ref):
    q_scale = MEM_HEAD_DIM ** -0.5 * LOG2E
    x = x_ref[0]
    h = _rms(x, gpre_ref[...]).astype(BF16)
    for hd in range(MEM_HEADS):
        cols = slice(hd * MEM_HEAD_DIM, (hd + 1) * MEM_HEAD_DIM)
        vcols = slice(D_MODEL + hd * MEM_HEAD_DIM, D_MODEL + (hd + 1) * MEM_HEAD_DIM)
        q = (_dot(h, wq_ref[:, cols]) * q_scale).astype(BF16)
        s = _dot_nt(q, kv_ref[0, :, cols])
        m = jnp.max(s, axis=-1, keepdims=True)
        p = jnp.exp2(s - m)
        l = jnp.sum(p, axis=-1, keepdims=True)
        a_ref[:, cols] = (_dot(p.astype(BF16), kv_ref[0, :, vcols]) / l).astype(BF16)
    sumsq = _project_rows(o_ref.at[0], [(a_ref[...], wo_ref)])
    _residual_norm_store(x, o_ref.at[0], sumsq, gpost_ref[...])


def _mem_attn(x, gpre, wq, kv, wo, gpost, tm):
    b, s, d = x.shape
    m = kv.shape[1]
    return pl.pallas_call(
        _mem_attn_body,
        out_shape=jax.ShapeDtypeStruct((b, s, d), F32),
        grid=(b, s // tm),
        in_specs=[
            pl.BlockSpec((1, tm, d), lambda bi, si: (bi, si, 0)),
            _resident((1, d)), _resident((d, d)),
            pl.BlockSpec((1, m, 2 * d), lambda bi, si: (bi, 0, 0)),
            _resident((d, d)), _resident((1, d)),
        ],
        out_specs=pl.BlockSpec((1, tm, d), lambda bi, si: (bi, si, 0)),
        scratch_shapes=[pltpu.VMEM((tm, d), BF16)],
        compiler_params=_cparams("parallel", "parallel"),
        name="mem_attn",
    )(x, gpre, wq, kv, wo, gpost)


def _gelu_tanh_times(x, v_half):
    c = math.sqrt(2.0 / math.pi)
    inner = x * (c + (c * 0.044715) * (x * x))
    return x * (1.0 + jnp.tanh(inner)) * v_half


def _ffn_body(x_ref, xprev_ref, xnext_ref, gpre_ref, wga_ref, wva_ref, wda_ref, wgb_ref, wvb_ref, wdb_ref,
              conv_ref, gpost_ref, o_ref, h_ref, u_ref, *, n_chunks):
    si = pl.program_id(1)
    j = pl.program_id(2)
    n_steps = pl.num_programs(2)
    tm = x_ref.shape[1]
    fc = wda_ref.shape[0]
    sub = FFN_SUB
    hl = SUBLANES
    n_lane_tiles = 2 * sub // LANES
    n_out = o_ref.shape[-1]
    out_chunk = _col_chunk(n_out)

    @pl.when(j == 0)
    def _():
        g = gpre_ref[...]
        h_ref[:tm] = _rms(x_ref[0], g).astype(BF16)
        nxt = jnp.where(si == pl.num_programs(1) - 1, 0.0, _rms(xnext_ref[0], g))
        prv = jnp.where(si == 0, 0.0, _rms(xprev_ref[0], g))
        h_ref[tm:] = jnp.concatenate([nxt, prv], axis=0).astype(BF16)
        o_ref[...] = jnp.zeros_like(o_ref)

    def window(slot, offset):
        return jnp.concatenate([u_ref[slot, k, hl + offset:hl + offset + tm] for k in range(n_lane_tiles)],
                               axis=1)

    def run(chunks):
        h = h_ref[...]
        subs = [(chunk, refs, slice(c * sub, (c + 1) * sub), c * sub)
                for chunk, refs in chunks for c in range(fc // sub)]

        def up_project(i):
            _, (wg_ref, wv_ref, _), cols, _ = subs[i]
            u = _dot(h, jnp.concatenate([wg_ref[:, cols], wv_ref[:, cols]], axis=1))
            for k in range(n_lane_tiles):
                lanes = slice(k * LANES, (k + 1) * LANES)
                u_ref[i % 2, k, hl:hl + tm] = u[:tm, lanes]
                u_ref[i % 2, k, :hl] = u[tm + hl:, lanes]
                u_ref[i % 2, k, hl + tm:] = u[tm:tm + hl, lanes]

        def gate_and_down(i):
            chunk, (_, _, wd_ref), cols, offset = subs[i]
            start = pl.multiple_of(chunk * fc + offset, sub)
            cp = jnp.concatenate([conv_ref[:, pl.ds(start, sub)],
                                  conv_ref[:, pl.ds(pl.multiple_of(D_FF + start, sub), sub)]], axis=1)
            cu = (window(i % 2, -1) * cp[0:1] + window(i % 2, 0) * cp[1:2] + window(i % 2, 1) * cp[2:3]
                  + cp[CONV_WIDTH:CONV_WIDTH + 1])
            act = _gelu_tanh_times(cu[:, :sub], cu[:, sub:]).astype(BF16)
            for k in range(0, n_out, out_chunk):
                o_ref[0, :, k:k + out_chunk] += _dot(act, wd_ref[cols, k:k + out_chunk])

        up_project(0)
        for i in range(len(subs)):
            if i + 1 < len(subs):
                up_project(i + 1)
            gate_and_down(i)

    chunk_a = (2 * j, (wga_ref, wva_ref, wda_ref))
    chunk_b = (2 * j + 1, (wgb_ref, wvb_ref, wdb_ref))
    if n_chunks % 2 == 0:
        run([chunk_a, chunk_b])
    else:
        @pl.when(j < n_steps - 1)
        def _():
            run([chunk_a, chunk_b])

        @pl.when(j == n_steps - 1)
        def _():
            run([chunk_a])

    @pl.when(j == n_steps - 1)
    def _():
        o_ref[0] = x_ref[0] + _rms(o_ref[0], gpost_ref[...])


def _ffn(x, gpre, w_up, conv, w_down, gpost, tm, fc):
    b, s, d = x.shape
    nf = D_FF // fc
    halo = SUBLANES
    tb = tm // halo
    last_blk = s // halo - 1
    chunk_a = lambda j: 2 * j
    chunk_b = lambda j: jnp.minimum(2 * j + 1, nf - 1)
    return pl.pallas_call(
        functools.partial(_ffn_body, n_chunks=nf),
        out_shape=jax.ShapeDtypeStruct((b, s, d), F32),
        grid=(b, s // tm, pl.cdiv(nf, 2)),
        in_specs=[
            pl.BlockSpec((1, tm, d), lambda bi, si, j: (bi, si, 0)),
            pl.BlockSpec((1, halo, d), lambda bi, si, j: (bi, jnp.maximum(si * tb - 1, 0), 0)),
            pl.BlockSpec((1, halo, d), lambda bi, si, j: (bi, jnp.minimum((si + 1) * tb, last_blk), 0)),
            _resident((1, d)),
            pl.BlockSpec((d, fc), lambda bi, si, j: (0, chunk_a(j))),
            pl.BlockSpec((d, fc), lambda bi, si, j: (0, nf + chunk_a(j))),
            pl.BlockSpec((fc, d), lambda bi, si, j: (chunk_a(j), 0)),
            pl.BlockSpec((d, fc), lambda bi, si, j: (0, chunk_b(j))),
            pl.BlockSpec((d, fc), lambda bi, si, j: (0, nf + chunk_b(j))),
            pl.BlockSpec((fc, d), lambda bi, si, j: (chunk_b(j), 0)),
            _resident((CONV_WIDTH + 1, 2 * D_FF)),
            _resident((1, d)),
        ],
        out_specs=pl.BlockSpec((1, tm, d), lambda bi, si, j: (bi, si, 0)),
        scratch_shapes=[pltpu.VMEM((tm + 2 * halo, d), BF16),
                        pltpu.VMEM((2, 2 * FFN_SUB // LANES, tm + 2 * halo, LANES), F32)],
        compiler_params=_cparams("parallel", "parallel", "arbitrary"),
        name="conv_ffn",
    )(x, x, x, gpre, w_up, w_up, w_down, w_up, w_up, w_down, conv, gpost)


def _rotate_half_columns(w):
    k, n = w.shape
    g = w.reshape(k, n // MLA_ROPE, 2, MLA_ROPE // 2)
    return jnp.concatenate([-g[:, :, 1:], g[:, :, :1]], axis=2).reshape(k, n)


def _prepare_weights(p, seq):
    row = lambda v: v.reshape(1, -1).astype(F32)
    w_in = p["w_in"]
    offs, acc = [], 0
    for wdt in (GLA_QK, GLA_QK, GLA_WIDTH, GLA_WIDTH, GLA_GATE_RANK, GLA_GATE_RANK, MLA_RANK, MLA_RANK, MLA_ROPE):
        offs.append((acc, acc + wdt))
        acc += wdt
    part = [w_in[:, a:b] for a, b in offs]
    gq, gk, gv, gr, ggf, ggb, cq, ckv, kr = part
    d = w_in.shape[0]
    gate_pad = jnp.zeros((d, LANES - 2 * GLA_GATE_RANK), w_in.dtype)
    w_in_p = jnp.concatenate([gq, gk, gv, gr, cq, ckv, ggf, ggb, gate_pad, kr, _rotate_half_columns(kr)],
                             axis=1).astype(BF16)

    def gate_w2(w2, slot):
        full = jnp.zeros((LANES, GLA_QK), F32)
        return full.at[slot * GLA_GATE_RANK:(slot + 1) * GLA_GATE_RANK].set(w2).astype(BF16)

    wq = p["mla_w_q_up"].reshape(MLA_RANK, MLA_HEADS, MLA_QK)
    wq_nope = wq[:, :, :MLA_NOPE].reshape(MLA_RANK, MLA_HEADS * MLA_NOPE)
    wq_rope = wq[:, :, MLA_NOPE:].reshape(MLA_RANK, MLA_HEADS * MLA_ROPE)

    half = MLA_ROPE // 2
    freqs = ROPE_BASE ** (-jnp.arange(half, dtype=F32) / half)
    ang = jnp.arange(seq, dtype=F32)[:, None] * freqs[None, :]
    cos_t = jnp.tile(jnp.cos(ang), (1, 2 * MLA_HEADS))
    sin_t = jnp.tile(jnp.sin(ang), (1, 2 * MLA_HEADS))

    value_half = jnp.concatenate([jnp.ones((1, D_FF), F32), jnp.full((1, D_FF), 0.5, F32)], axis=1)

    return dict(
        g_mix_pre=row(p["norm_mix_pre"]), g_mix_post=row(p["norm_mix_post"]), w_in=w_in_p,
        w2f=gate_w2(p["gla_gate_w2_fwd"], 0), b2f=row(p["gla_gate_b_fwd"]),
        w2b=gate_w2(p["gla_gate_w2_bwd"], 1), b2b=row(p["gla_gate_b_bwd"]),
        gla_out_norm=row(p["gla_out_norm"]),
        mla_q_norm=row(p["mla_q_norm"]), mla_kv_norm=row(p["mla_kv_norm"]),
        wq_nope=wq_nope.astype(BF16), wq_rope=wq_rope.astype(BF16),
        wq_rot=_rotate_half_columns(wq_rope).astype(BF16), wkv=p["mla_w_kv_up"].astype(BF16),
        cos=cos_t, sin=sin_t,
        w_out_gla=p["w_out"][:GLA_WIDTH].astype(BF16), w_out_mla=p["w_out"][GLA_WIDTH:].astype(BF16),
        g_mem_pre=row(p["norm_mem_pre"]), g_mem_post=row(p["norm_mem_post"]), g_mem_kv=row(p["mem_kv_norm"]),
        w_mem_q=p["w_mem_q"].astype(BF16), w_mem_o=p["w_mem_o"].astype(BF16),
        w_mem_kv=jnp.concatenate([p["w_mem_k"], p["w_mem_v"]], axis=1).astype(BF16),
        g_ffn_pre=row(p["norm_ffn_pre"]), g_ffn_post=row(p["norm_ffn_post"]),
        w_ffn_up=p["w_ffn_up"].astype(BF16),
        ffn_conv=jnp.concatenate([p["ffn_conv_w"].astype(F32), row(p["ffn_conv_b"])], axis=0) * value_half,
        w_ffn_down=p["w_ffn_down"].astype(BF16),
    )


def _tiles(seq):
    tm = min(512, seq)
    return dict(tm=tm, gla_ts=tm, mla_ts=tm, mla_tq=min(1024, seq), mla_tk=tm, ffn_tm=tm, ffn_fc=512)


def _apply_layer(x, mem, w):
    b, s, d = x.shape
    t = _tiles(s)
    tm = t["tm"]
    x2 = x.reshape(b * s, d)
    proj = _norm_matmul(x2, w["g_mix_pre"], w["w_in"], tm).reshape(b, s, IN_WIDTH_PADDED)
    o_gla = _gla(proj, w["w2f"], w["b2f"], w["w2b"], w["b2b"], w["gla_out_norm"], t["gla_ts"])
    o_mla = _mla(proj, w["mla_q_norm"], w["mla_kv_norm"], w["wq_nope"], w["wq_rope"], w["wq_rot"], w["wkv"],
                 w["cos"], w["sin"], t["mla_ts"], t["mla_tq"], t["mla_tk"])
    x2 = _mix_out(o_gla.reshape(b * s, GLA_WIDTH), o_mla.reshape(b * s, MLA_WIDTH),
                  w["w_out_gla"], w["w_out_mla"], x2, w["g_mix_post"], tm)
    mtok = mem.shape[1]
    kv = _norm_matmul(mem.reshape(b * mtok, d), w["g_mem_kv"], w["w_mem_kv"], min(tm, b * mtok))
    x3 = _mem_attn(x2.reshape(b, s, d), w["g_mem_pre"], w["w_mem_q"], kv.reshape(b, mtok, 2 * d),
                   w["w_mem_o"], w["g_mem_post"], tm)
    return _ffn(x3, w["g_ffn_pre"], w["w_ffn_up"], w["ffn_conv"], w["w_ffn_down"],
                w["g_ffn_post"], t["ffn_tm"], t["ffn_fc"])


def kernel(x_prompt, x_sample, mem_prompt, mem_sample, norm_mix_pre, norm_mix_post, w_in, gla_gate_w2_fwd, gla_gate_b_fwd, gla_gate_w2_bwd, gla_gate_b_bwd, gla_out_norm, mla_q_norm, mla_w_q_up, mla_kv_norm, mla_w_kv_up, w_out, norm_mem_pre, norm_mem_post, mem_kv_norm, w_mem_q, w_mem_k, w_mem_v, w_mem_o, norm_ffn_pre, norm_ffn_post, w_ffn_up, ffn_conv_w, ffn_conv_b, w_ffn_down):
    params = dict(
        norm_mix_pre=norm_mix_pre, norm_mix_post=norm_mix_post, w_in=w_in,
        gla_gate_w2_fwd=gla_gate_w2_fwd, gla_gate_b_fwd=gla_gate_b_fwd,
        gla_gate_w2_bwd=gla_gate_w2_bwd, gla_gate_b_bwd=gla_gate_b_bwd, gla_out_norm=gla_out_norm,
        mla_q_norm=mla_q_norm, mla_w_q_up=mla_w_q_up, mla_kv_norm=mla_kv_norm, mla_w_kv_up=mla_w_kv_up,
        w_out=w_out, norm_mem_pre=norm_mem_pre, norm_mem_post=norm_mem_post, mem_kv_norm=mem_kv_norm,
        w_mem_q=w_mem_q, w_mem_k=w_mem_k, w_mem_v=w_mem_v, w_mem_o=w_mem_o,
        norm_ffn_pre=norm_ffn_pre, norm_ffn_post=norm_ffn_post, w_ffn_up=w_ffn_up,
        ffn_conv_w=ffn_conv_w, ffn_conv_b=ffn_conv_b, w_ffn_down=w_ffn_down,
    )
    depth = w_in.shape[0]
    yp, ys = x_prompt, x_sample
    for layer in range(depth):
        p = {k: v[layer] for k, v in params.items()}
        assert yp.shape[1] == ys.shape[1]
        w = _prepare_weights(p, yp.shape[1])
        yp = _apply_layer(yp, mem_prompt, w)
        ys = _apply_layer(ys, mem_sample, w)
    return (yp, ys)
```

```python
import functools
import math

import jax
import jax.numpy as jnp
from jax import lax
from jax.experimental import pallas as pl
from jax.experimental.pallas import tpu as pltpu

F32 = jnp.float32
BF16 = jnp.bfloat16

EPS = 1e-6
D_MODEL = 2048
GLA_HEADS = 4
GLA_DK = 128
GLA_DV = 256
GLA_GATE_RANK = 16
GLA_TAU = 16.0
GLA_CHUNK = 64
GLA_SUB = 256
GLA_QK = GLA_HEADS * GLA_DK
GLA_WIDTH = GLA_HEADS * GLA_DV
MLA_HEADS = 8
MLA_RANK = 512
MLA_NOPE = 128
MLA_ROPE = 64
MLA_V = 128
MLA_QK = MLA_NOPE + MLA_ROPE
MLA_WIDTH = MLA_HEADS * MLA_V
ROPE_BASE = 10000.0
MEM_HEADS = 4
MEM_HEAD_DIM = D_MODEL // MEM_HEADS
D_FF = 5632
CONV_WIDTH = 3
FFN_SUB = 256

LANES = 128
SUBLANES = 8
BF16_ROWS = 16
VMEM_LIMIT_BYTES = 56 * 1024 * 1024
MLA_VT_ROWS = MLA_V + BF16_ROWS
LOG2E = 1.4426950408889634

IN_GQ = 0
IN_GK = IN_GQ + GLA_QK
IN_GV = IN_GK + GLA_QK
IN_GR = IN_GV + GLA_WIDTH
IN_CQ = IN_GR + GLA_WIDTH
IN_CKV = IN_CQ + MLA_RANK
IN_GATE = IN_CKV + MLA_RANK
IN_KR = IN_GATE + LANES
IN_WIDTH_PADDED = IN_KR + LANES


def _cparams(*semantics):
    return pltpu.CompilerParams(dimension_semantics=semantics, vmem_limit_bytes=VMEM_LIMIT_BYTES)


def _resident(shape):
    zeros = (0,) * len(shape)
    return pl.BlockSpec(shape, lambda *_: zeros, pipeline_mode=pl.Buffered(1))


def _rms(x, g):
    return x * lax.rsqrt(jnp.mean(x * x, axis=-1, keepdims=True) + EPS) * g


def _dot(a, b):
    return jnp.dot(a, b, preferred_element_type=F32)


def _dot_nt(a, b):
    return lax.dot_general(a, b, (((1,), (1,)), ((), ())), preferred_element_type=F32)


def _dot_tn(a, b):
    return lax.dot_general(a, b, (((0,), (0,)), ((), ())), preferred_element_type=F32)


def _col_chunk(n):
    for c in (512, 256, LANES):
        if n % c == 0:
            return c
    raise ValueError(f"width {n} is not a multiple of {LANES}")


def _norm_matmul_body(x_ref, g_ref, w_ref, o_ref):
    h = _rms(x_ref[...], g_ref[...]).astype(BF16)
    n = o_ref.shape[-1]
    nc = _col_chunk(n)
    for c in range(0, n, nc):
        o_ref[:, c:c + nc] = _dot(h, w_ref[:, c:c + nc]).astype(o_ref.dtype)


def _norm_matmul(x, g, w, tm):
    t, d = x.shape
    n = w.shape[1]
    return pl.pallas_call(
        _norm_matmul_body,
        out_shape=jax.ShapeDtypeStruct((t, n), BF16),
        grid=(t // tm,),
        in_specs=[pl.BlockSpec((tm, d), lambda i: (i, 0)), _resident((1, d)), _resident((d, n))],
        out_specs=pl.BlockSpec((tm, n), lambda i: (i, 0)),
        compiler_params=_cparams("parallel"),
        name="norm_matmul",
    )(x, g, w)


def _log_sigmoid(x):
    return jnp.minimum(x, 0.0) - jnp.log(1.0 + jnp.exp(-jnp.abs(x)))


def _gla_tables(reverse):
    shift = GLA_CHUNK.bit_length() - 1
    row = lax.broadcasted_iota(jnp.int32, (GLA_SUB, GLA_SUB), 0)
    col = lax.broadcasted_iota(jnp.int32, (GLA_SUB, GLA_SUB), 1)
    same_chunk = lax.shift_right_logical(row, shift) == lax.shift_right_logical(col, shift)
    in_chunk = lambda cond: jnp.where(same_chunk, jnp.where(cond, 1.0, 0.0), 0.0)
    if reverse:
        cum = in_chunk(col >= row)
        keep = in_chunk(col > row) > 0.5
    else:
        cum = in_chunk(col <= row)
        keep = cum > 0.5
    return cum.astype(BF16), keep


def _gla_log_decay(gate_ref, w2_ref, b2_ref, la_ref):
    logits = _dot(gate_ref[0], w2_ref[...]) + b2_ref[...]
    la_ref[...] = _log_sigmoid(logits) * (1.0 / GLA_TAU)


def _gla_block(q_ref, k_ref, v_ref, la_ref, state_ref, reverse):
    ts = q_ref.shape[1]
    c = GLA_CHUNK
    n_chunks = ts // c
    cum_mat, keep = _gla_tables(reverse)
    la = la_ref[...]
    la_hi = la.astype(BF16)
    la_lo = (la - la_hi.astype(F32)).astype(BF16)
    cum = jnp.concatenate(
        [_dot(cum_mat, la_hi[r:r + GLA_SUB]) + _dot(cum_mat, la_lo[r:r + GLA_SUB])
         for r in range(0, ts, GLA_SUB)], axis=0).reshape(n_chunks, c, GLA_QK)
    mid_row, last_row = (c // 2 - 1, 0) if reverse else (c // 2, c - 1)
    mid = cum[:, mid_row:mid_row + 1, :]
    last = cum[:, last_row:last_row + 1, :]
    q = q_ref[0].astype(F32).reshape(n_chunks, c, GLA_QK) * (GLA_DK ** -0.5)
    k = k_ref[0].astype(F32).reshape(n_chunks, c, GLA_QK)
    qi_f = q * jnp.exp(cum - mid)
    ki_f = k * jnp.exp(mid - cum)
    flat = lambda t: t.reshape(ts, GLA_QK).astype(BF16)
    qi = flat(qi_f)
    ki = flat(ki_f)
    qd = flat(qi_f * jnp.exp(mid))
    kd = flat(ki_f * jnp.exp(last - mid))
    chunk_decay = jnp.exp(last)
    order = range(n_chunks - 1, -1, -1) if reverse else range(n_chunks)
    outs = []
    for h in range(GLA_HEADS):
        kc = slice(h * GLA_DK, (h + 1) * GLA_DK)
        v = v_ref[0, :, h * GLA_DV:(h + 1) * GLA_DV]
        intra = []
        for r in range(0, ts, GLA_SUB):
            rows = slice(r, r + GLA_SUB)
            att = jnp.where(keep, _dot_nt(qi[rows, kc], ki[rows, kc]), 0.0).astype(BF16)
            intra.append(_dot(att, v[rows]))
        state = state_ref[h]
        inter = [None] * n_chunks
        for n in order:
            rows = slice(n * c, (n + 1) * c)
            inter[n] = _dot_nt(qd[rows, kc], state.astype(BF16))
            state = state * chunk_decay[n, :, kc] + _dot_tn(v[rows], kd[rows, kc])
        state_ref[h] = state
        outs.append(jnp.concatenate(intra, axis=0) + jnp.concatenate(inter, axis=0))
    return outs


def _gla_fwd_body(q_ref, k_ref, v_ref, gate_ref, w2_ref, b2_ref, o_ref, state_ref, la_ref):
    @pl.when(pl.program_id(1) == 0)
    def _():
        state_ref[...] = jnp.zeros_like(state_ref)

    _gla_log_decay(gate_ref, w2_ref, b2_ref, la_ref)
    outs = _gla_block(q_ref, k_ref, v_ref, la_ref, state_ref, False)
    for h in range(GLA_HEADS):
        o_ref[0, :, h * GLA_DV:(h + 1) * GLA_DV] = outs[h]


def _gla_bwd_body(q_ref, k_ref, v_ref, gate_ref, w2_ref, b2_ref, of_ref, r_ref, gn_ref,
                  o_ref, state_ref, la_ref):
    @pl.when(pl.program_id(1) == 0)
    def _():
        state_ref[...] = jnp.zeros_like(state_ref)

    _gla_log_decay(gate_ref, w2_ref, b2_ref, la_ref)
    outs = _gla_block(q_ref, k_ref, v_ref, la_ref, state_ref, True)
    gn = gn_ref[...]
    for h in range(GLA_HEADS):
        vcols = slice(h * GLA_DV, (h + 1) * GLA_DV)
        o = outs[h] + of_ref[0, :, vcols]
        r = r_ref[0, :, vcols].astype(F32)
        silu = r * (1.0 / (1.0 + jnp.exp(-r)))
        o_ref[0, :, vcols] = (_rms(o, gn) * silu).astype(o_ref.dtype)


def _gla(proj, w2f, b2f, w2b, b2b, out_norm, ts):
    b, s, _ = proj.shape
    ns = s // ts

    def col_spec(width, offset, rev):
        blk = offset // width
        if rev:
            return pl.BlockSpec((1, ts, width), lambda bi, si: (bi, ns - 1 - si, blk))
        return pl.BlockSpec((1, ts, width), lambda bi, si: (bi, si, blk))

    def common_specs(rev):
        return [col_spec(GLA_QK, IN_GQ, rev), col_spec(GLA_QK, IN_GK, rev), col_spec(GLA_WIDTH, IN_GV, rev),
                col_spec(LANES, IN_GATE, rev), _resident((LANES, GLA_QK)), _resident((1, GLA_QK))]

    scratch = [pltpu.VMEM((GLA_HEADS, GLA_DV, GLA_DK), F32), pltpu.VMEM((ts, GLA_QK), F32)]
    o_fwd = pl.pallas_call(
        _gla_fwd_body,
        out_shape=jax.ShapeDtypeStruct((b, s, GLA_WIDTH), F32),
        grid=(b, ns),
        in_specs=common_specs(False),
        out_specs=pl.BlockSpec((1, ts, GLA_WIDTH), lambda bi, si: (bi, si, 0)),
        scratch_shapes=scratch,
        compiler_params=_cparams("parallel", "arbitrary"),
        name="gla_fwd",
    )(proj, proj, proj, proj, w2f, b2f)
    return pl.pallas_call(
        _gla_bwd_body,
        out_shape=jax.ShapeDtypeStruct((b, s, GLA_WIDTH), BF16),
        grid=(b, ns),
        in_specs=common_specs(True) + [
            pl.BlockSpec((1, ts, GLA_WIDTH), lambda bi, si: (bi, ns - 1 - si, 0)),
            col_spec(GLA_WIDTH, IN_GR, True),
            _resident((1, GLA_DV)),
        ],
        out_specs=pl.BlockSpec((1, ts, GLA_WIDTH), lambda bi, si: (bi, ns - 1 - si, 0)),
        scratch_shapes=scratch,
        compiler_params=_cparams("parallel", "arbitrary"),
        name="gla_bwd",
    )(proj, proj, proj, proj, w2b, b2b, o_fwd, proj, out_norm)


def _mla_prep_body(cq_ref, ckv_ref, kr_ref, qn_ref, kvn_ref, wq_nope_ref, wq_rope_ref, wq_rot_ref, wkv_ref,
                   cos_ref, sin_ref, q_out, k_out, vt_out):
    hq = _rms(cq_ref[0].astype(F32), qn_ref[...]).astype(BF16)
    hkv = _rms(ckv_ref[0].astype(F32), kvn_ref[...]).astype(BF16)
    cos = cos_ref[...]
    sin = sin_ref[...]
    q_scale = MLA_QK ** -0.5 * LOG2E
    q_nope = _dot(hq, wq_nope_ref[...]) * q_scale
    q_pe = (_dot(hq, wq_rope_ref[...]) * cos + _dot(hq, wq_rot_ref[...]) * sin) * q_scale
    kv = _dot(hkv, wkv_ref[...])
    kr = kr_ref[0].astype(F32)
    k_pe = (kr[:, :MLA_ROPE] * cos[:, :MLA_ROPE] + kr[:, MLA_ROPE:] * sin[:, :MLA_ROPE]).astype(BF16)
    pad_rows = (MLA_VT_ROWS - MLA_V, cq_ref.shape[1])
    ones_row = jnp.where(lax.broadcasted_iota(jnp.int32, pad_rows, 0) == 0, 1.0, 0.0).astype(BF16)
    for h in range(MLA_HEADS):
        q_out[0, h, :, :MLA_NOPE] = q_nope[:, h * MLA_NOPE:(h + 1) * MLA_NOPE].astype(BF16)
        q_out[0, h, :, MLA_NOPE:] = q_pe[:, h * MLA_ROPE:(h + 1) * MLA_ROPE].astype(BF16)
        base = h * (MLA_NOPE + MLA_V)
        k_out[0, h, :, :MLA_NOPE] = kv[:, base:base + MLA_NOPE].astype(BF16)
        k_out[0, h, :, MLA_NOPE:] = k_pe
        vt_out[0, h, :MLA_V] = kv[:, base + MLA_NOPE:base + MLA_NOPE + MLA_V].T.astype(BF16)
        vt_out[0, h, MLA_V:] = ones_row


def _reduce_rows(op, x, groups=8):
    rows, n = x.shape
    partial = op(x.reshape(groups, rows // groups, n), axis=0)
    return op(partial, axis=0, keepdims=True)


def _mla_attn_body(q_ref, k_ref, vt_ref, o_ref, s_ref, *, tk):
    q = q_ref[0, 0]
    n_chunks = k_ref.shape[2] // tk

    s_ref[...] = _dot_nt(k_ref[0, 0], q)
    m = acc = None
    for c in range(n_chunks):
        s = s_ref[c * tk:(c + 1) * tk]
        chunk_max = _reduce_rows(jnp.max, s)
        m_new = chunk_max if m is None else jnp.maximum(m, chunk_max)
        p = jnp.exp2(s - m_new).astype(BF16)
        pv = _dot(vt_ref[0, 0, :, c * tk:(c + 1) * tk], p)
        acc = pv if acc is None else jnp.exp2(m - m_new) * acc + pv
        m = m_new
    o_ref[0] = (acc[:MLA_V] / acc[MLA_V:MLA_V + 1]).T.astype(o_ref.dtype)


def _mla(proj, q_norm, kv_norm, wq_nope, wq_rope, wq_rot, wkv, cos_t, sin_t, ts, tq, tk):
    b, s, _ = proj.shape
    rope_w = MLA_HEADS * MLA_ROPE
    q, k, vt = pl.pallas_call(
        _mla_prep_body,
        out_shape=(jax.ShapeDtypeStruct((b, MLA_HEADS, s, MLA_QK), BF16),
                   jax.ShapeDtypeStruct((b, MLA_HEADS, s, MLA_QK), BF16),
                   jax.ShapeDtypeStruct((b, MLA_HEADS, MLA_VT_ROWS, s), BF16)),
        grid=(b, s // ts),
        in_specs=[
            pl.BlockSpec((1, ts, MLA_RANK), lambda bi, si: (bi, si, IN_CQ // MLA_RANK)),
            pl.BlockSpec((1, ts, MLA_RANK), lambda bi, si: (bi, si, IN_CKV // MLA_RANK)),
            pl.BlockSpec((1, ts, LANES), lambda bi, si: (bi, si, IN_KR // LANES)),
            _resident((1, MLA_RANK)), _resident((1, MLA_RANK)),
            _resident((MLA_RANK, MLA_HEADS * MLA_NOPE)), _resident((MLA_RANK, rope_w)),
            _resident((MLA_RANK, rope_w)), _resident((MLA_RANK, MLA_HEADS * (MLA_NOPE + MLA_V))),
            pl.BlockSpec((ts, rope_w), lambda bi, si: (si, 0)),
            pl.BlockSpec((ts, rope_w), lambda bi, si: (si, 0)),
        ],
        out_specs=(pl.BlockSpec((1, MLA_HEADS, ts, MLA_QK), lambda bi, si: (bi, 0, si, 0)),
                   pl.BlockSpec((1, MLA_HEADS, ts, MLA_QK), lambda bi, si: (bi, 0, si, 0)),
                   pl.BlockSpec((1, MLA_HEADS, MLA_VT_ROWS, ts), lambda bi, si: (bi, 0, 0, si))),
        compiler_params=_cparams("parallel", "parallel"),
        name="mla_prep",
    )(proj, proj, proj, q_norm, kv_norm, wq_nope, wq_rope, wq_rot, wkv, cos_t, sin_t)
    return pl.pallas_call(
        functools.partial(_mla_attn_body, tk=tk),
        out_shape=jax.ShapeDtypeStruct((b, s, MLA_WIDTH), BF16),
        grid=(b, MLA_HEADS, s // tq),
        in_specs=[
            pl.BlockSpec((1, 1, tq, MLA_QK), lambda bi, hi, qi: (bi, hi, qi, 0)),
            pl.BlockSpec((1, 1, s, MLA_QK), lambda bi, hi, qi: (bi, hi, 0, 0)),
            pl.BlockSpec((1, 1, MLA_VT_ROWS, s), lambda bi, hi, qi: (bi, hi, 0, 0)),
        ],
        out_specs=pl.BlockSpec((1, tq, MLA_V), lambda bi, hi, qi: (bi, qi, hi)),
        scratch_shapes=[pltpu.VMEM((s, tq), F32)],
        compiler_params=_cparams("parallel", "parallel", "parallel"),
        name="mla_attn",
    )(q, k, vt)


def _residual_norm_store(x, o_ref, sumsq, g):
    scale = lax.rsqrt(sumsq * (1.0 / o_ref.shape[-1]) + EPS)
    o_ref[...] = x + o_ref[...] * scale * g


def _project_rows(o_ref, lhs_and_weights):
    n = o_ref.shape[-1]
    nc = _col_chunk(n)
    sumsq = jnp.zeros((o_ref.shape[0], 1), F32)
    for c in range(0, n, nc):
        y = sum(_dot(a, w_ref[:, c:c + nc]) for a, w_ref in lhs_and_weights)
        sumsq = sumsq + jnp.sum(y * y, axis=-1, keepdims=True)
        o_ref[:, c:c + nc] = y
    return sumsq


def _mix_out_body(a1_ref, a2_ref, w1_ref, w2_ref, x_ref, g_ref, o_ref):
    sumsq = _project_rows(o_ref, [(a1_ref[...], w1_ref), (a2_ref[...], w2_ref)])
    _residual_norm_store(x_ref[...], o_ref, sumsq, g_ref[...])


def _mix_out(o_gla, o_mla, w1, w2, x, g, tm):
    t, d = x.shape
    return pl.pallas_call(
        _mix_out_body,
        out_shape=jax.ShapeDtypeStruct((t, d), F32),
        grid=(t // tm,),
        in_specs=[
            pl.BlockSpec((tm, GLA_WIDTH), lambda i: (i, 0)),
            pl.BlockSpec((tm, MLA_WIDTH), lambda i: (i, 0)),
            _resident((GLA_WIDTH, d)), _resident((MLA_WIDTH, d)),
            pl.BlockSpec((tm, d), lambda i: (i, 0)),
            _resident((1, d)),
        ],
        out_specs=pl.BlockSpec((tm, d), lambda i: (i, 0)),
        compiler_params=_cparams("parallel"),
        name="mix_out",
    )(o_gla, o_mla, w1, w2, x, g)


def _mem_attn_body(x_ref, gpre_ref, wq_ref, kv_ref, wo_ref, gpost_ref, o_ref, a_ref):
    q_scale = MEM_HEAD_DIM ** -0.5 * LOG2E
    x = x_ref[0]
    h = _rms(x, gpre_ref[...]).astype(BF16)
    for hd in range(MEM_HEADS):
        cols = slice(hd * MEM_HEAD_DIM, (hd + 1) * MEM_HEAD_DIM)
        vcols = slice(D_MODEL + hd * MEM_HEAD_DIM, D_MODEL + (hd + 1) * MEM_HEAD_DIM)
        q = (_dot(h, wq_ref[:, cols]) * q_scale).astype(BF16)
        s = _dot_nt(q, kv_ref[0, :, cols])
        m = jnp.max(s, axis=-1, keepdims=True)
        p = jnp.exp2(s - m)
        l = jnp.sum(p, axis=-1, keepdims=True)
        a_ref[:, cols] = (_dot(p.astype(BF16), kv_ref[0, :, vcols]) / l).astype(BF16)
    sumsq = _project_rows(o_ref.at[0], [(a_ref[...], wo_ref)])
    _residual_norm_store(x, o_ref.at[0], sumsq, gpost_ref[...])


def _mem_attn(x, gpre, wq, kv, wo, gpost, tm):
    b, s, d = x.shape
    m = kv.shape[1]
    return pl.pallas_call(
        _mem_attn_body,
        out_shape=jax.ShapeDtypeStruct((b, s, d), F32),
        grid=(b, s // tm),
        in_specs=[
            pl.BlockSpec((1, tm, d), lambda bi, si: (bi, si, 0)),
            _resident((1, d)), _resident((d, d)),
            pl.BlockSpec((1, m, 2 * d), lambda bi, si: (bi, 0, 0)),
            _resident((d, d)), _resident((1, d)),
        ],
        out_specs=pl.BlockSpec((1, tm, d), lambda bi, si: (bi, si, 0)),
        scratch_shapes=[pltpu.VMEM((tm, d), BF16)],
        compiler_params=_cparams("parallel", "parallel"),
        name="mem_attn",
    )(x, gpre, wq, kv, wo, gpost)


def _gelu_tanh_times(x, v_half):
    c = math.sqrt(2.0 / math.pi)
    inner = x * (c + (c * 0.044715) * (x * x))
    return x * (1.0 + jnp.tanh(inner)) * v_half


def _ffn_body(x_ref, xprev_ref, xnext_ref, gpre_ref, wua_ref, wda_ref, wub_ref, wdb_ref,
              conv_ref, gpost_ref, o_ref, h_ref, u_ref, *, n_chunks):
    si = pl.program_id(1)
    j = pl.program_id(2)
    n_steps = pl.num_programs(2)
    tm = x_ref.shape[1]
    fc = wda_ref.shape[0]
    sub = FFN_SUB
    hl = SUBLANES
    n_lane_tiles = 2 * sub // LANES
    n_out = o_ref.shape[-1]
    out_chunk = _col_chunk(n_out)

    @pl.when(j == 0)
    def _():
        g = gpre_ref[...]
        h_ref[:tm] = _rms(x_ref[0], g).astype(BF16)
        nxt = jnp.where(si == pl.num_programs(1) - 1, 0.0, _rms(xnext_ref[0], g))
        prv = jnp.where(si == 0, 0.0, _rms(xprev_ref[0], g))
        h_ref[tm:] = jnp.concatenate([nxt, prv], axis=0).astype(BF16)
        o_ref[...] = jnp.zeros_like(o_ref)

    def window(slot, offset):
        return jnp.concatenate([u_ref[slot, k, hl + offset:hl + offset + tm] for k in range(n_lane_tiles)],
                               axis=1)

    def run(chunks):
        h = h_ref[...]
        subs = [(chunk, refs, slice(c * sub, (c + 1) * sub), c * sub)
                for chunk, refs in chunks for c in range(fc // sub)]

        def up_project(i):
            _, (wu_ref, _), cols, offset = subs[i]
            value_cols = slice(fc + offset, fc + offset + sub)
            u = _dot(h, jnp.concatenate([wu_ref[0, :, cols], wu_ref[0, :, value_cols]], axis=1))
            for k in range(n_lane_tiles):
                lanes = slice(k * LANES, (k + 1) * LANES)
                u_ref[i % 2, k, hl:hl + tm] = u[:tm, lanes]
                u_ref[i % 2, k, :hl] = u[tm + hl:, lanes]
                u_ref[i % 2, k, hl + tm:] = u[tm:tm + hl, lanes]

        def gate_and_down(i):
            chunk, (_, wd_ref), cols, offset = subs[i]
            start = pl.multiple_of(chunk * fc + offset, sub)
            cp = jnp.concatenate([conv_ref[:, pl.ds(start, sub)],
                                  conv_ref[:, pl.ds(pl.multiple_of(D_FF + start, sub), sub)]], axis=1)
            cu = (window(i % 2, -1) * cp[0:1] + window(i % 2, 0) * cp[1:2] + window(i % 2, 1) * cp[2:3]
                  + cp[CONV_WIDTH:CONV_WIDTH + 1])
            act = _gelu_tanh_times(cu[:, :sub], cu[:, sub:]).astype(BF16)
            for k in range(0, n_out, out_chunk):
                o_ref[0, :, k:k + out_chunk] += _dot(act, wd_ref[cols, k:k + out_chunk])

        up_project(0)
        for i in range(len(subs)):
            if i + 1 < len(subs):
                up_project(i + 1)
            gate_and_down(i)

    chunk_a = (2 * j, (wua_ref, wda_ref))
    chunk_b = (2 * j + 1, (wub_ref, wdb_ref))
    if n_chunks % 2 == 0:
        run([chunk_a, chunk_b])
    else:
        @pl.when(j < n_steps - 1)
        def _():
            run([chunk_a, chunk_b])

        @pl.when(j == n_steps - 1)
        def _():
            run([chunk_a])

    @pl.when(j == n_steps - 1)
    def _():
        o_ref[0] = x_ref[0] + _rms(o_ref[0], gpost_ref[...])


def _chunk_major_body(g_ref, v_ref, o_ref):
    fc = g_ref.shape[1]
    o_ref[0, :, :fc] = g_ref[...].astype(o_ref.dtype)
    o_ref[0, :, fc:] = v_ref[...].astype(o_ref.dtype)


def _ffn_up_chunk_major(w_up, fc):
    d = w_up.shape[0]
    nf = D_FF // fc
    return pl.pallas_call(
        _chunk_major_body,
        out_shape=jax.ShapeDtypeStruct((nf, d, 2 * fc), BF16),
        grid=(nf,),
        in_specs=[pl.BlockSpec((d, fc), lambda j: (0, j)), pl.BlockSpec((d, fc), lambda j: (0, nf + j))],
        out_specs=pl.BlockSpec((1, d, 2 * fc), lambda j: (j, 0, 0)),
        compiler_params=_cparams("parallel"),
        name="ffn_up_chunk_major",
    )(w_up, w_up)


def _ffn(x, gpre, w_up, conv, w_down, gpost, tm, fc):
    b, s, d = x.shape
    nf = D_FF // fc
    halo = SUBLANES
    tb = tm // halo
    last_blk = s // halo - 1
    chunk_a = lambda j: 2 * j
    chunk_b = lambda j: jnp.minimum(2 * j + 1, nf - 1)
    return pl.pallas_call(
        functools.partial(_ffn_body, n_chunks=nf),
        out_shape=jax.ShapeDtypeStruct((b, s, d), F32),
        grid=(b, s // tm, pl.cdiv(nf, 2)),
        in_specs=[
            pl.BlockSpec((1, tm, d), lambda bi, si, j: (bi, si, 0)),
            pl.BlockSpec((1, halo, d), lambda bi, si, j: (bi, jnp.maximum(si * tb - 1, 0), 0)),
            pl.BlockSpec((1, halo, d), lambda bi, si, j: (bi, jnp.minimum((si + 1) * tb, last_blk), 0)),
            _resident((1, d)),
            pl.BlockSpec((1, d, 2 * fc), lambda bi, si, j: (chunk_a(j), 0, 0)),
            pl.BlockSpec((fc, d), lambda bi, si, j: (chunk_a(j), 0)),
            pl.BlockSpec((1, d, 2 * fc), lambda bi, si, j: (chunk_b(j), 0, 0)),
            pl.BlockSpec((fc, d), lambda bi, si, j: (chunk_b(j), 0)),
            _resident((CONV_WIDTH + 1, 2 * D_FF)),
            _resident((1, d)),
        ],
        out_specs=pl.BlockSpec((1, tm, d), lambda bi, si, j: (bi, si, 0)),
        scratch_shapes=[pltpu.VMEM((tm + 2 * halo, d), BF16),
                        pltpu.VMEM((2, 2 * FFN_SUB // LANES, tm + 2 * halo, LANES), F32)],
        compiler_params=_cparams("parallel", "parallel", "arbitrary"),
        name="conv_ffn",
    )(x, x, x, gpre, w_up, w_down, w_up, w_down, conv, gpost)


def _rotate_half_columns(w):
    k, n = w.shape
    g = w.reshape(k, n // MLA_ROPE, 2, MLA_ROPE // 2)
    return jnp.concatenate([-g[:, :, 1:], g[:, :, :1]], axis=2).reshape(k, n)


def _prepare_weights(p, seq):
    row = lambda v: v.reshape(1, -1).astype(F32)
    w_in = p["w_in"]
    offs, acc = [], 0
    for wdt in (GLA_QK, GLA_QK, GLA_WIDTH, GLA_WIDTH, GLA_GATE_RANK, GLA_GATE_RANK, MLA_RANK, MLA_RANK, MLA_ROPE):
        offs.append((acc, acc + wdt))
        acc += wdt
    part = [w_in[:, a:b] for a, b in offs]
    gq, gk, gv, gr, ggf, ggb, cq, ckv, kr = part
    d = w_in.shape[0]
    gate_pad = jnp.zeros((d, LANES - 2 * GLA_GATE_RANK), w_in.dtype)
    w_in_p = jnp.concatenate([gq, gk, gv, gr, cq, ckv, ggf, ggb, gate_pad, kr, _rotate_half_columns(kr)],
                             axis=1).astype(BF16)

    def gate_w2(w2, slot):
        full = jnp.zeros((LANES, GLA_QK), F32)
        return full.at[slot * GLA_GATE_RANK:(slot + 1) * GLA_GATE_RANK].set(w2).astype(BF16)

    wq = p["mla_w_q_up"].reshape(MLA_RANK, MLA_HEADS, MLA_QK)
    wq_nope = wq[:, :, :MLA_NOPE].reshape(MLA_RANK, MLA_HEADS * MLA_NOPE)
    wq_rope = wq[:, :, MLA_NOPE:].reshape(MLA_RANK, MLA_HEADS * MLA_ROPE)

    half = MLA_ROPE // 2
    freqs = ROPE_BASE ** (-jnp.arange(half, dtype=F32) / half)
    ang = jnp.arange(seq, dtype=F32)[:, None] * freqs[None, :]
    cos_t = jnp.tile(jnp.cos(ang), (1, 2 * MLA_HEADS))
    sin_t = jnp.tile(jnp.sin(ang), (1, 2 * MLA_HEADS))

    value_half = jnp.concatenate([jnp.ones((1, D_FF), F32), jnp.full((1, D_FF), 0.5, F32)], axis=1)

    return dict(
        g_mix_pre=row(p["norm_mix_pre"]), g_mix_post=row(p["norm_mix_post"]), w_in=w_in_p,
        w2f=gate_w2(p["gla_gate_w2_fwd"], 0), b2f=row(p["gla_gate_b_fwd"]),
        w2b=gate_w2(p["gla_gate_w2_bwd"], 1), b2b=row(p["gla_gate_b_bwd"]),
        gla_out_norm=row(p["gla_out_norm"]),
        mla_q_norm=row(p["mla_q_norm"]), mla_kv_norm=row(p["mla_kv_norm"]),
        wq_nope=wq_nope.astype(BF16), wq_rope=wq_rope.astype(BF16),
        wq_rot=_rotate_half_columns(wq_rope).astype(BF16), wkv=p["mla_w_kv_up"].astype(BF16),
        cos=cos_t, sin=sin_t,
        w_out_gla=p["w_out"][:GLA_WIDTH].astype(BF16), w_out_mla=p["w_out"][GLA_WIDTH:].astype(BF16),
        g_mem_pre=row(p["norm_mem_pre"]), g_mem_post=row(p["norm_mem_post"]), g_mem_kv=row(p["mem_kv_norm"]),
        w_mem_q=p["w_mem_q"].astype(BF16), w_mem_o=p["w_mem_o"].astype(BF16),
        w_mem_kv=jnp.concatenate([p["w_mem_k"], p["w_mem_v"]], axis=1).astype(BF16),
        g_ffn_pre=row(p["norm_ffn_pre"]), g_ffn_post=row(p["norm_ffn_post"]),
        w_ffn_up=_ffn_up_chunk_major(p["w_ffn_up"], _tiles(seq)["ffn_fc"]),
        ffn_conv=jnp.concatenate([p["ffn_conv_w"].astype(F32), row(p["ffn_conv_b"])], axis=0) * value_half,
        w_ffn_down=p["w_ffn_down"].astype(BF16),
    )


def _tiles(seq):
    tm = min(512, seq)
    return dict(tm=tm, gla_ts=tm, mla_ts=tm, mla_tq=min(1024, seq), mla_tk=tm, ffn_tm=tm, ffn_fc=512)


def _apply_layer(x, mem, w):
    b, s, d = x.shape
    t = _tiles(s)
    tm = t["tm"]
    x2 = x.reshape(b * s, d)
    proj = _norm_matmul(x2, w["g_mix_pre"], w["w_in"], tm).reshape(b, s, IN_WIDTH_PADDED)
    o_gla = _gla(proj, w["w2f"], w["b2f"], w["w2b"], w["b2b"], w["gla_out_norm"], t["gla_ts"])
    o_mla = _mla(proj, w["mla_q_norm"], w["mla_kv_norm"], w["wq_nope"], w["wq_rope"], w["wq_rot"], w["wkv"],
                 w["cos"], w["sin"], t["mla_ts"], t["mla_tq"], t["mla_tk"])
    x2 = _mix_out(o_gla.reshape(b * s, GLA_WIDTH), o_mla.reshape(b * s, MLA_WIDTH),
                  w["w_out_gla"], w["w_out_mla"], x2, w["g_mix_post"], tm)
    mtok = mem.shape[1]
    kv = _norm_matmul(mem.reshape(b * mtok, d), w["g_mem_kv"], w["w_mem_kv"], min(tm, b * mtok))
    x3 = _mem_attn(x2.reshape(b, s, d), w["g_mem_pre"], w["w_mem_q"], kv.reshape(b, mtok, 2 * d),
                   w["w_mem_o"], w["g_mem_post"], tm)
    return _ffn(x3, w["g_ffn_pre"], w["w_ffn_up"], w["ffn_conv"], w["w_ffn_down"],
                w["g_ffn_post"], t["ffn_tm"], t["ffn_fc"])


def kernel(x_prompt, x_sample, mem_prompt, mem_sample, norm_mix_pre, norm_mix_post, w_in, gla_gate_w2_fwd, gla_gate_b_fwd, gla_gate_w2_bwd, gla_gate_b_bwd, gla_out_norm, mla_q_norm, mla_w_q_up, mla_kv_norm, mla_w_kv_up, w_out, norm_mem_pre, norm_mem_post, mem_kv_norm, w_mem_q, w_mem_k, w_mem_v, w_mem_o, norm_ffn_pre, norm_ffn_post, w_ffn_up, ffn_conv_w, ffn_conv_b, w_ffn_down):
    params = dict(
        norm_mix_pre=norm_mix_pre, norm_mix_post=norm_mix_post, w_in=w_in,
        gla_gate_w2_fwd=gla_gate_w2_fwd, gla_gate_b_fwd=gla_gate_b_fwd,
        gla_gate_w2_bwd=gla_gate_w2_bwd, gla_gate_b_bwd=gla_gate_b_bwd, gla_out_norm=gla_out_norm,
        mla_q_norm=mla_q_norm, mla_w_q_up=mla_w_q_up, mla_kv_norm=mla_kv_norm, mla_w_kv_up=mla_w_kv_up,
        w_out=w_out, norm_mem_pre=norm_mem_pre, norm_mem_post=norm_mem_post, mem_kv_norm=mem_kv_norm,
        w_mem_q=w_mem_q, w_mem_k=w_mem_k, w_mem_v=w_mem_v, w_mem_o=w_mem_o,
        norm_ffn_pre=norm_ffn_pre, norm_ffn_post=norm_ffn_post, w_ffn_up=w_ffn_up,
        ffn_conv_w=ffn_conv_w, ffn_conv_b=ffn_conv_b, w_ffn_down=w_ffn_down,
    )
    depth = w_in.shape[0]
    yp, ys = x_prompt, x_sample
    for layer in range(depth):
        p = {k: v[layer] for k, v in params.items()}
        assert yp.shape[1] == ys.shape[1]
        w = _prepare_weights(p, yp.shape[1])
        yp = _apply_layer(yp, mem_prompt, w)
        ys = _apply_layer(ys, mem_sample, w)
    return (yp, ys)
```

```python
import functools
import math

import jax
import jax.numpy as jnp
from jax import lax
from jax.experimental import pallas as pl
from jax.experimental.pallas import tpu as pltpu

F32 = jnp.float32
BF16 = jnp.bfloat16

EPS = 1e-6
D_MODEL = 2048
GLA_HEADS = 4
GLA_DK = 128
GLA_DV = 256
GLA_GATE_RANK = 16
GLA_TAU = 16.0
GLA_CHUNK = 64
GLA_SUB = 256
GLA_QK = GLA_HEADS * GLA_DK
GLA_WIDTH = GLA_HEADS * GLA_DV
MLA_HEADS = 8
MLA_RANK = 512
MLA_NOPE = 128
MLA_ROPE = 64
MLA_V = 128
MLA_QK = MLA_NOPE + MLA_ROPE
MLA_WIDTH = MLA_HEADS * MLA_V
ROPE_BASE = 10000.0
MEM_HEADS = 4
MEM_HEAD_DIM = D_MODEL // MEM_HEADS
D_FF = 5632
CONV_WIDTH = 3
FFN_SUB = 256

LANES = 128
SUBLANES = 8
BF16_ROWS = 16
VMEM_LIMIT_BYTES = 56 * 1024 * 1024
MLA_VT_ROWS = MLA_V + BF16_ROWS
LOG2E = 1.4426950408889634

IN_GQ = 0
IN_GK = IN_GQ + GLA_QK
IN_GV = IN_GK + GLA_QK
IN_GR = IN_GV + GLA_WIDTH
IN_CQ = IN_GR + GLA_WIDTH
IN_CKV = IN_CQ + MLA_RANK
IN_GATE = IN_CKV + MLA_RANK
IN_KR = IN_GATE + LANES
IN_WIDTH_PADDED = IN_KR + LANES


def _cparams(*semantics):
    return pltpu.CompilerParams(dimension_semantics=semantics, vmem_limit_bytes=VMEM_LIMIT_BYTES)


def _resident(shape):
    zeros = (0,) * len(shape)
    return pl.BlockSpec(shape, lambda *_: zeros, pipeline_mode=pl.Buffered(1))


def _rms(x, g):
    return x * lax.rsqrt(jnp.mean(x * x, axis=-1, keepdims=True) + EPS) * g


def _dot(a, b):
    return jnp.dot(a, b, preferred_element_type=F32)


def _dot_nt(a, b):
    return lax.dot_general(a, b, (((1,), (1,)), ((), ())), preferred_element_type=F32)


def _dot_tn(a, b):
    return lax.dot_general(a, b, (((0,), (0,)), ((), ())), preferred_element_type=F32)


def _col_chunk(n):
    for c in (512, 256, LANES):
        if n % c == 0:
            return c
    raise ValueError(f"width {n} is not a multiple of {LANES}")


def _norm_matmul_body(x_ref, g_ref, w_ref, o_ref):
    h = _rms(x_ref[...], g_ref[...]).astype(BF16)
    n = o_ref.shape[-1]
    nc = _col_chunk(n)
    for c in range(0, n, nc):
        o_ref[:, c:c + nc] = _dot(h, w_ref[:, c:c + nc]).astype(o_ref.dtype)


def _norm_matmul(x, g, w, tm):
    t, d = x.shape
    n = w.shape[1]
    return pl.pallas_call(
        _norm_matmul_body,
        out_shape=jax.ShapeDtypeStruct((t, n), BF16),
        grid=(t // tm,),
        in_specs=[pl.BlockSpec((tm, d), lambda i: (i, 0)), _resident((1, d)), _resident((d, n))],
        out_specs=pl.BlockSpec((tm, n), lambda i: (i, 0)),
        compiler_params=_cparams("parallel"),
        name="norm_matmul",
    )(x, g, w)


def _log_sigmoid(x):
    return jnp.minimum(x, 0.0) - jnp.log(1.0 + jnp.exp(-jnp.abs(x)))


def _gla_tables(reverse):
    shift = GLA_CHUNK.bit_length() - 1
    row = lax.broadcasted_iota(jnp.int32, (GLA_SUB, GLA_SUB), 0)
    col = lax.broadcasted_iota(jnp.int32, (GLA_SUB, GLA_SUB), 1)
    same_chunk = lax.shift_right_logical(row, shift) == lax.shift_right_logical(col, shift)
    in_chunk = lambda cond: jnp.where(same_chunk, jnp.where(cond, 1.0, 0.0), 0.0)
    if reverse:
        cum = in_chunk(col >= row)
        keep = in_chunk(col > row) > 0.5
    else:
        cum = in_chunk(col <= row)
        keep = cum > 0.5
    return cum.astype(BF16), keep


def _gla_log_decay(gate_ref, w2_ref, b2_ref, la_ref):
    logits = _dot(gate_ref[0], w2_ref[...]) + b2_ref[...]
    la_ref[...] = _log_sigmoid(logits) * (1.0 / GLA_TAU)


def _gla_block(q_ref, k_ref, v_ref, la_ref, state_ref, reverse):
    ts = q_ref.shape[1]
    c = GLA_CHUNK
    n_chunks = ts // c
    cum_mat, keep = _gla_tables(reverse)
    la = la_ref[...]
    la_hi = la.astype(BF16)
    la_lo = (la - la_hi.astype(F32)).astype(BF16)
    cum = jnp.concatenate(
        [_dot(cum_mat, la_hi[r:r + GLA_SUB]) + _dot(cum_mat, la_lo[r:r + GLA_SUB])
         for r in range(0, ts, GLA_SUB)], axis=0).reshape(n_chunks, c, GLA_QK)
    mid_row, last_row = (c // 2 - 1, 0) if reverse else (c // 2, c - 1)
    mid = cum[:, mid_row:mid_row + 1, :]
    last = cum[:, last_row:last_row + 1, :]
    q = q_ref[0].astype(F32).reshape(n_chunks, c, GLA_QK) * (GLA_DK ** -0.5)
    k = k_ref[0].astype(F32).reshape(n_chunks, c, GLA_QK)
    qi_f = q * jnp.exp(cum - mid)
    ki_f = k * jnp.exp(mid - cum)
    flat = lambda t: t.reshape(ts, GLA_QK).astype(BF16)
    qi = flat(qi_f)
    ki = flat(ki_f)
    qd = flat(qi_f * jnp.exp(mid))
    kd = flat(ki_f * jnp.exp(last - mid))
    chunk_decay = jnp.exp(last)
    order = range(n_chunks - 1, -1, -1) if reverse else range(n_chunks)
    outs = []
    for h in range(GLA_HEADS):
        kc = slice(h * GLA_DK, (h + 1) * GLA_DK)
        v = v_ref[0, :, h * GLA_DV:(h + 1) * GLA_DV]
        intra = []
        for r in range(0, ts, GLA_SUB):
            rows = slice(r, r + GLA_SUB)
            att = jnp.where(keep, _dot_nt(qi[rows, kc], ki[rows, kc]), 0.0).astype(BF16)
            intra.append(_dot(att, v[rows]))
        state = state_ref[h]
        inter = [None] * n_chunks
        for n in order:
            rows = slice(n * c, (n + 1) * c)
            inter[n] = _dot_nt(qd[rows, kc], state.astype(BF16))
            state = state * chunk_decay[n, :, kc] + _dot_tn(v[rows], kd[rows, kc])
        state_ref[h] = state
        outs.append(jnp.concatenate(intra, axis=0) + jnp.concatenate(inter, axis=0))
    return outs


def _gla_fwd_body(q_ref, k_ref, v_ref, gate_ref, w2_ref, b2_ref, o_ref, state_ref, la_ref):
    @pl.when(pl.program_id(1) == 0)
    def _():
        state_ref[...] = jnp.zeros_like(state_ref)

    _gla_log_decay(gate_ref, w2_ref, b2_ref, la_ref)
    outs = _gla_block(q_ref, k_ref, v_ref, la_ref, state_ref, False)
    for h in range(GLA_HEADS):
        o_ref[0, :, h * GLA_DV:(h + 1) * GLA_DV] = outs[h]


def _gla_bwd_body(q_ref, k_ref, v_ref, gate_ref, w2_ref, b2_ref, of_ref, r_ref, gn_ref,
                  o_ref, state_ref, la_ref):
    @pl.when(pl.program_id(1) == 0)
    def _():
        state_ref[...] = jnp.zeros_like(state_ref)

    _gla_log_decay(gate_ref, w2_ref, b2_ref, la_ref)
    outs = _gla_block(q_ref, k_ref, v_ref, la_ref, state_ref, True)
    gn = gn_ref[...]
    for h in range(GLA_HEADS):
        vcols = slice(h * GLA_DV, (h + 1) * GLA_DV)
        o = outs[h] + of_ref[0, :, vcols]
        half_r = 0.5 * r_ref[0, :, vcols].astype(F32)
        silu = half_r + half_r * jnp.tanh(half_r)
        o_ref[0, :, vcols] = (_rms(o, gn) * silu).astype(o_ref.dtype)


def _gla(proj, w2f, b2f, w2b, b2b, out_norm, ts):
    b, s, _ = proj.shape
    ns = s // ts

    def col_spec(width, offset, rev):
        blk = offset // width
        if rev:
            return pl.BlockSpec((1, ts, width), lambda bi, si: (bi, ns - 1 - si, blk))
        return pl.BlockSpec((1, ts, width), lambda bi, si: (bi, si, blk))

    def common_specs(rev):
        return [col_spec(GLA_QK, IN_GQ, rev), col_spec(GLA_QK, IN_GK, rev), col_spec(GLA_WIDTH, IN_GV, rev),
                col_spec(LANES, IN_GATE, rev), _resident((LANES, GLA_QK)), _resident((1, GLA_QK))]

    scratch = [pltpu.VMEM((GLA_HEADS, GLA_DV, GLA_DK), F32), pltpu.VMEM((ts, GLA_QK), F32)]
    o_fwd = pl.pallas_call(
        _gla_fwd_body,
        out_shape=jax.ShapeDtypeStruct((b, s, GLA_WIDTH), F32),
        grid=(b, ns),
        in_specs=common_specs(False),
        out_specs=pl.BlockSpec((1, ts, GLA_WIDTH), lambda bi, si: (bi, si, 0)),
        scratch_shapes=scratch,
        compiler_params=_cparams("parallel", "arbitrary"),
        name="gla_fwd",
    )(proj, proj, proj, proj, w2f, b2f)
    return pl.pallas_call(
        _gla_bwd_body,
        out_shape=jax.ShapeDtypeStruct((b, s, GLA_WIDTH), BF16),
        grid=(b, ns),
        in_specs=common_specs(True) + [
            pl.BlockSpec((1, ts, GLA_WIDTH), lambda bi, si: (bi, ns - 1 - si, 0)),
            col_spec(GLA_WIDTH, IN_GR, True),
            _resident((1, GLA_DV)),
        ],
        out_specs=pl.BlockSpec((1, ts, GLA_WIDTH), lambda bi, si: (bi, ns - 1 - si, 0)),
        scratch_shapes=scratch,
        compiler_params=_cparams("parallel", "arbitrary"),
        name="gla_bwd",
    )(proj, proj, proj, proj, w2b, b2b, o_fwd, proj, out_norm)


def _mla_prep_body(cq_ref, ckv_ref, kr_ref, qn_ref, kvn_ref, wq_nope_ref, wq_rope_ref, wq_rot_ref, wkv_ref,
                   cos_ref, sin_ref, q_out, k_out, vt_out):
    hq = _rms(cq_ref[0].astype(F32), qn_ref[...]).astype(BF16)
    hkv = _rms(ckv_ref[0].astype(F32), kvn_ref[...]).astype(BF16)
    cos = cos_ref[...]
    sin = sin_ref[...]
    q_scale = MLA_QK ** -0.5 * LOG2E
    q_nope = _dot(hq, wq_nope_ref[...]) * q_scale
    q_pe = (_dot(hq, wq_rope_ref[...]) * cos + _dot(hq, wq_rot_ref[...]) * sin) * q_scale
    kv = _dot(hkv, wkv_ref[...])
    kr = kr_ref[0].astype(F32)
    k_pe = (kr[:, :MLA_ROPE] * cos[:, :MLA_ROPE] + kr[:, MLA_ROPE:] * sin[:, :MLA_ROPE]).astype(BF16)
    pad_rows = (MLA_VT_ROWS - MLA_V, cq_ref.shape[1])
    ones_row = jnp.where(lax.broadcasted_iota(jnp.int32, pad_rows, 0) == 0, 1.0, 0.0).astype(BF16)
    for h in range(MLA_HEADS):
        q_out[0, h, :, :MLA_NOPE] = q_nope[:, h * MLA_NOPE:(h + 1) * MLA_NOPE].astype(BF16)
        q_out[0, h, :, MLA_NOPE:] = q_pe[:, h * MLA_ROPE:(h + 1) * MLA_ROPE].astype(BF16)
        base = h * (MLA_NOPE + MLA_V)
        k_out[0, h, :, :MLA_NOPE] = kv[:, base:base + MLA_NOPE].astype(BF16)
        k_out[0, h, :, MLA_NOPE:] = k_pe
        vt_out[0, h, :MLA_V] = kv[:, base + MLA_NOPE:base + MLA_NOPE + MLA_V].T.astype(BF16)
        vt_out[0, h, MLA_V:] = ones_row


def _reduce_rows(op, x, groups=8):
    rows, n = x.shape
    partial = op(x.reshape(groups, rows // groups, n), axis=0)
    return op(partial, axis=0, keepdims=True)


def _mla_attn_body(q_ref, k_ref, vt_ref, o_ref, s_ref, *, tk):
    q = q_ref[0, 0]
    n_chunks = k_ref.shape[2] // tk

    n_slots = s_ref.shape[0]
    chunk_max = [None] * n_chunks

    def scores(c):
        s = _dot_nt(k_ref[0, 0, c * tk:(c + 1) * tk, :], q)
        s_ref[c % n_slots] = s
        chunk_max[c] = _reduce_rows(jnp.max, s)

    m = acc = None
    for c in range(min(n_slots - 1, n_chunks)):
        scores(c)
    for c in range(n_chunks):
        if c + n_slots - 1 < n_chunks:
            scores(c + n_slots - 1)
        m_new = chunk_max[c] if m is None else jnp.maximum(m, chunk_max[c])
        p = jnp.exp2(s_ref[c % n_slots] - m_new).astype(BF16)
        pv = _dot(vt_ref[0, 0, :, c * tk:(c + 1) * tk], p)
        acc = pv if acc is None else jnp.exp2(m - m_new) * acc + pv
        m = m_new
    o_ref[0] = (acc[:MLA_V] / acc[MLA_V:MLA_V + 1]).T.astype(o_ref.dtype)


def _mla(proj, q_norm, kv_norm, wq_nope, wq_rope, wq_rot, wkv, cos_t, sin_t, ts, tq, tk):
    b, s, _ = proj.shape
    rope_w = MLA_HEADS * MLA_ROPE
    q, k, vt = pl.pallas_call(
        _mla_prep_body,
        out_shape=(jax.ShapeDtypeStruct((b, MLA_HEADS, s, MLA_QK), BF16),
                   jax.ShapeDtypeStruct((b, MLA_HEADS, s, MLA_QK), BF16),
                   jax.ShapeDtypeStruct((b, MLA_HEADS, MLA_VT_ROWS, s), BF16)),
        grid=(b, s // ts),
        in_specs=[
            pl.BlockSpec((1, ts, MLA_RANK), lambda bi, si: (bi, si, IN_CQ // MLA_RANK)),
            pl.BlockSpec((1, ts, MLA_RANK), lambda bi, si: (bi, si, IN_CKV // MLA_RANK)),
            pl.BlockSpec((1, ts, LANES), lambda bi, si: (bi, si, IN_KR // LANES)),
            _resident((1, MLA_RANK)), _resident((1, MLA_RANK)),
            _resident((MLA_RANK, MLA_HEADS * MLA_NOPE)), _resident((MLA_RANK, rope_w)),
            _resident((MLA_RANK, rope_w)), _resident((MLA_RANK, MLA_HEADS * (MLA_NOPE + MLA_V))),
            pl.BlockSpec((ts, rope_w), lambda bi, si: (si, 0)),
            pl.BlockSpec((ts, rope_w), lambda bi, si: (si, 0)),
        ],
        out_specs=(pl.BlockSpec((1, MLA_HEADS, ts, MLA_QK), lambda bi, si: (bi, 0, si, 0)),
                   pl.BlockSpec((1, MLA_HEADS, ts, MLA_QK), lambda bi, si: (bi, 0, si, 0)),
                   pl.BlockSpec((1, MLA_HEADS, MLA_VT_ROWS, ts), lambda bi, si: (bi, 0, 0, si))),
        compiler_params=_cparams("parallel", "parallel"),
        name="mla_prep",
    )(proj, proj, proj, q_norm, kv_norm, wq_nope, wq_rope, wq_rot, wkv, cos_t, sin_t)
    return pl.pallas_call(
        functools.partial(_mla_attn_body, tk=tk),
        out_shape=jax.ShapeDtypeStruct((b, s, MLA_WIDTH), BF16),
        grid=(b, MLA_HEADS, s // tq),
        in_specs=[
            pl.BlockSpec((1, 1, tq, MLA_QK), lambda bi, hi, qi: (bi, hi, qi, 0)),
            pl.BlockSpec((1, 1, s, MLA_QK), lambda bi, hi, qi: (bi, hi, 0, 0)),
            pl.BlockSpec((1, 1, MLA_VT_ROWS, s), lambda bi, hi, qi: (bi, hi, 0, 0)),
        ],
        out_specs=pl.BlockSpec((1, tq, MLA_V), lambda bi, hi, qi: (bi, qi, hi)),
        scratch_shapes=[pltpu.VMEM((3, tk, tq), F32)],
        compiler_params=_cparams("parallel", "parallel", "parallel"),
        name="mla_attn",
    )(q, k, vt)


def _residual_norm_store(x, o_ref, sumsq, g):
    scale = lax.rsqrt(sumsq * (1.0 / o_ref.shape[-1]) + EPS)
    o_ref[...] = x + o_ref[...] * scale * g


def _project_rows(o_ref, lhs_and_weights):
    n = o_ref.shape[-1]
    nc = _col_chunk(n)
    sumsq = jnp.zeros((o_ref.shape[0], 1), F32)
    for c in range(0, n, nc):
        y = sum(_dot(a, w_ref[:, c:c + nc]) for a, w_ref in lhs_and_weights)
        sumsq = sumsq + jnp.sum(y * y, axis=-1, keepdims=True)
        o_ref[:, c:c + nc] = y
    return sumsq


def _mix_out_body(a1_ref, a2_ref, w1_ref, w2_ref, x_ref, g_ref, o_ref):
    sumsq = _project_rows(o_ref, [(a1_ref[...], w1_ref), (a2_ref[...], w2_ref)])
    _residual_norm_store(x_ref[...], o_ref, sumsq, g_ref[...])


def _mix_out(o_gla, o_mla, w1, w2, x, g, tm):
    t, d = x.shape
    return pl.pallas_call(
        _mix_out_body,
        out_shape=jax.ShapeDtypeStruct((t, d), F32),
        grid=(t // tm,),
        in_specs=[
            pl.BlockSpec((tm, GLA_WIDTH), lambda i: (i, 0)),
            pl.BlockSpec((tm, MLA_WIDTH), lambda i: (i, 0)),
            _resident((GLA_WIDTH, d)), _resident((MLA_WIDTH, d)),
            pl.BlockSpec((tm, d), lambda i: (i, 0)),
            _resident((1, d)),
        ],
        out_specs=pl.BlockSpec((tm, d), lambda i: (i, 0)),
        compiler_params=_cparams("parallel"),
        name="mix_out",
    )(o_gla, o_mla, w1, w2, x, g)


def _mem_attn_body(x_ref, gpre_ref, wq_ref, kv_ref, wo_ref, gpost_ref, o_ref, a_ref):
    q_scale = MEM_HEAD_DIM ** -0.5 * LOG2E
    x = x_ref[0]
    h = _rms(x, gpre_ref[...]).astype(BF16)
    for hd in range(MEM_HEADS):
        cols = slice(hd * MEM_HEAD_DIM, (hd + 1) * MEM_HEAD_DIM)
        vcols = slice(D_MODEL + hd * MEM_HEAD_DIM, D_MODEL + (hd + 1) * MEM_HEAD_DIM)
        q = (_dot(h, wq_ref[:, cols]) * q_scale).astype(BF16)
        s = _dot_nt(q, kv_ref[0, :, cols])
        m = jnp.max(s, axis=-1, keepdims=True)
        p = jnp.exp2(s - m)
        l = jnp.sum(p, axis=-1, keepdims=True)
        a_ref[:, cols] = (_dot(p.astype(BF16), kv_ref[0, :, vcols]) / l).astype(BF16)
    sumsq = _project_rows(o_ref.at[0], [(a_ref[...], wo_ref)])
    _residual_norm_store(x, o_ref.at[0], sumsq, gpost_ref[...])


def _mem_attn(x, gpre, wq, kv, wo, gpost, tm):
    b, s, d = x.shape
    m = kv.shape[1]
    return pl.pallas_call(
        _mem_attn_body,
        out_shape=jax.ShapeDtypeStruct((b, s, d), F32),
        grid=(b, s // tm),
        in_specs=[
            pl.BlockSpec((1, tm, d), lambda bi, si: (bi, si, 0)),
            _resident((1, d)), _resident((d, d)),
            pl.BlockSpec((1, m, 2 * d), lambda bi, si: (bi, 0, 0)),
            _resident((d, d)), _resident((1, d)),
        ],
        out_specs=pl.BlockSpec((1, tm, d), lambda bi, si: (bi, si, 0)),
        scratch_shapes=[pltpu.VMEM((tm, d), BF16)],
        compiler_params=_cparams("parallel", "parallel"),
        name="mem_attn",
    )(x, gpre, wq, kv, wo, gpost)


def _gelu_tanh_times(x, v_half):
    c = math.sqrt(2.0 / math.pi)
    inner = x * (c + (c * 0.044715) * (x * x))
    return x * (1.0 + jnp.tanh(inner)) * v_half


def _ffn_body(x_ref, xprev_ref, xnext_ref, gpre_ref, wg_ref, wv_ref, cwg_ref, cwv_ref, cbg_ref, cbv_ref,
              wd_ref, gpost_ref, o_ref, h_ref, u_ref):
    si = pl.program_id(1)
    j = pl.program_id(2)
    tm = x_ref.shape[1]
    sub = FFN_SUB
    hl = SUBLANES
    n_lane_tiles = 2 * sub // LANES

    @pl.when(j == 0)
    def _():
        g = gpre_ref[...]
        h_ref[:tm] = _rms(x_ref[0], g).astype(BF16)
        nxt = jnp.where(si == pl.num_programs(1) - 1, 0.0, _rms(xnext_ref[0], g))
        prv = jnp.where(si == 0, 0.0, _rms(xprev_ref[0], g))
        h_ref[tm:] = jnp.concatenate([nxt, prv], axis=0).astype(BF16)
        o_ref[...] = jnp.zeros_like(o_ref)

    h = h_ref[...]
    n_sub = wd_ref.shape[0] // sub
    cols = [slice(c * sub, (c + 1) * sub) for c in range(n_sub)]

    def gate_value(g_ref, v_ref, c):
        return jnp.concatenate([g_ref[:, cols[c]], v_ref[:, cols[c]]], axis=1)

    def up_project(c):
        u = _dot(h, gate_value(wg_ref, wv_ref, c))
        for k in range(n_lane_tiles):
            lanes = slice(k * LANES, (k + 1) * LANES)
            u_ref[c % 2, k, hl:hl + tm] = u[:tm, lanes]
            u_ref[c % 2, k, :hl] = u[tm + hl:, lanes]
            u_ref[c % 2, k, hl + tm:] = u[tm:tm + hl, lanes]

    def window(c, offset):
        return jnp.concatenate([u_ref[c % 2, k, hl + offset:hl + offset + tm] for k in range(n_lane_tiles)],
                               axis=1)

    def gate_and_down(c):
        cw = gate_value(cwg_ref, cwv_ref, c)
        cu = (window(c, -1) * cw[0:1] + window(c, 0) * cw[1:2] + window(c, 1) * cw[2:3]
              + gate_value(cbg_ref, cbv_ref, c))
        act = _gelu_tanh_times(cu[:, :sub], cu[:, sub:]).astype(BF16)
        n = o_ref.shape[-1]
        nc = _col_chunk(n)
        for k in range(0, n, nc):
            o_ref[0, :, k:k + nc] += _dot(act, wd_ref[cols[c], k:k + nc])

    up_project(0)
    for c in range(n_sub):
        if c + 1 < n_sub:
            up_project(c + 1)
        gate_and_down(c)

    @pl.when(j == pl.num_programs(2) - 1)
    def _():
        o_ref[0] = x_ref[0] + _rms(o_ref[0], gpost_ref[...])


def _ffn(x, gpre, w_up, conv_w, conv_b, w_down, gpost, tm, fc):
    b, s, d = x.shape
    nf = D_FF // fc
    halo = SUBLANES
    tb = tm // halo
    last_blk = s // halo - 1
    return pl.pallas_call(
        _ffn_body,
        out_shape=jax.ShapeDtypeStruct((b, s, d), F32),
        grid=(b, s // tm, nf),
        in_specs=[
            pl.BlockSpec((1, tm, d), lambda bi, si, j: (bi, si, 0)),
            pl.BlockSpec((1, halo, d), lambda bi, si, j: (bi, jnp.maximum(si * tb - 1, 0), 0)),
            pl.BlockSpec((1, halo, d), lambda bi, si, j: (bi, jnp.minimum((si + 1) * tb, last_blk), 0)),
            _resident((1, d)),
            pl.BlockSpec((d, fc), lambda bi, si, j: (0, j)),
            pl.BlockSpec((d, fc), lambda bi, si, j: (0, nf + j)),
            pl.BlockSpec((CONV_WIDTH, fc), lambda bi, si, j: (0, j)),
            pl.BlockSpec((CONV_WIDTH, fc), lambda bi, si, j: (0, nf + j)),
            pl.BlockSpec((1, fc), lambda bi, si, j: (0, j)),
            pl.BlockSpec((1, fc), lambda bi, si, j: (0, nf + j)),
            pl.BlockSpec((fc, d), lambda bi, si, j: (j, 0)),
            _resident((1, d)),
        ],
        out_specs=pl.BlockSpec((1, tm, d), lambda bi, si, j: (bi, si, 0)),
        scratch_shapes=[pltpu.VMEM((tm + 2 * halo, d), BF16),
                        pltpu.VMEM((2, 2 * FFN_SUB // LANES, tm + 2 * halo, LANES), F32)],
        compiler_params=_cparams("parallel", "parallel", "arbitrary"),
        name="conv_ffn",
    )(x, x, x, gpre, w_up, w_up, conv_w, conv_w, conv_b, conv_b, w_down, gpost)


def _rotate_half_columns(w):
    k, n = w.shape
    g = w.reshape(k, n // MLA_ROPE, 2, MLA_ROPE // 2)
    return jnp.concatenate([-g[:, :, 1:], g[:, :, :1]], axis=2).reshape(k, n)


def _prepare_weights(p, seq):
    row = lambda v: v.reshape(1, -1).astype(F32)
    w_in = p["w_in"]
    offs, acc = [], 0
    for wdt in (GLA_QK, GLA_QK, GLA_WIDTH, GLA_WIDTH, GLA_GATE_RANK, GLA_GATE_RANK, MLA_RANK, MLA_RANK, MLA_ROPE):
        offs.append((acc, acc + wdt))
        acc += wdt
    part = [w_in[:, a:b] for a, b in offs]
    gq, gk, gv, gr, ggf, ggb, cq, ckv, kr = part
    d = w_in.shape[0]
    gate_pad = jnp.zeros((d, LANES - 2 * GLA_GATE_RANK), w_in.dtype)
    w_in_p = jnp.concatenate([gq, gk, gv, gr, cq, ckv, ggf, ggb, gate_pad, kr, _rotate_half_columns(kr)],
                             axis=1).astype(BF16)

    def gate_w2(w2, slot):
        full = jnp.zeros((LANES, GLA_QK), F32)
        return full.at[slot * GLA_GATE_RANK:(slot + 1) * GLA_GATE_RANK].set(w2).astype(BF16)

    wq = p["mla_w_q_up"].reshape(MLA_RANK, MLA_HEADS, MLA_QK)
    wq_nope = wq[:, :, :MLA_NOPE].reshape(MLA_RANK, MLA_HEADS * MLA_NOPE)
    wq_rope = wq[:, :, MLA_NOPE:].reshape(MLA_RANK, MLA_HEADS * MLA_ROPE)

    half = MLA_ROPE // 2
    freqs = ROPE_BASE ** (-jnp.arange(half, dtype=F32) / half)
    ang = jnp.arange(seq, dtype=F32)[:, None] * freqs[None, :]
    cos_t = jnp.tile(jnp.cos(ang), (1, 2 * MLA_HEADS))
    sin_t = jnp.tile(jnp.sin(ang), (1, 2 * MLA_HEADS))

    value_half = jnp.concatenate([jnp.ones((1, D_FF), F32), jnp.full((1, D_FF), 0.5, F32)], axis=1)

    return dict(
        g_mix_pre=row(p["norm_mix_pre"]), g_mix_post=row(p["norm_mix_post"]), w_in=w_in_p,
        w2f=gate_w2(p["gla_gate_w2_fwd"], 0), b2f=row(p["gla_gate_b_fwd"]),
        w2b=gate_w2(p["gla_gate_w2_bwd"], 1), b2b=row(p["gla_gate_b_bwd"]),
        gla_out_norm=row(p["gla_out_norm"]),
        mla_q_norm=row(p["mla_q_norm"]), mla_kv_norm=row(p["mla_kv_norm"]),
        wq_nope=wq_nope.astype(BF16), wq_rope=wq_rope.astype(BF16),
        wq_rot=_rotate_half_columns(wq_rope).astype(BF16), wkv=p["mla_w_kv_up"].astype(BF16),
        cos=cos_t, sin=sin_t,
        w_out_gla=p["w_out"][:GLA_WIDTH].astype(BF16), w_out_mla=p["w_out"][GLA_WIDTH:].astype(BF16),
        g_mem_pre=row(p["norm_mem_pre"]), g_mem_post=row(p["norm_mem_post"]), g_mem_kv=row(p["mem_kv_norm"]),
        w_mem_q=p["w_mem_q"].astype(BF16), w_mem_o=p["w_mem_o"].astype(BF16),
        w_mem_kv=jnp.concatenate([p["w_mem_k"], p["w_mem_v"]], axis=1).astype(BF16),
        g_ffn_pre=row(p["norm_ffn_pre"]), g_ffn_post=row(p["norm_ffn_post"]),
        w_ffn_up=p["w_ffn_up"].astype(BF16),
        ffn_conv_w=p["ffn_conv_w"].astype(F32) * value_half,
        ffn_conv_b=row(p["ffn_conv_b"]) * value_half,
        w_ffn_down=p["w_ffn_down"].astype(BF16),
    )


def _tiles(seq):
    tm = min(512, seq)
    return dict(tm=tm, gla_ts=tm, mla_ts=tm, mla_tq=min(1024, seq), mla_tk=tm, ffn_tm=tm, ffn_fc=512)


def _apply_layer(x, mem, w):
    b, s, d = x.shape
    t = _tiles(s)
    tm = t["tm"]
    x2 = x.reshape(b * s, d)
    proj = _norm_matmul(x2, w["g_mix_pre"], w["w_in"], tm).reshape(b, s, IN_WIDTH_PADDED)
    o_gla = _gla(proj, w["w2f"], w["b2f"], w["w2b"], w["b2b"], w["gla_out_norm"], t["gla_ts"])
    o_mla = _mla(proj, w["mla_q_norm"], w["mla_kv_norm"], w["wq_nope"], w["wq_rope"], w["wq_rot"], w["wkv"],
                 w["cos"], w["sin"], t["mla_ts"], t["mla_tq"], t["mla_tk"])
    x2 = _mix_out(o_gla.reshape(b * s, GLA_WIDTH), o_mla.reshape(b * s, MLA_WIDTH),
                  w["w_out_gla"], w["w_out_mla"], x2, w["g_mix_post"], tm)
    mtok = mem.shape[1]
    kv = _norm_matmul(mem.reshape(b * mtok, d), w["g_mem_kv"], w["w_mem_kv"], min(tm, b * mtok))
    x3 = _mem_attn(x2.reshape(b, s, d), w["g_mem_pre"], w["w_mem_q"], kv.reshape(b, mtok, 2 * d),
                   w["w_mem_o"], w["g_mem_post"], tm)
    return _ffn(x3, w["g_ffn_pre"], w["w_ffn_up"], w["ffn_conv_w"], w["ffn_conv_b"], w["w_ffn_down"],
                w["g_ffn_post"], t["ffn_tm"], t["ffn_fc"])


def kernel(x_prompt, x_sample, mem_prompt, mem_sample, norm_mix_pre, norm_mix_post, w_in, gla_gate_w2_fwd, gla_gate_b_fwd, gla_gate_w2_bwd, gla_gate_b_bwd, gla_out_norm, mla_q_norm, mla_w_q_up, mla_kv_norm, mla_w_kv_up, w_out, norm_mem_pre, norm_mem_post, mem_kv_norm, w_mem_q, w_mem_k, w_mem_v, w_mem_o, norm_ffn_pre, norm_ffn_post, w_ffn_up, ffn_conv_w, ffn_conv_b, w_ffn_down):
    params = dict(
        norm_mix_pre=norm_mix_pre, norm_mix_post=norm_mix_post, w_in=w_in,
        gla_gate_w2_fwd=gla_gate_w2_fwd, gla_gate_b_fwd=gla_gate_b_fwd,
        gla_gate_w2_bwd=gla_gate_w2_bwd, gla_gate_b_bwd=gla_gate_b_bwd, gla_out_norm=gla_out_norm,
        mla_q_norm=mla_q_norm, mla_w_q_up=mla_w_q_up, mla_kv_norm=mla_kv_norm, mla_w_kv_up=mla_w_kv_up,
        w_out=w_out, norm_mem_pre=norm_mem_pre, norm_mem_post=norm_mem_post, mem_kv_norm=mem_kv_norm,
        w_mem_q=w_mem_q, w_mem_k=w_mem_k, w_mem_v=w_mem_v, w_mem_o=w_mem_o,
        norm_ffn_pre=norm_ffn_pre, norm_ffn_post=norm_ffn_post, w_ffn_up=w_ffn_up,
        ffn_conv_w=ffn_conv_w, ffn_conv_b=ffn_conv_b, w_ffn_down=w_ffn_down,
    )
    depth = w_in.shape[0]
    yp, ys = x_prompt, x_sample
    for layer in range(depth):
        p = {k: v[layer] for k, v in params.items()}
        assert yp.shape[1] == ys.shape[1]
        w = _prepare_weights(p, yp.shape[1])
        yp = _apply_layer(yp, mem_prompt, w)
        ys = _apply_layer(ys, mem_sample, w)
    return (yp, ys)
```

```python
import functools
import math

import jax
import jax.numpy as jnp
from jax import lax
from jax.experimental import pallas as pl
from jax.experimental.pallas import tpu as pltpu

F32 = jnp.float32
BF16 = jnp.bfloat16

EPS = 1e-6
D_MODEL = 2048
GLA_HEADS = 4
GLA_DK = 128
GLA_DV = 256
GLA_GATE_RANK = 16
GLA_TAU = 16.0
GLA_CHUNK = 64
GLA_SUB = 256
GLA_QK = GLA_HEADS * GLA_DK
GLA_WIDTH = GLA_HEADS * GLA_DV
MLA_HEADS = 8
MLA_RANK = 512
MLA_NOPE = 128
MLA_ROPE = 64
MLA_V = 128
MLA_QK = MLA_NOPE + MLA_ROPE
MLA_WIDTH = MLA_HEADS * MLA_V
ROPE_BASE = 10000.0
MEM_HEADS = 4
MEM_HEAD_DIM = D_MODEL // MEM_HEADS
D_FF = 5632
CONV_WIDTH = 3
FFN_SUB = 256

LANES = 128
SUBLANES = 8
BF16_ROWS = 16
VMEM_LIMIT_BYTES = 56 * 1024 * 1024
MLA_VT_ROWS = MLA_V + BF16_ROWS
LOG2E = 1.4426950408889634

IN_GQ = 0
IN_GK = IN_GQ + GLA_QK
IN_GV = IN_GK + GLA_QK
IN_GR = IN_GV + GLA_WIDTH
IN_CQ = IN_GR + GLA_WIDTH
IN_CKV = IN_CQ + MLA_RANK
IN_GATE = IN_CKV + MLA_RANK
IN_KR = IN_GATE + LANES
IN_WIDTH_PADDED = IN_KR + LANES


def _cparams(*semantics):
    return pltpu.CompilerParams(dimension_semantics=semantics, vmem_limit_bytes=VMEM_LIMIT_BYTES)


def _resident(shape):
    zeros = (0,) * len(shape)
    return pl.BlockSpec(shape, lambda *_: zeros, pipeline_mode=pl.Buffered(1))


def _rms(x, g):
    return x * lax.rsqrt(jnp.mean(x * x, axis=-1, keepdims=True) + EPS) * g


def _dot(a, b):
    return jnp.dot(a, b, preferred_element_type=F32)


def _dot_nt(a, b):
    return lax.dot_general(a, b, (((1,), (1,)), ((), ())), preferred_element_type=F32)


def _dot_tn(a, b):
    return lax.dot_general(a, b, (((0,), (0,)), ((), ())), preferred_element_type=F32)


def _col_chunk(n):
    for c in (512, 256, LANES):
        if n % c == 0:
            return c
    raise ValueError(f"width {n} is not a multiple of {LANES}")


def _norm_matmul_body(x_ref, g_ref, w_ref, o_ref):
    h = _rms(x_ref[...], g_ref[...]).astype(BF16)
    n = o_ref.shape[-1]
    nc = _col_chunk(n)
    for c in range(0, n, nc):
        o_ref[:, c:c + nc] = _dot(h, w_ref[:, c:c + nc]).astype(o_ref.dtype)


def _norm_matmul(x, g, w, tm):
    t, d = x.shape
    n = w.shape[1]
    return pl.pallas_call(
        _norm_matmul_body,
        out_shape=jax.ShapeDtypeStruct((t, n), BF16),
        grid=(t // tm,),
        in_specs=[pl.BlockSpec((tm, d), lambda i: (i, 0)), _resident((1, d)), _resident((d, n))],
        out_specs=pl.BlockSpec((tm, n), lambda i: (i, 0)),
        compiler_params=_cparams("parallel"),
        name="norm_matmul",
    )(x, g, w)


def _log_sigmoid(x):
    return jnp.minimum(x, 0.0) - jnp.log(1.0 + jnp.exp(-jnp.abs(x)))


def _gla_tables(reverse):
    shift = GLA_CHUNK.bit_length() - 1
    row = lax.broadcasted_iota(jnp.int32, (GLA_SUB, GLA_SUB), 0)
    col = lax.broadcasted_iota(jnp.int32, (GLA_SUB, GLA_SUB), 1)
    same_chunk = lax.shift_right_logical(row, shift) == lax.shift_right_logical(col, shift)
    in_chunk = lambda cond: jnp.where(same_chunk, jnp.where(cond, 1.0, 0.0), 0.0)
    if reverse:
        cum = in_chunk(col >= row)
        keep = in_chunk(col > row) > 0.5
    else:
        cum = in_chunk(col <= row)
        keep = cum > 0.5
    return cum.astype(BF16), keep


def _gla_log_decay(gate_ref, w2_ref, b2_ref, la_ref):
    logits = _dot(gate_ref[0], w2_ref[...]) + b2_ref[...]
    la_ref[...] = _log_sigmoid(logits) * (1.0 / GLA_TAU)


def _gla_block(q_ref, k_ref, v_ref, la_ref, state_ref, reverse):
    ts = q_ref.shape[1]
    c = GLA_CHUNK
    n_chunks = ts // c
    cum_mat, keep = _gla_tables(reverse)
    la = la_ref[...]
    la_hi = la.astype(BF16)
    la_lo = (la - la_hi.astype(F32)).astype(BF16)
    cum = jnp.concatenate(
        [_dot(cum_mat, la_hi[r:r + GLA_SUB]) + _dot(cum_mat, la_lo[r:r + GLA_SUB])
         for r in range(0, ts, GLA_SUB)], axis=0).reshape(n_chunks, c, GLA_QK)
    mid_row, last_row = (c // 2 - 1, 0) if reverse else (c // 2, c - 1)
    mid = cum[:, mid_row:mid_row + 1, :]
    last = cum[:, last_row:last_row + 1, :]
    q = q_ref[0].astype(F32).reshape(n_chunks, c, GLA_QK) * (GLA_DK ** -0.5)
    k = k_ref[0].astype(F32).reshape(n_chunks, c, GLA_QK)
    qi_f = q * jnp.exp(cum - mid)
    ki_f = k * jnp.exp(mid - cum)
    flat = lambda t: t.reshape(ts, GLA_QK).astype(BF16)
    qi = flat(qi_f)
    ki = flat(ki_f)
    qd = flat(qi_f * jnp.exp(mid))
    kd = flat(ki_f * jnp.exp(last - mid))
    chunk_decay = jnp.exp(last)
    order = range(n_chunks - 1, -1, -1) if reverse else range(n_chunks)
    outs = []
    for h in range(GLA_HEADS):
        kc = slice(h * GLA_DK, (h + 1) * GLA_DK)
        v = v_ref[0, :, h * GLA_DV:(h + 1) * GLA_DV]
        intra = []
        for r in range(0, ts, GLA_SUB):
            rows = slice(r, r + GLA_SUB)
            att = jnp.where(keep, _dot_nt(qi[rows, kc], ki[rows, kc]), 0.0).astype(BF16)
            intra.append(_dot(att, v[rows]))
        state = state_ref[h]
        inter = [None] * n_chunks
        for n in order:
            rows = slice(n * c, (n + 1) * c)
            inter[n] = _dot_nt(qd[rows, kc], state.astype(BF16))
            state = state * chunk_decay[n, :, kc] + _dot_tn(v[rows], kd[rows, kc])
        state_ref[h] = state
        outs.append(jnp.concatenate(intra, axis=0) + jnp.concatenate(inter, axis=0))
    return outs


def _gla_fwd_body(q_ref, k_ref, v_ref, gate_ref, w2_ref, b2_ref, o_ref, state_ref, la_ref):
    @pl.when(pl.program_id(1) == 0)
    def _():
        state_ref[...] = jnp.zeros_like(state_ref)

    _gla_log_decay(gate_ref, w2_ref, b2_ref, la_ref)
    outs = _gla_block(q_ref, k_ref, v_ref, la_ref, state_ref, False)
    for h in range(GLA_HEADS):
        o_ref[0, :, h * GLA_DV:(h + 1) * GLA_DV] = outs[h]


def _gla_bwd_body(q_ref, k_ref, v_ref, gate_ref, w2_ref, b2_ref, of_ref, r_ref, gn_ref,
                  o_ref, state_ref, la_ref):
    @pl.when(pl.program_id(1) == 0)
    def _():
        state_ref[...] = jnp.zeros_like(state_ref)

    _gla_log_decay(gate_ref, w2_ref, b2_ref, la_ref)
    outs = _gla_block(q_ref, k_ref, v_ref, la_ref, state_ref, True)
    gn = gn_ref[...]
    for h in range(GLA_HEADS):
        vcols = slice(h * GLA_DV, (h + 1) * GLA_DV)
        o = outs[h] + of_ref[0, :, vcols]
        half_r = 0.5 * r_ref[0, :, vcols].astype(F32)
        silu = half_r + half_r * jnp.tanh(half_r)
        o_ref[0, :, vcols] = (_rms(o, gn) * silu).astype(o_ref.dtype)


def _gla(proj, w2f, b2f, w2b, b2b, out_norm, ts):
    b, s, _ = proj.shape
    ns = s // ts

    def col_spec(width, offset, rev):
        blk = offset // width
        if rev:
            return pl.BlockSpec((1, ts, width), lambda bi, si: (bi, ns - 1 - si, blk))
        return pl.BlockSpec((1, ts, width), lambda bi, si: (bi, si, blk))

    def common_specs(rev):
        return [col_spec(GLA_QK, IN_GQ, rev), col_spec(GLA_QK, IN_GK, rev), col_spec(GLA_WIDTH, IN_GV, rev),
                col_spec(LANES, IN_GATE, rev), _resident((LANES, GLA_QK)), _resident((1, GLA_QK))]

    scratch = [pltpu.VMEM((GLA_HEADS, GLA_DV, GLA_DK), F32), pltpu.VMEM((ts, GLA_QK), F32)]
    o_fwd = pl.pallas_call(
        _gla_fwd_body,
        out_shape=jax.ShapeDtypeStruct((b, s, GLA_WIDTH), F32),
        grid=(b, ns),
        in_specs=common_specs(False),
        out_specs=pl.BlockSpec((1, ts, GLA_WIDTH), lambda bi, si: (bi, si, 0)),
        scratch_shapes=scratch,
        compiler_params=_cparams("parallel", "arbitrary"),
        name="gla_fwd",
    )(proj, proj, proj, proj, w2f, b2f)
    return pl.pallas_call(
        _gla_bwd_body,
        out_shape=jax.ShapeDtypeStruct((b, s, GLA_WIDTH), BF16),
        grid=(b, ns),
        in_specs=common_specs(True) + [
            pl.BlockSpec((1, ts, GLA_WIDTH), lambda bi, si: (bi, ns - 1 - si, 0)),
            col_spec(GLA_WIDTH, IN_GR, True),
            _resident((1, GLA_DV)),
        ],
        out_specs=pl.BlockSpec((1, ts, GLA_WIDTH), lambda bi, si: (bi, ns - 1 - si, 0)),
        scratch_shapes=scratch,
        compiler_params=_cparams("parallel", "arbitrary"),
        name="gla_bwd",
    )(proj, proj, proj, proj, w2b, b2b, o_fwd, proj, out_norm)


def _mla_prep_body(cq_ref, ckv_ref, kr_ref, qn_ref, kvn_ref, wq_nope_ref, wq_rope_ref, wq_rot_ref, wkv_ref,
                   cos_ref, sin_ref, q_out, k_out, vt_out):
    hq = _rms(cq_ref[0].astype(F32), qn_ref[...]).astype(BF16)
    hkv = _rms(ckv_ref[0].astype(F32), kvn_ref[...]).astype(BF16)
    cos = cos_ref[...]
    sin = sin_ref[...]
    q_scale = MLA_QK ** -0.5 * LOG2E
    q_nope = _dot(hq, wq_nope_ref[...]) * q_scale
    q_pe = (_dot(hq, wq_rope_ref[...]) * cos + _dot(hq, wq_rot_ref[...]) * sin) * q_scale
    kv = _dot(hkv, wkv_ref[...])
    kr = kr_ref[0].astype(F32)
    k_pe = (kr[:, :MLA_ROPE] * cos[:, :MLA_ROPE] + kr[:, MLA_ROPE:] * sin[:, :MLA_ROPE]).astype(BF16)
    pad_rows = (MLA_VT_ROWS - MLA_V, cq_ref.shape[1])
    ones_row = jnp.where(lax.broadcasted_iota(jnp.int32, pad_rows, 0) == 0, 1.0, 0.0).astype(BF16)
    for h in range(MLA_HEADS):
        q_out[0, h, :, :MLA_NOPE] = q_nope[:, h * MLA_NOPE:(h + 1) * MLA_NOPE].astype(BF16)
        q_out[0, h, :, MLA_NOPE:] = q_pe[:, h * MLA_ROPE:(h + 1) * MLA_ROPE].astype(BF16)
        base = h * (MLA_NOPE + MLA_V)
        k_out[0, h, :, :MLA_NOPE] = kv[:, base:base + MLA_NOPE].astype(BF16)
        k_out[0, h, :, MLA_NOPE:] = k_pe
        vt_out[0, h, :MLA_V] = kv[:, base + MLA_NOPE:base + MLA_NOPE + MLA_V].T.astype(BF16)
        vt_out[0, h, MLA_V:] = ones_row


def _reduce_rows(op, x, groups=8):
    rows, n = x.shape
    partial = op(x.reshape(groups, rows // groups, n), axis=0)
    return op(partial, axis=0, keepdims=True)


def _mla_attn_body(q_ref, k_ref, vt_ref, o_ref, s_ref, *, tk):
    q = q_ref[0, 0]
    n_chunks = k_ref.shape[2] // tk

    n_slots = s_ref.shape[0]
    chunk_max = [None] * n_chunks

    def scores(c):
        s = _dot_nt(k_ref[0, 0, c * tk:(c + 1) * tk, :], q)
        s_ref[c % n_slots] = s
        chunk_max[c] = _reduce_rows(jnp.max, s)

    m = acc = None
    for c in range(min(n_slots - 1, n_chunks)):
        scores(c)
    for c in range(n_chunks):
        if c + n_slots - 1 < n_chunks:
            scores(c + n_slots - 1)
        m_new = chunk_max[c] if m is None else jnp.maximum(m, chunk_max[c])
        p = jnp.exp2(s_ref[c % n_slots] - m_new).astype(BF16)
        pv = _dot(vt_ref[0, 0, :, c * tk:(c + 1) * tk], p)
        acc = pv if acc is None else jnp.exp2(m - m_new) * acc + pv
        m = m_new
    o_ref[0] = (acc[:MLA_V] / acc[MLA_V:MLA_V + 1]).T.astype(o_ref.dtype)


def _mla(proj, q_norm, kv_norm, wq_nope, wq_rope, wq_rot, wkv, cos_t, sin_t, ts, tq, tk):
    b, s, _ = proj.shape
    rope_w = MLA_HEADS * MLA_ROPE
    q, k, vt = pl.pallas_call(
        _mla_prep_body,
        out_shape=(jax.ShapeDtypeStruct((b, MLA_HEADS, s, MLA_QK), BF16),
                   jax.ShapeDtypeStruct((b, MLA_HEADS, s, MLA_QK), BF16),
                   jax.ShapeDtypeStruct((b, MLA_HEADS, MLA_VT_ROWS, s), BF16)),
        grid=(b, s // ts),
        in_specs=[
            pl.BlockSpec((1, ts, MLA_RANK), lambda bi, si: (bi, si, IN_CQ // MLA_RANK)),
            pl.BlockSpec((1, ts, MLA_RANK), lambda bi, si: (bi, si, IN_CKV // MLA_RANK)),
            pl.BlockSpec((1, ts, LANES), lambda bi, si: (bi, si, IN_KR // LANES)),
            _resident((1, MLA_RANK)), _resident((1, MLA_RANK)),
            _resident((MLA_RANK, MLA_HEADS * MLA_NOPE)), _resident((MLA_RANK, rope_w)),
            _resident((MLA_RANK, rope_w)), _resident((MLA_RANK, MLA_HEADS * (MLA_NOPE + MLA_V))),
            pl.BlockSpec((ts, rope_w), lambda bi, si: (si, 0)),
            pl.BlockSpec((ts, rope_w), lambda bi, si: (si, 0)),
        ],
        out_specs=(pl.BlockSpec((1, MLA_HEADS, ts, MLA_QK), lambda bi, si: (bi, 0, si, 0)),
                   pl.BlockSpec((1, MLA_HEADS, ts, MLA_QK), lambda bi, si: (bi, 0, si, 0)),
                   pl.BlockSpec((1, MLA_HEADS, MLA_VT_ROWS, ts), lambda bi, si: (bi, 0, 0, si))),
        compiler_params=_cparams("parallel", "parallel"),
        name="mla_prep",
    )(proj, proj, proj, q_norm, kv_norm, wq_nope, wq_rope, wq_rot, wkv, cos_t, sin_t)
    return pl.pallas_call(
        functools.partial(_mla_attn_body, tk=tk),
        out_shape=jax.ShapeDtypeStruct((b, s, MLA_WIDTH), BF16),
        grid=(b, MLA_HEADS, s // tq),
        in_specs=[
            pl.BlockSpec((1, 1, tq, MLA_QK), lambda bi, hi, qi: (bi, hi, qi, 0)),
            pl.BlockSpec((1, 1, s, MLA_QK), lambda bi, hi, qi: (bi, hi, 0, 0)),
            pl.BlockSpec((1, 1, MLA_VT_ROWS, s), lambda bi, hi, qi: (bi, hi, 0, 0)),
        ],
        out_specs=pl.BlockSpec((1, tq, MLA_V), lambda bi, hi, qi: (bi, qi, hi)),
        scratch_shapes=[pltpu.VMEM((3, tk, tq), F32)],
        compiler_params=_cparams("parallel", "parallel", "parallel"),
        name="mla_attn",
    )(q, k, vt)


def _residual_norm_store(x, o_ref, sumsq, g):
    scale = lax.rsqrt(sumsq * (1.0 / o_ref.shape[-1]) + EPS)
    o_ref[...] = x + o_ref[...] * scale * g


def _project_rows(o_ref, lhs_and_weights):
    n = o_ref.shape[-1]
    nc = _col_chunk(n)
    sumsq = jnp.zeros((o_ref.shape[0], 1), F32)
    for c in range(0, n, nc):
        y = sum(_dot(a, w_ref[:, c:c + nc]) for a, w_ref in lhs_and_weights)
        sumsq = sumsq + jnp.sum(y * y, axis=-1, keepdims=True)
        o_ref[:, c:c + nc] = y
    return sumsq


def _mix_out_body(a1_ref, a2_ref, w1_ref, w2_ref, x_ref, g_ref, o_ref):
    sumsq = _project_rows(o_ref, [(a1_ref[...], w1_ref), (a2_ref[...], w2_ref)])
    _residual_norm_store(x_ref[...], o_ref, sumsq, g_ref[...])


def _mix_out(o_gla, o_mla, w1, w2, x, g, tm):
    t, d = x.shape
    return pl.pallas_call(
        _mix_out_body,
        out_shape=jax.ShapeDtypeStruct((t, d), F32),
        grid=(t // tm,),
        in_specs=[
            pl.BlockSpec((tm, GLA_WIDTH), lambda i: (i, 0)),
            pl.BlockSpec((tm, MLA_WIDTH), lambda i: (i, 0)),
            _resident((GLA_WIDTH, d)), _resident((MLA_WIDTH, d)),
            pl.BlockSpec((tm, d), lambda i: (i, 0)),
            _resident((1, d)),
        ],
        out_specs=pl.BlockSpec((tm, d), lambda i: (i, 0)),
        compiler_params=_cparams("parallel"),
        name="mix_out",
    )(o_gla, o_mla, w1, w2, x, g)


def _mem_attn_body(x_ref, gpre_ref, wq_ref, kv_ref, wo_ref, gpost_ref, o_ref, a_ref, q_ref):
    q_scale = MEM_HEAD_DIM ** -0.5 * LOG2E
    x = x_ref[0]
    h = _rms(x, gpre_ref[...]).astype(BF16)
    head_cols = [slice(hd * MEM_HEAD_DIM, (hd + 1) * MEM_HEAD_DIM) for hd in range(MEM_HEADS)]

    def project_q(hd):
        q_ref[hd % 2] = (_dot(h, wq_ref[:, head_cols[hd]]) * q_scale).astype(BF16)

    project_q(0)
    for hd in range(MEM_HEADS):
        if hd + 1 < MEM_HEADS:
            project_q(hd + 1)
        cols = head_cols[hd]
        vcols = slice(D_MODEL + hd * MEM_HEAD_DIM, D_MODEL + (hd + 1) * MEM_HEAD_DIM)
        s = _dot_nt(q_ref[hd % 2], kv_ref[0, :, cols])
        m = jnp.max(s, axis=-1, keepdims=True)
        p = jnp.exp2(s - m)
        l = jnp.sum(p, axis=-1, keepdims=True)
        a_ref[:, cols] = (_dot(p.astype(BF16), kv_ref[0, :, vcols]) / l).astype(BF16)
    sumsq = _project_rows(o_ref.at[0], [(a_ref[...], wo_ref)])
    _residual_norm_store(x, o_ref.at[0], sumsq, gpost_ref[...])


def _mem_attn(x, gpre, wq, kv, wo, gpost, tm):
    b, s, d = x.shape
    m = kv.shape[1]
    return pl.pallas_call(
        _mem_attn_body,
        out_shape=jax.ShapeDtypeStruct((b, s, d), F32),
        grid=(b, s // tm),
        in_specs=[
            pl.BlockSpec((1, tm, d), lambda bi, si: (bi, si, 0)),
            _resident((1, d)), _resident((d, d)),
            pl.BlockSpec((1, m, 2 * d), lambda bi, si: (bi, 0, 0)),
            _resident((d, d)), _resident((1, d)),
        ],
        out_specs=pl.BlockSpec((1, tm, d), lambda bi, si: (bi, si, 0)),
        scratch_shapes=[pltpu.VMEM((tm, d), BF16), pltpu.VMEM((2, tm, MEM_HEAD_DIM), BF16)],
        compiler_params=_cparams("parallel", "parallel"),
        name="mem_attn",
    )(x, gpre, wq, kv, wo, gpost)


def _gelu_tanh_times(x, v_half):
    c = math.sqrt(2.0 / math.pi)
    inner = x * (c + (c * 0.044715) * (x * x))
    return x * (1.0 + jnp.tanh(inner)) * v_half


def _ffn_body(x_ref, xprev_ref, xnext_ref, gpre_ref, wg_ref, wv_ref, cwg_ref, cwv_ref, cbg_ref, cbv_ref,
              wd_ref, gpost_ref, o_ref, h_ref, u_ref):
    si = pl.program_id(1)
    j = pl.program_id(2)
    tm = x_ref.shape[1]
    sub = FFN_SUB
    hl = SUBLANES
    n_lane_tiles = 2 * sub // LANES

    @pl.when(j == 0)
    def _():
        g = gpre_ref[...]
        h_ref[:tm] = _rms(x_ref[0], g).astype(BF16)
        nxt = jnp.where(si == pl.num_programs(1) - 1, 0.0, _rms(xnext_ref[0], g))
        prv = jnp.where(si == 0, 0.0, _rms(xprev_ref[0], g))
        h_ref[tm:] = jnp.concatenate([nxt, prv], axis=0).astype(BF16)
        o_ref[...] = jnp.zeros_like(o_ref)

    h = h_ref[...]
    n_sub = wd_ref.shape[0] // sub
    cols = [slice(c * sub, (c + 1) * sub) for c in range(n_sub)]

    def gate_value(g_ref, v_ref, c):
        return jnp.concatenate([g_ref[:, cols[c]], v_ref[:, cols[c]]], axis=1)

    def up_project(c):
        u = _dot(h, gate_value(wg_ref, wv_ref, c))
        for k in range(n_lane_tiles):
            lanes = slice(k * LANES, (k + 1) * LANES)
            u_ref[c % 2, k, hl:hl + tm] = u[:tm, lanes]
            u_ref[c % 2, k, :hl] = u[tm + hl:, lanes]
            u_ref[c % 2, k, hl + tm:] = u[tm:tm + hl, lanes]

    def window(c, offset):
        return jnp.concatenate([u_ref[c % 2, k, hl + offset:hl + offset + tm] for k in range(n_lane_tiles)],
                               axis=1)

    def gate_and_down(c):
        cw = gate_value(cwg_ref, cwv_ref, c)
        cu = (window(c, -1) * cw[0:1] + window(c, 0) * cw[1:2] + window(c, 1) * cw[2:3]
              + gate_value(cbg_ref, cbv_ref, c))
        act = _gelu_tanh_times(cu[:, :sub], cu[:, sub:]).astype(BF16)
        n = o_ref.shape[-1]
        nc = _col_chunk(n)
        for k in range(0, n, nc):
            o_ref[0, :, k:k + nc] += _dot(act, wd_ref[cols[c], k:k + nc])

    up_project(0)
    for c in range(n_sub):
        if c + 1 < n_sub:
            up_project(c + 1)
        gate_and_down(c)

    @pl.when(j == pl.num_programs(2) - 1)
    def _():
        o_ref[0] = x_ref[0] + _rms(o_ref[0], gpost_ref[...])


def _ffn(x, gpre, w_up, conv_w, conv_b, w_down, gpost, tm, fc):
    b, s, d = x.shape
    nf = D_FF // fc
    halo = SUBLANES
    tb = tm // halo
    last_blk = s // halo - 1
    return pl.pallas_call(
        _ffn_body,
        out_shape=jax.ShapeDtypeStruct((b, s, d), F32),
        grid=(b, s // tm, nf),
        in_specs=[
            pl.BlockSpec((1, tm, d), lambda bi, si, j: (bi, si, 0)),
            pl.BlockSpec((1, halo, d), lambda bi, si, j: (bi, jnp.maximum(si * tb - 1, 0), 0)),
            pl.BlockSpec((1, halo, d), lambda bi, si, j: (bi, jnp.minimum((si + 1) * tb, last_blk), 0)),
            _resident((1, d)),
            pl.BlockSpec((d, fc), lambda bi, si, j: (0, j)),
            pl.BlockSpec((d, fc), lambda bi, si, j: (0, nf + j)),
            pl.BlockSpec((CONV_WIDTH, fc), lambda bi, si, j: (0, j)),
            pl.BlockSpec((CONV_WIDTH, fc), lambda bi, si, j: (0, nf + j)),
            pl.BlockSpec((1, fc), lambda bi, si, j: (0, j)),
            pl.BlockSpec((1, fc), lambda bi, si, j: (0, nf + j)),
            pl.BlockSpec((fc, d), lambda bi, si, j: (j, 0)),
            _resident((1, d)),
        ],
        out_specs=pl.BlockSpec((1, tm, d), lambda bi, si, j: (bi, si, 0)),
        scratch_shapes=[pltpu.VMEM((tm + 2 * halo, d), BF16),
                        pltpu.VMEM((2, 2 * FFN_SUB // LANES, tm + 2 * halo, LANES), F32)],
        compiler_params=_cparams("parallel", "parallel", "arbitrary"),
        name="conv_ffn",
    )(x, x, x, gpre, w_up, w_up, conv_w, conv_w, conv_b, conv_b, w_down, gpost)


def _rotate_half_columns(w):
    k, n = w.shape
    g = w.reshape(k, n // MLA_ROPE, 2, MLA_ROPE // 2)
    return jnp.concatenate([-g[:, :, 1:], g[:, :, :1]], axis=2).reshape(k, n)


def _prepare_weights(p, seq):
    row = lambda v: v.reshape(1, -1).astype(F32)
    w_in = p["w_in"]
    offs, acc = [], 0
    for wdt in (GLA_QK, GLA_QK, GLA_WIDTH, GLA_WIDTH, GLA_GATE_RANK, GLA_GATE_RANK, MLA_RANK, MLA_RANK, MLA_ROPE):
        offs.append((acc, acc + wdt))
        acc += wdt
    part = [w_in[:, a:b] for a, b in offs]
    gq, gk, gv, gr, ggf, ggb, cq, ckv, kr = part
    d = w_in.shape[0]
    gate_pad = jnp.zeros((d, LANES - 2 * GLA_GATE_RANK), w_in.dtype)
    w_in_p = jnp.concatenate([gq, gk, gv, gr, cq, ckv, ggf, ggb, gate_pad, kr, _rotate_half_columns(kr)],
                             axis=1).astype(BF16)

    def gate_w2(w2, slot):
        full = jnp.zeros((LANES, GLA_QK), F32)
        return full.at[slot * GLA_GATE_RANK:(slot + 1) * GLA_GATE_RANK].set(w2).astype(BF16)

    wq = p["mla_w_q_up"].reshape(MLA_RANK, MLA_HEADS, MLA_QK)
    wq_nope = wq[:, :, :MLA_NOPE].reshape(MLA_RANK, MLA_HEADS * MLA_NOPE)
    wq_rope = wq[:, :, MLA_NOPE:].reshape(MLA_RANK, MLA_HEADS * MLA_ROPE)

    half = MLA_ROPE // 2
    freqs = ROPE_BASE ** (-jnp.arange(half, dtype=F32) / half)
    ang = jnp.arange(seq, dtype=F32)[:, None] * freqs[None, :]
    cos_t = jnp.tile(jnp.cos(ang), (1, 2 * MLA_HEADS))
    sin_t = jnp.tile(jnp.sin(ang), (1, 2 * MLA_HEADS))

    value_half = jnp.concatenate([jnp.ones((1, D_FF), F32), jnp.full((1, D_FF), 0.5, F32)], axis=1)

    return dict(
        g_mix_pre=row(p["norm_mix_pre"]), g_mix_post=row(p["norm_mix_post"]), w_in=w_in_p,
        w2f=gate_w2(p["gla_gate_w2_fwd"], 0), b2f=row(p["gla_gate_b_fwd"]),
        w2b=gate_w2(p["gla_gate_w2_bwd"], 1), b2b=row(p["gla_gate_b_bwd"]),
        gla_out_norm=row(p["gla_out_norm"]),
        mla_q_norm=row(p["mla_q_norm"]), mla_kv_norm=row(p["mla_kv_norm"]),
        wq_nope=wq_nope.astype(BF16), wq_rope=wq_rope.astype(BF16),
        wq_rot=_rotate_half_columns(wq_rope).astype(BF16), wkv=p["mla_w_kv_up"].astype(BF16),
        cos=cos_t, sin=sin_t,
        w_out_gla=p["w_out"][:GLA_WIDTH].astype(BF16), w_out_mla=p["w_out"][GLA_WIDTH:].astype(BF16),
        g_mem_pre=row(p["norm_mem_pre"]), g_mem_post=row(p["norm_mem_post"]), g_mem_kv=row(p["mem_kv_norm"]),
        w_mem_q=p["w_mem_q"].astype(BF16), w_mem_o=p["w_mem_o"].astype(BF16),
        w_mem_kv=jnp.concatenate([p["w_mem_k"], p["w_mem_v"]], axis=1).astype(BF16),
        g_ffn_pre=row(p["norm_ffn_pre"]), g_ffn_post=row(p["norm_ffn_post"]),
        w_ffn_up=p["w_ffn_up"].astype(BF16),
        ffn_conv_w=p["ffn_conv_w"].astype(F32) * value_half,
        ffn_conv_b=row(p["ffn_conv_b"]) * value_half,
        w_ffn_down=p["w_ffn_down"].astype(BF16),
    )


def _tiles(seq):
    tm = min(512, seq)
    return dict(tm=tm, gla_ts=tm, mla_ts=tm, mla_tq=min(1024, seq), mla_tk=tm, ffn_tm=tm, ffn_fc=512)


def _apply_layer(x, mem, w):
    b, s, d = x.shape
    t = _tiles(s)
    tm = t["tm"]
    x2 = x.reshape(b * s, d)
    proj = _norm_matmul(x2, w["g_mix_pre"], w["w_in"], tm).reshape(b, s, IN_WIDTH_PADDED)
    o_gla = _gla(proj, w["w2f"], w["b2f"], w["w2b"], w["b2b"], w["gla_out_norm"], t["gla_ts"])
    o_mla = _mla(proj, w["mla_q_norm"], w["mla_kv_norm"], w["wq_nope"], w["wq_rope"], w["wq_rot"], w["wkv"],
                 w["cos"], w["sin"], t["mla_ts"], t["mla_tq"], t["mla_tk"])
    x2 = _mix_out(o_gla.reshape(b * s, GLA_WIDTH), o_mla.reshape(b * s, MLA_WIDTH),
                  w["w_out_gla"], w["w_out_mla"], x2, w["g_mix_post"], tm)
    mtok = mem.shape[1]
    kv = _norm_matmul(mem.reshape(b * mtok, d), w["g_mem_kv"], w["w_mem_kv"], min(tm, b * mtok))
    x3 = _mem_attn(x2.reshape(b, s, d), w["g_mem_pre"], w["w_mem_q"], kv.reshape(b, mtok, 2 * d),
                   w["w_mem_o"], w["g_mem_post"], tm)
    return _ffn(x3, w["g_ffn_pre"], w["w_ffn_up"], w["ffn_conv_w"], w["ffn_conv_b"], w["w_ffn_down"],
                w["g_ffn_post"], t["ffn_tm"], t["ffn_fc"])


def kernel(x_prompt, x_sample, mem_prompt, mem_sample, norm_mix_pre, norm_mix_post, w_in, gla_gate_w2_fwd, gla_gate_b_fwd, gla_gate_w2_bwd, gla_gate_b_bwd, gla_out_norm, mla_q_norm, mla_w_q_up, mla_kv_norm, mla_w_kv_up, w_out, norm_mem_pre, norm_mem_post, mem_kv_norm, w_mem_q, w_mem_k, w_mem_v, w_mem_o, norm_ffn_pre, norm_ffn_post, w_ffn_up, ffn_conv_w, ffn_conv_b, w_ffn_down):
    params = dict(
        norm_mix_pre=norm_mix_pre, norm_mix_post=norm_mix_post, w_in=w_in,
        gla_gate_w2_fwd=gla_gate_w2_fwd, gla_gate_b_fwd=gla_gate_b_fwd,
        gla_gate_w2_bwd=gla_gate_w2_bwd, gla_gate_b_bwd=gla_gate_b_bwd, gla_out_norm=gla_out_norm,
        mla_q_norm=mla_q_norm, mla_w_q_up=mla_w_q_up, mla_kv_norm=mla_kv_norm, mla_w_kv_up=mla_w_kv_up,
        w_out=w_out, norm_mem_pre=norm_mem_pre, norm_mem_post=norm_mem_post, mem_kv_norm=mem_kv_norm,
        w_mem_q=w_mem_q, w_mem_k=w_mem_k, w_mem_v=w_mem_v, w_mem_o=w_mem_o,
        norm_ffn_pre=norm_ffn_pre, norm_ffn_post=norm_ffn_post, w_ffn_up=w_ffn_up,
        ffn_conv_w=ffn_conv_w, ffn_conv_b=ffn_conv_b, w_ffn_down=w_ffn_down,
    )
    depth = w_in.shape[0]
    yp, ys = x_prompt, x_sample
    for layer in range(depth):
        p = {k: v[layer] for k, v in params.items()}
        assert yp.shape[1] == ys.shape[1]
        w = _prepare_weights(p, yp.shape[1])
        yp = _apply_layer(yp, mem_prompt, w)
        ys = _apply_layer(ys, mem_sample, w)
    return (yp, ys)
```

```python
import functools
import math

import jax
import jax.numpy as jnp
from jax import lax
from jax.experimental import pallas as pl
from jax.experimental.pallas import tpu as pltpu

F32 = jnp.float32
BF16 = jnp.bfloat16

EPS = 1e-6
D_MODEL = 2048
GLA_HEADS = 4
GLA_DK = 128
GLA_DV = 256
GLA_GATE_RANK = 16
GLA_TAU = 16.0
GLA_CHUNK = 64
GLA_SUB = 256
GLA_QK = GLA_HEADS * GLA_DK
GLA_WIDTH = GLA_HEADS * GLA_DV
MLA_HEADS = 8
MLA_RANK = 512
MLA_NOPE = 128
MLA_ROPE = 64
MLA_V = 128
MLA_QK = MLA_NOPE + MLA_ROPE
MLA_WIDTH = MLA_HEADS * MLA_V
ROPE_BASE = 10000.0
MEM_HEADS = 4
MEM_HEAD_DIM = D_MODEL // MEM_HEADS
D_FF = 5632
CONV_WIDTH = 3
FFN_SUB = 256

LANES = 128
SUBLANES = 8
BF16_ROWS = 16
VMEM_LIMIT_BYTES = 56 * 1024 * 1024
MLA_VT_ROWS = MLA_V + BF16_ROWS
LOG2E = 1.4426950408889634

IN_GQ = 0
IN_GK = IN_GQ + GLA_QK
IN_GV = IN_GK + GLA_QK
IN_GR = IN_GV + GLA_WIDTH
IN_CQ = IN_GR + GLA_WIDTH
IN_CKV = IN_CQ + MLA_RANK
IN_GATE = IN_CKV + MLA_RANK
IN_KR = IN_GATE + LANES
IN_WIDTH_PADDED = IN_KR + LANES


def _cparams(*semantics):
    return pltpu.CompilerParams(dimension_semantics=semantics, vmem_limit_bytes=VMEM_LIMIT_BYTES)


def _resident(shape):
    zeros = (0,) * len(shape)
    return pl.BlockSpec(shape, lambda *_: zeros, pipeline_mode=pl.Buffered(1))


def _rms(x, g):
    return x * lax.rsqrt(jnp.mean(x * x, axis=-1, keepdims=True) + EPS) * g


def _dot(a, b):
    return jnp.dot(a, b, preferred_element_type=F32)


def _dot_nt(a, b):
    return lax.dot_general(a, b, (((1,), (1,)), ((), ())), preferred_element_type=F32)


def _dot_tn(a, b):
    return lax.dot_general(a, b, (((0,), (0,)), ((), ())), preferred_element_type=F32)


def _col_chunk(n):
    for c in (512, 256, LANES):
        if n % c == 0:
            return c
    raise ValueError(f"width {n} is not a multiple of {LANES}")


def _norm_matmul_body(x_ref, g_ref, w_ref, o_ref):
    h = _rms(x_ref[...], g_ref[...]).astype(BF16)
    n = o_ref.shape[-1]
    nc = _col_chunk(n)
    for c in range(0, n, nc):
        o_ref[:, c:c + nc] = _dot(h, w_ref[:, c:c + nc]).astype(o_ref.dtype)


def _norm_matmul(x, g, w, tm):
    t, d = x.shape
    n = w.shape[1]
    return pl.pallas_call(
        _norm_matmul_body,
        out_shape=jax.ShapeDtypeStruct((t, n), BF16),
        grid=(t // tm,),
        in_specs=[pl.BlockSpec((tm, d), lambda i: (i, 0)), _resident((1, d)), _resident((d, n))],
        out_specs=pl.BlockSpec((tm, n), lambda i: (i, 0)),
        compiler_params=_cparams("parallel"),
        name="norm_matmul",
    )(x, g, w)


def _log_sigmoid(x):
    return jnp.minimum(x, 0.0) - jnp.log(1.0 + jnp.exp(-jnp.abs(x)))


def _gla_tables(reverse):
    shift = GLA_CHUNK.bit_length() - 1
    row = lax.broadcasted_iota(jnp.int32, (GLA_SUB, GLA_SUB), 0)
    col = lax.broadcasted_iota(jnp.int32, (GLA_SUB, GLA_SUB), 1)
    same_chunk = lax.shift_right_logical(row, shift) == lax.shift_right_logical(col, shift)
    in_chunk = lambda cond: jnp.where(same_chunk, jnp.where(cond, 1.0, 0.0), 0.0)
    if reverse:
        cum = in_chunk(col >= row)
        keep = in_chunk(col > row) > 0.5
    else:
        cum = in_chunk(col <= row)
        keep = cum > 0.5
    return cum.astype(BF16), keep


def _gla_log_decay(gate_ref, w2_ref, b2_ref, la_ref):
    logits = _dot(gate_ref[0], w2_ref[...]) + b2_ref[...]
    la_ref[...] = _log_sigmoid(logits) * (1.0 / GLA_TAU)


def _gla_block(q_ref, k_ref, v_ref, la_ref, state_ref, reverse):
    ts = q_ref.shape[1]
    c = GLA_CHUNK
    n_chunks = ts // c
    cum_mat, keep = _gla_tables(reverse)
    la = la_ref[...]
    la_hi = la.astype(BF16)
    la_lo = (la - la_hi.astype(F32)).astype(BF16)
    cum = jnp.concatenate(
        [_dot(cum_mat, la_hi[r:r + GLA_SUB]) + _dot(cum_mat, la_lo[r:r + GLA_SUB])
         for r in range(0, ts, GLA_SUB)], axis=0).reshape(n_chunks, c, GLA_QK)
    mid_row, last_row = (c // 2 - 1, 0) if reverse else (c // 2, c - 1)
    mid = cum[:, mid_row:mid_row + 1, :]
    last = cum[:, last_row:last_row + 1, :]
    q = q_ref[0].astype(F32).reshape(n_chunks, c, GLA_QK) * (GLA_DK ** -0.5)
    k = k_ref[0].astype(F32).reshape(n_chunks, c, GLA_QK)
    qi_f = q * jnp.exp(cum - mid)
    ki_f = k * jnp.exp(mid - cum)
    flat = lambda t: t.reshape(ts, GLA_QK).astype(BF16)
    qi = flat(qi_f)
    ki = flat(ki_f)
    qd = flat(qi_f * jnp.exp(mid))
    kd = flat(ki_f * jnp.exp(last - mid))
    chunk_decay = jnp.exp(last)
    order = range(n_chunks - 1, -1, -1) if reverse else range(n_chunks)
    outs = []
    for h in range(GLA_HEADS):
        kc = slice(h * GLA_DK, (h + 1) * GLA_DK)
        v = v_ref[0, :, h * GLA_DV:(h + 1) * GLA_DV]
        intra = []
        for r in range(0, ts, GLA_SUB):
            rows = slice(r, r + GLA_SUB)
            att = jnp.where(keep, _dot_nt(qi[rows, kc], ki[rows, kc]), 0.0).astype(BF16)
            intra.append(_dot(att, v[rows]))
        state = state_ref[h]
        inter = [None] * n_chunks
        for n in order:
            rows = slice(n * c, (n + 1) * c)
            inter[n] = _dot_nt(qd[rows, kc], state.astype(BF16))
            state = state * chunk_decay[n, :, kc] + _dot_tn(v[rows], kd[rows, kc])
        state_ref[h] = state
        outs.append(jnp.concatenate(intra, axis=0) + jnp.concatenate(inter, axis=0))
    return outs


def _gla_fwd_body(q_ref, k_ref, v_ref, gate_ref, w2_ref, b2_ref, o_ref, state_ref, la_ref):
    @pl.when(pl.program_id(1) == 0)
    def _():
        state_ref[...] = jnp.zeros_like(state_ref)

    _gla_log_decay(gate_ref, w2_ref, b2_ref, la_ref)
    outs = _gla_block(q_ref, k_ref, v_ref, la_ref, state_ref, False)
    for h in range(GLA_HEADS):
        o_ref[0, :, h * GLA_DV:(h + 1) * GLA_DV] = outs[h]


def _gla_bwd_body(q_ref, k_ref, v_ref, gate_ref, w2_ref, b2_ref, of_ref, r_ref, gn_ref,
                  o_ref, state_ref, la_ref):
    @pl.when(pl.program_id(1) == 0)
    def _():
        state_ref[...] = jnp.zeros_like(state_ref)

    _gla_log_decay(gate_ref, w2_ref, b2_ref, la_ref)
    outs = _gla_block(q_ref, k_ref, v_ref, la_ref, state_ref, True)
    gn = gn_ref[...]
    for h in range(GLA_HEADS):
        vcols = slice(h * GLA_DV, (h + 1) * GLA_DV)
        o = outs[h] + of_ref[0, :, vcols]
        half_r = 0.5 * r_ref[0, :, vcols].astype(F32)
        silu = half_r + half_r * jnp.tanh(half_r)
        o_ref[0, :, vcols] = (_rms(o, gn) * silu).astype(o_ref.dtype)


def _gla(proj, w2f, b2f, w2b, b2b, out_norm, ts):
    b, s, _ = proj.shape
    ns = s // ts

    def col_spec(width, offset, rev):
        blk = offset // width
        if rev:
            return pl.BlockSpec((1, ts, width), lambda bi, si: (bi, ns - 1 - si, blk))
        return pl.BlockSpec((1, ts, width), lambda bi, si: (bi, si, blk))

    def common_specs(rev):
        return [col_spec(GLA_QK, IN_GQ, rev), col_spec(GLA_QK, IN_GK, rev), col_spec(GLA_WIDTH, IN_GV, rev),
                col_spec(LANES, IN_GATE, rev), _resident((LANES, GLA_QK)), _resident((1, GLA_QK))]

    scratch = [pltpu.VMEM((GLA_HEADS, GLA_DV, GLA_DK), F32), pltpu.VMEM((ts, GLA_QK), F32)]
    o_fwd = pl.pallas_call(
        _gla_fwd_body,
        out_shape=jax.ShapeDtypeStruct((b, s, GLA_WIDTH), F32),
        grid=(b, ns),
        in_specs=common_specs(False),
        out_specs=pl.BlockSpec((1, ts, GLA_WIDTH), lambda bi, si: (bi, si, 0)),
        scratch_shapes=scratch,
        compiler_params=_cparams("parallel", "arbitrary"),
        name="gla_fwd",
    )(proj, proj, proj, proj, w2f, b2f)
    return pl.pallas_call(
        _gla_bwd_body,
        out_shape=jax.ShapeDtypeStruct((b, s, GLA_WIDTH), BF16),
        grid=(b, ns),
        in_specs=common_specs(True) + [
            pl.BlockSpec((1, ts, GLA_WIDTH), lambda bi, si: (bi, ns - 1 - si, 0)),
            col_spec(GLA_WIDTH, IN_GR, True),
            _resident((1, GLA_DV)),
        ],
        out_specs=pl.BlockSpec((1, ts, GLA_WIDTH), lambda bi, si: (bi, ns - 1 - si, 0)),
        scratch_shapes=scratch,
        compiler_params=_cparams("parallel", "arbitrary"),
        name="gla_bwd",
    )(proj, proj, proj, proj, w2b, b2b, o_fwd, proj, out_norm)


def _mla_prep_body(cq_ref, ckv_ref, kr_ref, qn_ref, kvn_ref, wq_nope_ref, wq_rope_ref, wq_rot_ref, wkv_ref,
                   cos_ref, sin_ref, q_out, k_out, vt_out):
    hq = _rms(cq_ref[0].astype(F32), qn_ref[...]).astype(BF16)
    hkv = _rms(ckv_ref[0].astype(F32), kvn_ref[...]).astype(BF16)
    lane_tiles = MLA_HEADS * MLA_ROPE // LANES
    cos = jnp.concatenate([cos_ref[...]] * lane_tiles, axis=1)
    sin = jnp.concatenate([sin_ref[...]] * lane_tiles, axis=1)
    q_scale = MLA_QK ** -0.5 * LOG2E
    q_nope = _dot(hq, wq_nope_ref[...]) * q_scale
    q_pe = (_dot(hq, wq_rope_ref[...]) * cos + _dot(hq, wq_rot_ref[...]) * sin) * q_scale
    kv = _dot(hkv, wkv_ref[...])
    kr = kr_ref[0].astype(F32)
    k_pe = (kr[:, :MLA_ROPE] * cos[:, :MLA_ROPE] + kr[:, MLA_ROPE:] * sin[:, :MLA_ROPE]).astype(BF16)
    pad_rows = (MLA_VT_ROWS - MLA_V, cq_ref.shape[1])
    ones_row = jnp.where(lax.broadcasted_iota(jnp.int32, pad_rows, 0) == 0, 1.0, 0.0).astype(BF16)
    for h in range(MLA_HEADS):
        q_out[0, h, :, :MLA_NOPE] = q_nope[:, h * MLA_NOPE:(h + 1) * MLA_NOPE].astype(BF16)
        q_out[0, h, :, MLA_NOPE:] = q_pe[:, h * MLA_ROPE:(h + 1) * MLA_ROPE].astype(BF16)
        base = h * (MLA_NOPE + MLA_V)
        k_out[0, h, :, :MLA_NOPE] = kv[:, base:base + MLA_NOPE].astype(BF16)
        k_out[0, h, :, MLA_NOPE:] = k_pe
        vt_out[0, h, :MLA_V] = kv[:, base + MLA_NOPE:base + MLA_NOPE + MLA_V].T.astype(BF16)
        vt_out[0, h, MLA_V:] = ones_row


def _reduce_rows(op, x, groups=8):
    rows, n = x.shape
    partial = op(x.reshape(groups, rows // groups, n), axis=0)
    return op(partial, axis=0, keepdims=True)


def _mla_attn_body(q_ref, k_ref, vt_ref, o_ref, s_ref, *, tk):
    q = q_ref[0, 0]
    n_chunks = k_ref.shape[2] // tk

    n_slots = s_ref.shape[0]
    chunk_max = [None] * n_chunks

    def scores(c):
        s = _dot_nt(k_ref[0, 0, c * tk:(c + 1) * tk, :], q)
        s_ref[c % n_slots] = s
        chunk_max[c] = _reduce_rows(jnp.max, s)

    m = acc = None
    for c in range(min(n_slots - 1, n_chunks)):
        scores(c)
    for c in range(n_chunks):
        if c + n_slots - 1 < n_chunks:
            scores(c + n_slots - 1)
        m_new = chunk_max[c] if m is None else jnp.maximum(m, chunk_max[c])
        p = jnp.exp2(s_ref[c % n_slots] - m_new).astype(BF16)
        pv = _dot(vt_ref[0, 0, :, c * tk:(c + 1) * tk], p)
        acc = pv if acc is None else jnp.exp2(m - m_new) * acc + pv
        m = m_new
    o_ref[0] = (acc[:MLA_V] / acc[MLA_V:MLA_V + 1]).T.astype(o_ref.dtype)


def _mla(proj, q_norm, kv_norm, wq_nope, wq_rope, wq_rot, wkv, cos_t, sin_t, ts, tq, tk):
    b, s, _ = proj.shape
    rope_w = MLA_HEADS * MLA_ROPE
    q, k, vt = pl.pallas_call(
        _mla_prep_body,
        out_shape=(jax.ShapeDtypeStruct((b, MLA_HEADS, s, MLA_QK), BF16),
                   jax.ShapeDtypeStruct((b, MLA_HEADS, s, MLA_QK), BF16),
                   jax.ShapeDtypeStruct((b, MLA_HEADS, MLA_VT_ROWS, s), BF16)),
        grid=(b, s // ts),
        in_specs=[
            pl.BlockSpec((1, ts, MLA_RANK), lambda bi, si: (bi, si, IN_CQ // MLA_RANK)),
            pl.BlockSpec((1, ts, MLA_RANK), lambda bi, si: (bi, si, IN_CKV // MLA_RANK)),
            pl.BlockSpec((1, ts, LANES), lambda bi, si: (bi, si, IN_KR // LANES)),
            _resident((1, MLA_RANK)), _resident((1, MLA_RANK)),
            _resident((MLA_RANK, MLA_HEADS * MLA_NOPE)), _resident((MLA_RANK, rope_w)),
            _resident((MLA_RANK, rope_w)), _resident((MLA_RANK, MLA_HEADS * (MLA_NOPE + MLA_V))),
            pl.BlockSpec((ts, LANES), lambda bi, si: (si, 0)),
            pl.BlockSpec((ts, LANES), lambda bi, si: (si, 0)),
        ],
        out_specs=(pl.BlockSpec((1, MLA_HEADS, ts, MLA_QK), lambda bi, si: (bi, 0, si, 0)),
                   pl.BlockSpec((1, MLA_HEADS, ts, MLA_QK), lambda bi, si: (bi, 0, si, 0)),
                   pl.BlockSpec((1, MLA_HEADS, MLA_VT_ROWS, ts), lambda bi, si: (bi, 0, 0, si))),
        compiler_params=_cparams("parallel", "parallel"),
        name="mla_prep",
    )(proj, proj, proj, q_norm, kv_norm, wq_nope, wq_rope, wq_rot, wkv, cos_t, sin_t)
    return pl.pallas_call(
        functools.partial(_mla_attn_body, tk=tk),
        out_shape=jax.ShapeDtypeStruct((b, s, MLA_WIDTH), BF16),
        grid=(b, MLA_HEADS, s // tq),
        in_specs=[
            pl.BlockSpec((1, 1, tq, MLA_QK), lambda bi, hi, qi: (bi, hi, qi, 0)),
            pl.BlockSpec((1, 1, s, MLA_QK), lambda bi, hi, qi: (bi, hi, 0, 0)),
            pl.BlockSpec((1, 1, MLA_VT_ROWS, s), lambda bi, hi, qi: (bi, hi, 0, 0)),
        ],
        out_specs=pl.BlockSpec((1, tq, MLA_V), lambda bi, hi, qi: (bi, qi, hi)),
        scratch_shapes=[pltpu.VMEM((3, tk, tq), F32)],
        compiler_params=_cparams("parallel", "parallel", "parallel"),
        name="mla_attn",
    )(q, k, vt)


def _residual_norm_store(x, o_ref, sumsq, g):
    scale = lax.rsqrt(sumsq * (1.0 / o_ref.shape[-1]) + EPS)
    o_ref[...] = x + o_ref[...] * scale * g


def _project_rows(o_ref, lhs_and_weights):
    n = o_ref.shape[-1]
    nc = _col_chunk(n)
    sumsq = jnp.zeros((o_ref.shape[0], 1), F32)
    for c in range(0, n, nc):
        y = sum(_dot(a, w_ref[:, c:c + nc]) for a, w_ref in lhs_and_weights)
        sumsq = sumsq + jnp.sum(y * y, axis=-1, keepdims=True)
        o_ref[:, c:c + nc] = y
    return sumsq


def _mix_out_body(a1_ref, a2_ref, w1_ref, w2_ref, x_ref, g_ref, o_ref):
    sumsq = _project_rows(o_ref, [(a1_ref[...], w1_ref), (a2_ref[...], w2_ref)])
    _residual_norm_store(x_ref[...], o_ref, sumsq, g_ref[...])


def _mix_out(o_gla, o_mla, w1, w2, x, g, tm):
    t, d = x.shape
    return pl.pallas_call(
        _mix_out_body,
        out_shape=jax.ShapeDtypeStruct((t, d), F32),
        grid=(t // tm,),
        in_specs=[
            pl.BlockSpec((tm, GLA_WIDTH), lambda i: (i, 0)),
            pl.BlockSpec((tm, MLA_WIDTH), lambda i: (i, 0)),
            _resident((GLA_WIDTH, d)), _resident((MLA_WIDTH, d)),
            pl.BlockSpec((tm, d), lambda i: (i, 0)),
            _resident((1, d)),
        ],
        out_specs=pl.BlockSpec((tm, d), lambda i: (i, 0)),
        compiler_params=_cparams("parallel"),
        name="mix_out",
    )(o_gla, o_mla, w1, w2, x, g)


def _mem_attn_body(x_ref, gpre_ref, wq_ref, kv_ref, wo_ref, gpost_ref, o_ref, a_ref, q_ref):
    q_scale = MEM_HEAD_DIM ** -0.5 * LOG2E
    x = x_ref[0]
    h = _rms(x, gpre_ref[...]).astype(BF16)
    head_cols = [slice(hd * MEM_HEAD_DIM, (hd + 1) * MEM_HEAD_DIM) for hd in range(MEM_HEADS)]

    def project_q(hd):
        q_ref[hd % 2] = (_dot(h, wq_ref[:, head_cols[hd]]) * q_scale).astype(BF16)

    project_q(0)
    for hd in range(MEM_HEADS):
        if hd + 1 < MEM_HEADS:
            project_q(hd + 1)
        cols = head_cols[hd]
        vcols = slice(D_MODEL + hd * MEM_HEAD_DIM, D_MODEL + (hd + 1) * MEM_HEAD_DIM)
        s = _dot_nt(q_ref[hd % 2], kv_ref[0, :, cols])
        m = jnp.max(s, axis=-1, keepdims=True)
        p = jnp.exp2(s - m)
        l = jnp.sum(p, axis=-1, keepdims=True)
        a_ref[:, cols] = (_dot(p.astype(BF16), kv_ref[0, :, vcols]) / l).astype(BF16)
    sumsq = _project_rows(o_ref.at[0], [(a_ref[...], wo_ref)])
    _residual_norm_store(x, o_ref.at[0], sumsq, gpost_ref[...])


def _mem_attn(x, gpre, wq, kv, wo, gpost, tm):
    b, s, d = x.shape
    m = kv.shape[1]
    return pl.pallas_call(
        _mem_attn_body,
        out_shape=jax.ShapeDtypeStruct((b, s, d), F32),
        grid=(b, s // tm),
        in_specs=[
            pl.BlockSpec((1, tm, d), lambda bi, si: (bi, si, 0)),
            _resident((1, d)), _resident((d, d)),
            pl.BlockSpec((1, m, 2 * d), lambda bi, si: (bi, 0, 0)),
            _resident((d, d)), _resident((1, d)),
        ],
        out_specs=pl.BlockSpec((1, tm, d), lambda bi, si: (bi, si, 0)),
        scratch_shapes=[pltpu.VMEM((tm, d), BF16), pltpu.VMEM((2, tm, MEM_HEAD_DIM), BF16)],
        compiler_params=_cparams("parallel", "parallel"),
        name="mem_attn",
    )(x, gpre, wq, kv, wo, gpost)


def _gelu_tanh_times(x, v_half):
    c = math.sqrt(2.0 / math.pi)
    inner = x * (c + (c * 0.044715) * (x * x))
    return x * (1.0 + jnp.tanh(inner)) * v_half


def _ffn_body(x_ref, xprev_ref, xnext_ref, gpre_ref, wg_ref, wv_ref, cwg_ref, cwv_ref, cbg_ref, cbv_ref,
              wd_ref, gpost_ref, o_ref, h_ref, u_ref):
    si = pl.program_id(1)
    j = pl.program_id(2)
    tm = x_ref.shape[1]
    sub = FFN_SUB
    hl = SUBLANES
    n_lane_tiles = 2 * sub // LANES

    @pl.when(j == 0)
    def _():
        g = gpre_ref[...]
        h_ref[:tm] = _rms(x_ref[0], g).astype(BF16)
        nxt = jnp.where(si == pl.num_programs(1) - 1, 0.0, _rms(xnext_ref[0], g))
        prv = jnp.where(si == 0, 0.0, _rms(xprev_ref[0], g))
        h_ref[tm:] = jnp.concatenate([nxt, prv], axis=0).astype(BF16)
        o_ref[...] = jnp.zeros_like(o_ref)

    h = h_ref[...]
    n_sub = wd_ref.shape[0] // sub
    cols = [slice(c * sub, (c + 1) * sub) for c in range(n_sub)]

    def gate_value(g_ref, v_ref, c):
        return jnp.concatenate([g_ref[:, cols[c]], v_ref[:, cols[c]]], axis=1)

    def up_project(c):
        u = _dot(h, gate_value(wg_ref, wv_ref, c))
        for k in range(n_lane_tiles):
            lanes = slice(k * LANES, (k + 1) * LANES)
            u_ref[c % 2, k, hl:hl + tm] = u[:tm, lanes]
            u_ref[c % 2, k, :hl] = u[tm + hl:, lanes]
            u_ref[c % 2, k, hl + tm:] = u[tm:tm + hl, lanes]

    def window(c, offset):
        return jnp.concatenate([u_ref[c % 2, k, hl + offset:hl + offset + tm] for k in range(n_lane_tiles)],
                               axis=1)

    def gate_and_down(c):
        cw = gate_value(cwg_ref, cwv_ref, c)
        cu = (window(c, -1) * cw[0:1] + window(c, 0) * cw[1:2] + window(c, 1) * cw[2:3]
              + gate_value(cbg_ref, cbv_ref, c))
        act = _gelu_tanh_times(cu[:, :sub], cu[:, sub:]).astype(BF16)
        n = o_ref.shape[-1]
        nc = _col_chunk(n)
        for k in range(0, n, nc):
            o_ref[0, :, k:k + nc] += _dot(act, wd_ref[cols[c], k:k + nc])

    up_project(0)
    for c in range(n_sub):
        if c + 1 < n_sub:
            up_project(c + 1)
        gate_and_down(c)

    @pl.when(j == pl.num_programs(2) - 1)
    def _():
        o_ref[0] = x_ref[0] + _rms(o_ref[0], gpost_ref[...])


def _ffn(x, gpre, w_up, conv_w, conv_b, w_down, gpost, tm, fc):
    b, s, d = x.shape
    nf = D_FF // fc
    halo = SUBLANES
    tb = tm // halo
    last_blk = s // halo - 1
    return pl.pallas_call(
        _ffn_body,
        out_shape=jax.ShapeDtypeStruct((b, s, d), F32),
        grid=(b, s // tm, nf),
        in_specs=[
            pl.BlockSpec((1, tm, d), lambda bi, si, j: (bi, si, 0)),
            pl.BlockSpec((1, halo, d), lambda bi, si, j: (bi, jnp.maximum(si * tb - 1, 0), 0)),
            pl.BlockSpec((1, halo, d), lambda bi, si, j: (bi, jnp.minimum((si + 1) * tb, last_blk), 0)),
            _resident((1, d)),
            pl.BlockSpec((d, fc), lambda bi, si, j: (0, j)),
            pl.BlockSpec((d, fc), lambda bi, si, j: (0, nf + j)),
            pl.BlockSpec((CONV_WIDTH, fc), lambda bi, si, j: (0, j)),
            pl.BlockSpec((CONV_WIDTH, fc), lambda bi, si, j: (0, nf + j)),
            pl.BlockSpec((1, fc), lambda bi, si, j: (0, j)),
            pl.BlockSpec((1, fc), lambda bi, si, j: (0, nf + j)),
            pl.BlockSpec((fc, d), lambda bi, si, j: (j, 0)),
            _resident((1, d)),
        ],
        out_specs=pl.BlockSpec((1, tm, d), lambda bi, si, j: (bi, si, 0)),
        scratch_shapes=[pltpu.VMEM((tm + 2 * halo, d), BF16),
                        pltpu.VMEM((2, 2 * FFN_SUB // LANES, tm + 2 * halo, LANES), F32)],
        compiler_params=_cparams("parallel", "parallel", "arbitrary"),
        name="conv_ffn",
    )(x, x, x, gpre, w_up, w_up, conv_w, conv_w, conv_b, conv_b, w_down, gpost)


def _rotate_half_columns(w):
    k, n = w.shape
    g = w.reshape(k, n // MLA_ROPE, 2, MLA_ROPE // 2)
    return jnp.concatenate([-g[:, :, 1:], g[:, :, :1]], axis=2).reshape(k, n)


def _prepare_weights(p, seq):
    row = lambda v: v.reshape(1, -1).astype(F32)
    w_in = p["w_in"]
    offs, acc = [], 0
    for wdt in (GLA_QK, GLA_QK, GLA_WIDTH, GLA_WIDTH, GLA_GATE_RANK, GLA_GATE_RANK, MLA_RANK, MLA_RANK, MLA_ROPE):
        offs.append((acc, acc + wdt))
        acc += wdt
    part = [w_in[:, a:b] for a, b in offs]
    gq, gk, gv, gr, ggf, ggb, cq, ckv, kr = part
    d = w_in.shape[0]
    gate_pad = jnp.zeros((d, LANES - 2 * GLA_GATE_RANK), w_in.dtype)
    w_in_p = jnp.concatenate([gq, gk, gv, gr, cq, ckv, ggf, ggb, gate_pad, kr, _rotate_half_columns(kr)],
                             axis=1).astype(BF16)

    def gate_w2(w2, slot):
        full = jnp.zeros((LANES, GLA_QK), F32)
        return full.at[slot * GLA_GATE_RANK:(slot + 1) * GLA_GATE_RANK].set(w2).astype(BF16)

    wq = p["mla_w_q_up"].reshape(MLA_RANK, MLA_HEADS, MLA_QK)
    wq_nope = wq[:, :, :MLA_NOPE].reshape(MLA_RANK, MLA_HEADS * MLA_NOPE)
    wq_rope = wq[:, :, MLA_NOPE:].reshape(MLA_RANK, MLA_HEADS * MLA_ROPE)

    half = MLA_ROPE // 2
    freqs = ROPE_BASE ** (-jnp.arange(half, dtype=F32) / half)
    ang = jnp.arange(seq, dtype=F32)[:, None] * freqs[None, :]
    cos_t = jnp.tile(jnp.cos(ang), (1, LANES // half))
    sin_t = jnp.tile(jnp.sin(ang), (1, LANES // half))

    value_half = jnp.concatenate([jnp.ones((1, D_FF), F32), jnp.full((1, D_FF), 0.5, F32)], axis=1)

    return dict(
        g_mix_pre=row(p["norm_mix_pre"]), g_mix_post=row(p["norm_mix_post"]), w_in=w_in_p,
        w2f=gate_w2(p["gla_gate_w2_fwd"], 0), b2f=row(p["gla_gate_b_fwd"]),
        w2b=gate_w2(p["gla_gate_w2_bwd"], 1), b2b=row(p["gla_gate_b_bwd"]),
        gla_out_norm=row(p["gla_out_norm"]),
        mla_q_norm=row(p["mla_q_norm"]), mla_kv_norm=row(p["mla_kv_norm"]),
        wq_nope=wq_nope.astype(BF16), wq_rope=wq_rope.astype(BF16),
        wq_rot=_rotate_half_columns(wq_rope).astype(BF16), wkv=p["mla_w_kv_up"].astype(BF16),
        cos=cos_t, sin=sin_t,
        w_out_gla=p["w_out"][:GLA_WIDTH].astype(BF16), w_out_mla=p["w_out"][GLA_WIDTH:].astype(BF16),
        g_mem_pre=row(p["norm_mem_pre"]), g_mem_post=row(p["norm_mem_post"]), g_mem_kv=row(p["mem_kv_norm"]),
        w_mem_q=p["w_mem_q"].astype(BF16), w_mem_o=p["w_mem_o"].astype(BF16),
        w_mem_kv=jnp.concatenate([p["w_mem_k"], p["w_mem_v"]], axis=1).astype(BF16),
        g_ffn_pre=row(p["norm_ffn_pre"]), g_ffn_post=row(p["norm_ffn_post"]),
        w_ffn_up=p["w_ffn_up"].astype(BF16),
        ffn_conv_w=p["ffn_conv_w"].astype(F32) * value_half,
        ffn_conv_b=row(p["ffn_conv_b"]) * value_half,
        w_ffn_down=p["w_ffn_down"].astype(BF16),
    )


def _tiles(seq):
    tm = min(512, seq)
    return dict(tm=tm, gla_ts=tm, mla_ts=tm, mla_tq=min(1024, seq), mla_tk=tm, ffn_tm=tm, ffn_fc=512)


def _apply_layer(x, mem, w):
    b, s, d = x.shape
    t = _tiles(s)
    tm = t["tm"]
    x2 = x.reshape(b * s, d)
    proj = _norm_matmul(x2, w["g_mix_pre"], w["w_in"], tm).reshape(b, s, IN_WIDTH_PADDED)
    o_gla = _gla(proj, w["w2f"], w["b2f"], w["w2b"], w["b2b"], w["gla_out_norm"], t["gla_ts"])
    o_mla = _mla(proj, w["mla_q_norm"], w["mla_kv_norm"], w["wq_nope"], w["wq_rope"], w["wq_rot"], w["wkv"],
                 w["cos"], w["sin"], t["mla_ts"], t["mla_tq"], t["mla_tk"])
    x2 = _mix_out(o_gla.reshape(b * s, GLA_WIDTH), o_mla.reshape(b * s, MLA_WIDTH),
                  w["w_out_gla"], w["w_out_mla"], x2, w["g_mix_post"], tm)
    mtok = mem.shape[1]
    kv = _norm_matmul(mem.reshape(b * mtok, d), w["g_mem_kv"], w["w_mem_kv"], min(tm, b * mtok))
    x3 = _mem_attn(x2.reshape(b, s, d), w["g_mem_pre"], w["w_mem_q"], kv.reshape(b, mtok, 2 * d),
                   w["w_mem_o"], w["g_mem_post"], tm)
    return _ffn(x3, w["g_ffn_pre"], w["w_ffn_up"], w["ffn_conv_w"], w["ffn_conv_b"], w["w_ffn_down"],
                w["g_ffn_post"], t["ffn_tm"], t["ffn_fc"])


def kernel(x_prompt, x_sample, mem_prompt, mem_sample, norm_mix_pre, norm_mix_post, w_in, gla_gate_w2_fwd, gla_gate_b_fwd, gla_gate_w2_bwd, gla_gate_b_bwd, gla_out_norm, mla_q_norm, mla_w_q_up, mla_kv_norm, mla_w_kv_up, w_out, norm_mem_pre, norm_mem_post, mem_kv_norm, w_mem_q, w_mem_k, w_mem_v, w_mem_o, norm_ffn_pre, norm_ffn_post, w_ffn_up, ffn_conv_w, ffn_conv_b, w_ffn_down):
    params = dict(
        norm_mix_pre=norm_mix_pre, norm_mix_post=norm_mix_post, w_in=w_in,
        gla_gate_w2_fwd=gla_gate_w2_fwd, gla_gate_b_fwd=gla_gate_b_fwd,
        gla_gate_w2_bwd=gla_gate_w2_bwd, gla_gate_b_bwd=gla_gate_b_bwd, gla_out_norm=gla_out_norm,
        mla_q_norm=mla_q_norm, mla_w_q_up=mla_w_q_up, mla_kv_norm=mla_kv_norm, mla_w_kv_up=mla_w_kv_up,
        w_out=w_out, norm_mem_pre=norm_mem_pre, norm_mem_post=norm_mem_post, mem_kv_norm=mem_kv_norm,
        w_mem_q=w_mem_q, w_mem_k=w_mem_k, w_mem_v=w_mem_v, w_mem_o=w_mem_o,
        norm_ffn_pre=norm_ffn_pre, norm_ffn_post=norm_ffn_post, w_ffn_up=w_ffn_up,
        ffn_conv_w=ffn_conv_w, ffn_conv_b=ffn_conv_b, w_ffn_down=w_ffn_down,
    )
    depth = w_in.shape[0]
    yp, ys = x_prompt, x_sample
    for layer in range(depth):
        p = {k: v[layer] for k, v in params.items()}
        assert yp.shape[1] == ys.shape[1]
        w = _prepare_weights(p, yp.shape[1])
        yp = _apply_layer(yp, mem_prompt, w)
        ys = _apply_layer(ys, mem_sample, w)
    return (yp, ys)
```

```python
import functools
import math

import jax
import jax.numpy as jnp
from jax import lax
from jax.experimental import pallas as pl
from jax.experimental.pallas import tpu as pltpu

F32 = jnp.float32
BF16 = jnp.bfloat16

EPS = 1e-6
D_MODEL = 2048
GLA_HEADS = 4
GLA_DK = 128
GLA_DV = 256
GLA_GATE_RANK = 16
GLA_TAU = 16.0
GLA_CHUNK = 64
GLA_SUB = 256
GLA_QK = GLA_HEADS * GLA_DK
GLA_WIDTH = GLA_HEADS * GLA_DV
MLA_HEADS = 8
MLA_RANK = 512
MLA_NOPE = 128
MLA_ROPE = 64
MLA_V = 128
MLA_QK = MLA_NOPE + MLA_ROPE
MLA_WIDTH = MLA_HEADS * MLA_V
ROPE_BASE = 10000.0
MEM_HEADS = 4
MEM_HEAD_DIM = D_MODEL // MEM_HEADS
D_FF = 5632
CONV_WIDTH = 3
FFN_SUB = 256

LANES = 128
SUBLANES = 8
BF16_ROWS = 16
VMEM_LIMIT_BYTES = 56 * 1024 * 1024
MLA_VT_ROWS = MLA_V + BF16_ROWS
LOG2E = 1.4426950408889634

IN_GQ = 0
IN_GK = IN_GQ + GLA_QK
IN_GV = IN_GK + GLA_QK
IN_GR = IN_GV + GLA_WIDTH
IN_CQ = IN_GR + GLA_WIDTH
IN_CKV = IN_CQ + MLA_RANK
IN_GATE = IN_CKV + MLA_RANK
IN_KR = IN_GATE + LANES
IN_WIDTH_PADDED = IN_KR + LANES


def _cparams(*semantics):
    return pltpu.CompilerParams(dimension_semantics=semantics, vmem_limit_bytes=VMEM_LIMIT_BYTES)


def _resident(shape):
    zeros = (0,) * len(shape)
    return pl.BlockSpec(shape, lambda *_: zeros, pipeline_mode=pl.Buffered(1))


def _rms(x, g):
    return x * lax.rsqrt(jnp.mean(x * x, axis=-1, keepdims=True) + EPS) * g


def _dot(a, b):
    return jnp.dot(a, b, preferred_element_type=F32)


def _dot_nt(a, b):
    return lax.dot_general(a, b, (((1,), (1,)), ((), ())), preferred_element_type=F32)


def _dot_tn(a, b):
    return lax.dot_general(a, b, (((0,), (0,)), ((), ())), preferred_element_type=F32)


def _col_chunk(n):
    for c in (512, 256, LANES):
        if n % c == 0:
            return c
    raise ValueError(f"width {n} is not a multiple of {LANES}")


def _norm_matmul_body(x_ref, g_ref, w_ref, o_ref):
    h = _rms(x_ref[...], g_ref[...]).astype(BF16)
    n = o_ref.shape[-1]
    nc = _col_chunk(n)
    for c in range(0, n, nc):
        o_ref[:, c:c + nc] = _dot(h, w_ref[:, c:c + nc]).astype(o_ref.dtype)


def _norm_matmul(x, g, w, tm):
    t, d = x.shape
    n = w.shape[1]
    return pl.pallas_call(
        _norm_matmul_body,
        out_shape=jax.ShapeDtypeStruct((t, n), BF16),
        grid=(t // tm,),
        in_specs=[pl.BlockSpec((tm, d), lambda i: (i, 0)), _resident((1, d)), _resident((d, n))],
        out_specs=pl.BlockSpec((tm, n), lambda i: (i, 0)),
        compiler_params=_cparams("parallel"),
        name="norm_matmul",
    )(x, g, w)


def _log_sigmoid(x):
    return jnp.minimum(x, 0.0) - jnp.log(1.0 + jnp.exp(-jnp.abs(x)))


def _gla_tables(reverse):
    shift = GLA_CHUNK.bit_length() - 1
    row = lax.broadcasted_iota(jnp.int32, (GLA_SUB, GLA_SUB), 0)
    col = lax.broadcasted_iota(jnp.int32, (GLA_SUB, GLA_SUB), 1)
    same_chunk = lax.shift_right_logical(row, shift) == lax.shift_right_logical(col, shift)
    in_chunk = lambda cond: jnp.where(same_chunk, jnp.where(cond, 1.0, 0.0), 0.0)
    if reverse:
        cum = in_chunk(col >= row)
        keep = in_chunk(col > row) > 0.5
    else:
        cum = in_chunk(col <= row)
        keep = cum > 0.5
    return cum.astype(BF16), keep


def _gla_log_decay(gate_ref, w2_ref, b2_ref, la_ref):
    logits = _dot(gate_ref[0], w2_ref[...]) + b2_ref[...]
    la_ref[...] = _log_sigmoid(logits) * (1.0 / GLA_TAU)


def _gla_block(q_ref, k_ref, v_ref, la_ref, state_ref, reverse):
    ts = q_ref.shape[1]
    c = GLA_CHUNK
    n_chunks = ts // c
    cum_mat, keep = _gla_tables(reverse)
    la = la_ref[...]
    la_hi = la.astype(BF16)
    la_lo = (la - la_hi.astype(F32)).astype(BF16)
    cum = jnp.concatenate(
        [_dot(cum_mat, la_hi[r:r + GLA_SUB]) + _dot(cum_mat, la_lo[r:r + GLA_SUB])
         for r in range(0, ts, GLA_SUB)], axis=0).reshape(n_chunks, c, GLA_QK)
    mid_row, last_row = (c // 2 - 1, 0) if reverse else (c // 2, c - 1)
    mid = cum[:, mid_row:mid_row + 1, :]
    last = cum[:, last_row:last_row + 1, :]
    q = q_ref[0].astype(F32).reshape(n_chunks, c, GLA_QK) * (GLA_DK ** -0.5)
    k = k_ref[0].astype(F32).reshape(n_chunks, c, GLA_QK)
    qi_f = q * jnp.exp(cum - mid)
    ki_f = k * jnp.exp(mid - cum)
    flat = lambda t: t.reshape(ts, GLA_QK).astype(BF16)
    qi = flat(qi_f)
    ki = flat(ki_f)
    qd = flat(qi_f * jnp.exp(mid))
    kd = flat(ki_f * jnp.exp(last - mid))
    chunk_decay = jnp.exp(last)
    order = range(n_chunks - 1, -1, -1) if reverse else range(n_chunks)
    outs = []
    for h in range(GLA_HEADS):
        kc = slice(h * GLA_DK, (h + 1) * GLA_DK)
        v = v_ref[0, :, h * GLA_DV:(h + 1) * GLA_DV]
        intra = []
        for r in range(0, ts, GLA_SUB):
            rows = slice(r, r + GLA_SUB)
            att = jnp.where(keep, _dot_nt(qi[rows, kc], ki[rows, kc]), 0.0).astype(BF16)
            intra.append(_dot(att, v[rows]))
        state = state_ref[h]
        inter = [None] * n_chunks
        for n in order:
            rows = slice(n * c, (n + 1) * c)
            inter[n] = _dot_nt(qd[rows, kc], state.astype(BF16))
            state = state * chunk_decay[n, :, kc] + _dot_tn(v[rows], kd[rows, kc])
        state_ref[h] = state
        outs.append(jnp.concatenate(intra, axis=0) + jnp.concatenate(inter, axis=0))
    return outs


def _gla_fwd_body(q_ref, k_ref, v_ref, gate_ref, w2_ref, b2_ref, o_ref, state_ref, la_ref):
    @pl.when(pl.program_id(1) == 0)
    def _():
        state_ref[...] = jnp.zeros_like(state_ref)

    _gla_log_decay(gate_ref, w2_ref, b2_ref, la_ref)
    outs = _gla_block(q_ref, k_ref, v_ref, la_ref, state_ref, False)
    for h in range(GLA_HEADS):
        o_ref[0, :, h * GLA_DV:(h + 1) * GLA_DV] = outs[h]


def _gla_bwd_body(q_ref, k_ref, v_ref, gate_ref, w2_ref, b2_ref, of_ref, r_ref, gn_ref,
                  o_ref, state_ref, la_ref):
    @pl.when(pl.program_id(1) == 0)
    def _():
        state_ref[...] = jnp.zeros_like(state_ref)

    _gla_log_decay(gate_ref, w2_ref, b2_ref, la_ref)
    outs = _gla_block(q_ref, k_ref, v_ref, la_ref, state_ref, True)
    gn = gn_ref[...]
    for h in range(GLA_HEADS):
        vcols = slice(h * GLA_DV, (h + 1) * GLA_DV)
        o = outs[h] + of_ref[0, :, vcols]
        half_r = 0.5 * r_ref[0, :, vcols].astype(F32)
        silu = half_r + half_r * jnp.tanh(half_r)
        o_ref[0, :, vcols] = (_rms(o, gn) * silu).astype(o_ref.dtype)


def _gla(proj, w2f, b2f, w2b, b2b, out_norm, ts):
    b, s, _ = proj.shape
    ns = s // ts

    def col_spec(width, offset, rev):
        blk = offset // width
        if rev:
            return pl.BlockSpec((1, ts, width), lambda bi, si: (bi, ns - 1 - si, blk))
        return pl.BlockSpec((1, ts, width), lambda bi, si: (bi, si, blk))

    def common_specs(rev):
        return [col_spec(GLA_QK, IN_GQ, rev), col_spec(GLA_QK, IN_GK, rev), col_spec(GLA_WIDTH, IN_GV, rev),
                col_spec(LANES, IN_GATE, rev), _resident((LANES, GLA_QK)), _resident((1, GLA_QK))]

    scratch = [pltpu.VMEM((GLA_HEADS, GLA_DV, GLA_DK), F32), pltpu.VMEM((ts, GLA_QK), F32)]
    o_fwd = pl.pallas_call(
        _gla_fwd_body,
        out_shape=jax.ShapeDtypeStruct((b, s, GLA_WIDTH), F32),
        grid=(b, ns),
        in_specs=common_specs(False),
        out_specs=pl.BlockSpec((1, ts, GLA_WIDTH), lambda bi, si: (bi, si, 0)),
        scratch_shapes=scratch,
        compiler_params=_cparams("parallel", "arbitrary"),
        name="gla_fwd",
    )(proj, proj, proj, proj, w2f, b2f)
    return pl.pallas_call(
        _gla_bwd_body,
        out_shape=jax.ShapeDtypeStruct((b, s, GLA_WIDTH), BF16),
        grid=(b, ns),
        in_specs=common_specs(True) + [
            pl.BlockSpec((1, ts, GLA_WIDTH), lambda bi, si: (bi, ns - 1 - si, 0)),
            col_spec(GLA_WIDTH, IN_GR, True),
            _resident((1, GLA_DV)),
        ],
        out_specs=pl.BlockSpec((1, ts, GLA_WIDTH), lambda bi, si: (bi, ns - 1 - si, 0)),
        scratch_shapes=scratch,
        compiler_params=_cparams("parallel", "arbitrary"),
        name="gla_bwd",
    )(proj, proj, proj, proj, w2b, b2b, o_fwd, proj, out_norm)


def _mla_prep_body(cq_ref, ckv_ref, kr_ref, qn_ref, kvn_ref, wq_nope_ref, wq_rope_ref, wq_rot_ref, wkv_ref,
                   cos_ref, sin_ref, q_out, k_out, vt_out):
    hq = _rms(cq_ref[0].astype(F32), qn_ref[...]).astype(BF16)
    hkv = _rms(ckv_ref[0].astype(F32), kvn_ref[...]).astype(BF16)
    lane_tiles = MLA_HEADS * MLA_ROPE // LANES
    cos = jnp.concatenate([cos_ref[...]] * lane_tiles, axis=1)
    sin = jnp.concatenate([sin_ref[...]] * lane_tiles, axis=1)
    q_scale = MLA_QK ** -0.5 * LOG2E
    q_nope = _dot(hq, wq_nope_ref[...]) * q_scale
    q_pe = (_dot(hq, wq_rope_ref[...]) * cos + _dot(hq, wq_rot_ref[...]) * sin) * q_scale
    kv = _dot(hkv, wkv_ref[...])
    kr = kr_ref[0].astype(F32)
    k_pe = (kr[:, :MLA_ROPE] * cos[:, :MLA_ROPE] + kr[:, MLA_ROPE:] * sin[:, :MLA_ROPE]).astype(BF16)
    pad_rows = (MLA_VT_ROWS - MLA_V, cq_ref.shape[1])
    ones_row = jnp.where(lax.broadcasted_iota(jnp.int32, pad_rows, 0) == 0, 1.0, 0.0).astype(BF16)
    for h in range(MLA_HEADS):
        q_out[0, h, :, :MLA_NOPE] = q_nope[:, h * MLA_NOPE:(h + 1) * MLA_NOPE].astype(BF16)
        q_out[0, h, :, MLA_NOPE:] = q_pe[:, h * MLA_ROPE:(h + 1) * MLA_ROPE].astype(BF16)
        base = h * (MLA_NOPE + MLA_V)
        k_out[0, h, :, :MLA_NOPE] = kv[:, base:base + MLA_NOPE].astype(BF16)
        k_out[0, h, :, MLA_NOPE:] = k_pe
        vt_out[0, h, :MLA_V] = kv[:, base + MLA_NOPE:base + MLA_NOPE + MLA_V].T.astype(BF16)
        vt_out[0, h, MLA_V:] = ones_row


def _reduce_rows(op, x, groups=8):
    rows, n = x.shape
    partial = op(x.reshape(groups, rows // groups, n), axis=0)
    return op(partial, axis=0, keepdims=True)


def _mla_attn_body(q_ref, k_ref, vt_ref, o_ref, s_ref, *, tk):
    q = q_ref[0, 0]
    n_chunks = k_ref.shape[2] // tk

    n_slots = s_ref.shape[0]
    chunk_max = [None] * n_chunks

    def scores(c):
        s = _dot_nt(k_ref[0, 0, c * tk:(c + 1) * tk, :], q)
        s_ref[c % n_slots] = s
        chunk_max[c] = _reduce_rows(jnp.max, s)

    m = acc = None
    for c in range(min(n_slots - 1, n_chunks)):
        scores(c)
    for c in range(n_chunks):
        if c + n_slots - 1 < n_chunks:
            scores(c + n_slots - 1)
        m_new = chunk_max[c] if m is None else jnp.maximum(m, chunk_max[c])
        p = jnp.exp2(s_ref[c % n_slots] - m_new).astype(BF16)
        pv = _dot(vt_ref[0, 0, :, c * tk:(c + 1) * tk], p)
        acc = pv if acc is None else jnp.exp2(m - m_new) * acc + pv
        m = m_new
    o_ref[0] = (acc[:MLA_V] / acc[MLA_V:MLA_V + 1]).T.astype(o_ref.dtype)


def _mla(proj, q_norm, kv_norm, wq_nope, wq_rope, wq_rot, wkv, cos_t, sin_t, ts, tq, tk):
    b, s, _ = proj.shape
    rope_w = MLA_HEADS * MLA_ROPE
    q, k, vt = pl.pallas_call(
        _mla_prep_body,
        out_shape=(jax.ShapeDtypeStruct((b, MLA_HEADS, s, MLA_QK), BF16),
                   jax.ShapeDtypeStruct((b, MLA_HEADS, s, MLA_QK), BF16),
                   jax.ShapeDtypeStruct((b, MLA_HEADS, MLA_VT_ROWS, s), BF16)),
        grid=(b, s // ts),
        in_specs=[
            pl.BlockSpec((1, ts, MLA_RANK), lambda bi, si: (bi, si, IN_CQ // MLA_RANK)),
            pl.BlockSpec((1, ts, MLA_RANK), lambda bi, si: (bi, si, IN_CKV // MLA_RANK)),
            pl.BlockSpec((1, ts, LANES), lambda bi, si: (bi, si, IN_KR // LANES)),
            _resident((1, MLA_RANK)), _resident((1, MLA_RANK)),
            _resident((MLA_RANK, MLA_HEADS * MLA_NOPE)), _resident((MLA_RANK, rope_w)),
            _resident((MLA_RANK, rope_w)), _resident((MLA_RANK, MLA_HEADS * (MLA_NOPE + MLA_V))),
            pl.BlockSpec((ts, LANES), lambda bi, si: (si, 0)),
            pl.BlockSpec((ts, LANES), lambda bi, si: (si, 0)),
        ],
        out_specs=(pl.BlockSpec((1, MLA_HEADS, ts, MLA_QK), lambda bi, si: (bi, 0, si, 0)),
                   pl.BlockSpec((1, MLA_HEADS, ts, MLA_QK), lambda bi, si: (bi, 0, si, 0)),
                   pl.BlockSpec((1, MLA_HEADS, MLA_VT_ROWS, ts), lambda bi, si: (bi, 0, 0, si))),
        compiler_params=_cparams("parallel", "parallel"),
        name="mla_prep",
    )(proj, proj, proj, q_norm, kv_norm, wq_nope, wq_rope, wq_rot, wkv, cos_t, sin_t)
    return pl.pallas_call(
        functools.partial(_mla_attn_body, tk=tk),
        out_shape=jax.ShapeDtypeStruct((b, s, MLA_WIDTH), BF16),
        grid=(b, MLA_HEADS, s // tq),
        in_specs=[
            pl.BlockSpec((1, 1, tq, MLA_QK), lambda bi, hi, qi: (bi, hi, qi, 0)),
            pl.BlockSpec((1, 1, s, MLA_QK), lambda bi, hi, qi: (bi, hi, 0, 0)),
            pl.BlockSpec((1, 1, MLA_VT_ROWS, s), lambda bi, hi, qi: (bi, hi, 0, 0)),
        ],
        out_specs=pl.BlockSpec((1, tq, MLA_V), lambda bi, hi, qi: (bi, qi, hi)),
        scratch_shapes=[pltpu.VMEM((3, tk, tq), F32)],
        compiler_params=_cparams("parallel", "parallel", "parallel"),
        name="mla_attn",
    )(q, k, vt)


def _residual_norm_store(x, o_ref, sumsq, g):
    scale = lax.rsqrt(sumsq * (1.0 / o_ref.shape[-1]) + EPS)
    o_ref[...] = x + o_ref[...] * scale * g


def _project_rows(o_ref, lhs_and_weights):
    n = o_ref.shape[-1]
    nc = _col_chunk(n)
    sumsq = jnp.zeros((o_ref.shape[0], 1), F32)
    for c in range(0, n, nc):
        y = sum(_dot(a, w_ref[:, c:c + nc]) for a, w_ref in lhs_and_weights)
        sumsq = sumsq + jnp.sum(y * y, axis=-1, keepdims=True)
        o_ref[:, c:c + nc] = y
    return sumsq


def _mix_out_body(a1_ref, a2_ref, w1_ref, w2_ref, x_ref, g_ref, o_ref):
    sumsq = _project_rows(o_ref, [(a1_ref[...], w1_ref), (a2_ref[...], w2_ref)])
    _residual_norm_store(x_ref[...], o_ref, sumsq, g_ref[...])


def _mix_out(o_gla, o_mla, w1, w2, x, g, tm):
    t, d = x.shape
    return pl.pallas_call(
        _mix_out_body,
        out_shape=jax.ShapeDtypeStruct((t, d), F32),
        grid=(t // tm,),
        in_specs=[
            pl.BlockSpec((tm, GLA_WIDTH), lambda i: (i, 0)),
            pl.BlockSpec((tm, MLA_WIDTH), lambda i: (i, 0)),
            _resident((GLA_WIDTH, d)), _resident((MLA_WIDTH, d)),
            pl.BlockSpec((tm, d), lambda i: (i, 0)),
            _resident((1, d)),
        ],
        out_specs=pl.BlockSpec((tm, d), lambda i: (i, 0)),
        compiler_params=_cparams("parallel"),
        name="mix_out",
    )(o_gla, o_mla, w1, w2, x, g)


def _mem_attn_body(x_ref, gpre_ref, wq_ref, kv_ref, wo_ref, gpost_ref, o_ref, a_ref, q_ref):
    q_scale = MEM_HEAD_DIM ** -0.5 * LOG2E
    x = x_ref[0]
    h = _rms(x, gpre_ref[...]).astype(BF16)
    head_cols = [slice(hd * MEM_HEAD_DIM, (hd + 1) * MEM_HEAD_DIM) for hd in range(MEM_HEADS)]

    def project_q(hd):
        q_ref[hd % 2] = (_dot(h, wq_ref[:, head_cols[hd]]) * q_scale).astype(BF16)

    project_q(0)
    for hd in range(MEM_HEADS):
        if hd + 1 < MEM_HEADS:
            project_q(hd + 1)
        cols = head_cols[hd]
        vcols = slice(D_MODEL + hd * MEM_HEAD_DIM, D_MODEL + (hd + 1) * MEM_HEAD_DIM)
        s = _dot_nt(q_ref[hd % 2], kv_ref[0, :, cols])
        m = jnp.max(s, axis=-1, keepdims=True)
        p = jnp.exp2(s - m)
        l = jnp.sum(p, axis=-1, keepdims=True)
        a_ref[:, cols] = (_dot(p.astype(BF16), kv_ref[0, :, vcols]) / l).astype(BF16)
    sumsq = _project_rows(o_ref.at[0], [(a_ref[...], wo_ref)])
    _residual_norm_store(x, o_ref.at[0], sumsq, gpost_ref[...])


def _mem_attn(x, gpre, wq, kv, wo, gpost, tm):
    b, s, d = x.shape
    m = kv.shape[1]
    return pl.pallas_call(
        _mem_attn_body,
        out_shape=jax.ShapeDtypeStruct((b, s, d), F32),
        grid=(b, s // tm),
        in_specs=[
            pl.BlockSpec((1, tm, d), lambda bi, si: (bi, si, 0)),
            _resident((1, d)), _resident((d, d)),
            pl.BlockSpec((1, m, 2 * d), lambda bi, si: (bi, 0, 0)),
            _resident((d, d)), _resident((1, d)),
        ],
        out_specs=pl.BlockSpec((1, tm, d), lambda bi, si: (bi, si, 0)),
        scratch_shapes=[pltpu.VMEM((tm, d), BF16), pltpu.VMEM((2, tm, MEM_HEAD_DIM), BF16)],
        compiler_params=_cparams("parallel", "parallel"),
        name="mem_attn",
    )(x, gpre, wq, kv, wo, gpost)


def _gelu_tanh_times(x, v_half):
    c = math.sqrt(2.0 / math.pi)
    inner = x * (c + (c * 0.044715) * (x * x))
    return x * (1.0 + jnp.tanh(inner)) * v_half


def _ffn_body(x_ref, xprev_ref, xnext_ref, gpre_ref, wg_ref, wv_ref, cwg_ref, cwv_ref, cbg_ref, cbv_ref,
              wd_ref, gpost_ref, o_ref, h_ref, u_ref):
    si = pl.program_id(1)
    j = pl.program_id(2)
    tm = x_ref.shape[1]
    sub = FFN_SUB
    hl = SUBLANES
    n_lane_tiles = 2 * sub // LANES

    @pl.when(j == 0)
    def _():
        g = gpre_ref[...]
        h_ref[:tm] = _rms(x_ref[0], g).astype(BF16)
        nxt = jnp.where(si == pl.num_programs(1) - 1, 0.0, _rms(xnext_ref[0], g))
        prv = jnp.where(si == 0, 0.0, _rms(xprev_ref[0], g))
        h_ref[tm:] = jnp.concatenate([nxt, prv], axis=0).astype(BF16)
        o_ref[...] = jnp.zeros_like(o_ref)

    h = h_ref[...]
    n_sub = wd_ref.shape[0] // sub
    cols = [slice(c * sub, (c + 1) * sub) for c in range(n_sub)]

    def gate_value(g_ref, v_ref, c):
        return jnp.concatenate([g_ref[:, cols[c]], v_ref[:, cols[c]]], axis=1)

    def up_project(c):
        u = _dot(h, gate_value(wg_ref, wv_ref, c))
        for k in range(n_lane_tiles):
            lanes = slice(k * LANES, (k + 1) * LANES)
            u_ref[c % 2, k, hl:hl + tm] = u[:tm, lanes]
            u_ref[c % 2, k, :hl] = u[tm + hl:, lanes]
            u_ref[c % 2, k, hl + tm:] = u[tm:tm + hl, lanes]

    def window(c, offset):
        return jnp.concatenate([u_ref[c % 2, k, hl + offset:hl + offset + tm] for k in range(n_lane_tiles)],
                               axis=1)

    def gate_and_down(c):
        cw = gate_value(cwg_ref, cwv_ref, c)
        cu = (window(c, -1) * cw[0:1] + window(c, 0) * cw[1:2] + window(c, 1) * cw[2:3]
              + gate_value(cbg_ref, cbv_ref, c))
        act = _gelu_tanh_times(cu[:, :sub], cu[:, sub:]).astype(BF16)
        n = o_ref.shape[-1]
        nc = _col_chunk(n)
        for k in range(0, n, nc):
            o_ref[0, :, k:k + nc] += _dot(act, wd_ref[cols[c], k:k + nc])

    up_project(0)
    for c in range(n_sub):
        if c + 1 < n_sub:
            up_project(c + 1)
        gate_and_down(c)

    @pl.when(j == pl.num_programs(2) - 1)
    def _():
        o_ref[0] = x_ref[0] + _rms(o_ref[0], gpost_ref[...])


def _ffn(x, gpre, w_up, conv_w, conv_b, w_down, gpost, tm, fc):
    b, s, d = x.shape
    nf = D_FF // fc
    halo = SUBLANES
    tb = tm // halo
    last_blk = s // halo - 1
    return pl.pallas_call(
        _ffn_body,
        out_shape=jax.ShapeDtypeStruct((b, s, d), F32),
        grid=(b, s // tm, nf),
        in_specs=[
            pl.BlockSpec((1, tm, d), lambda bi, si, j: (bi, si, 0)),
            pl.BlockSpec((1, halo, d), lambda bi, si, j: (bi, jnp.maximum(si * tb - 1, 0), 0)),
            pl.BlockSpec((1, halo, d), lambda bi, si, j: (bi, jnp.minimum((si + 1) * tb, last_blk), 0)),
            _resident((1, d)),
            pl.BlockSpec((d, fc), lambda bi, si, j: (0, j)),
            pl.BlockSpec((d, fc), lambda bi, si, j: (0, nf + j)),
            pl.BlockSpec((CONV_WIDTH, fc), lambda bi, si, j: (0, j)),
            pl.BlockSpec((CONV_WIDTH, fc), lambda bi, si, j: (0, nf + j)),
            pl.BlockSpec((1, fc), lambda bi, si, j: (0, j)),
            pl.BlockSpec((1, fc), lambda bi, si, j: (0, nf + j)),
            pl.BlockSpec((fc, d), lambda bi, si, j: (j, 0)),
            _resident((1, d)),
        ],
        out_specs=pl.BlockSpec((1, tm, d), lambda bi, si, j: (bi, si, 0)),
        scratch_shapes=[pltpu.VMEM((tm + 2 * halo, d), BF16),
                        pltpu.VMEM((2, 2 * FFN_SUB // LANES, tm + 2 * halo, LANES), F32)],
        compiler_params=_cparams("parallel", "parallel", "arbitrary"),
        name="conv_ffn",
    )(x, x, x, gpre, w_up, w_up, conv_w, conv_w, conv_b, conv_b, w_down, gpost)


def _rotate_half_columns(w):
    k, n = w.shape
    g = w.reshape(k, n // MLA_ROPE, 2, MLA_ROPE // 2)
    return jnp.concatenate([-g[:, :, 1:], g[:, :, :1]], axis=2).reshape(k, n)


def _prepare_weights(p, seq):
    row = lambda v: v.reshape(1, -1).astype(F32)
    w_in = p["w_in"]
    offs, acc = [], 0
    for wdt in (GLA_QK, GLA_QK, GLA_WIDTH, GLA_WIDTH, GLA_GATE_RANK, GLA_GATE_RANK, MLA_RANK, MLA_RANK, MLA_ROPE):
        offs.append((acc, acc + wdt))
        acc += wdt
    part = [w_in[:, a:b] for a, b in offs]
    gq, gk, gv, gr, ggf, ggb, cq, ckv, kr = part
    d = w_in.shape[0]
    gate_pad = jnp.zeros((d, LANES - 2 * GLA_GATE_RANK), w_in.dtype)
    w_in_p = jnp.concatenate([gq, gk, gv, gr, cq, ckv, ggf, ggb, gate_pad, kr, _rotate_half_columns(kr)],
                             axis=1).astype(BF16)

    def gate_w2(w2, slot):
        full = jnp.zeros((LANES, GLA_QK), F32)
        return full.at[slot * GLA_GATE_RANK:(slot + 1) * GLA_GATE_RANK].set(w2).astype(BF16)

    wq = p["mla_w_q_up"].reshape(MLA_RANK, MLA_HEADS, MLA_QK)
    wq_nope = wq[:, :, :MLA_NOPE].reshape(MLA_RANK, MLA_HEADS * MLA_NOPE)
    wq_rope = wq[:, :, MLA_NOPE:].reshape(MLA_RANK, MLA_HEADS * MLA_ROPE)

    half = MLA_ROPE // 2
    freqs = ROPE_BASE ** (-jnp.arange(half, dtype=F32) / half)
    ang = jnp.arange(seq, dtype=F32)[:, None] * freqs[None, :]
    cos_t = jnp.tile(jnp.cos(ang), (1, LANES // half))
    sin_t = jnp.tile(jnp.sin(ang), (1, LANES // half))

    value_half = jnp.concatenate([jnp.ones((1, D_FF), F32), jnp.full((1, D_FF), 0.5, F32)], axis=1)

    return dict(
        g_mix_pre=row(p["norm_mix_pre"]), g_mix_post=row(p["norm_mix_post"]), w_in=w_in_p,
        w2f=gate_w2(p["gla_gate_w2_fwd"], 0), b2f=row(p["gla_gate_b_fwd"]),
        w2b=gate_w2(p["gla_gate_w2_bwd"], 1), b2b=row(p["gla_gate_b_bwd"]),
        gla_out_norm=row(p["gla_out_norm"]),
        mla_q_norm=row(p["mla_q_norm"]), mla_kv_norm=row(p["mla_kv_norm"]),
        wq_nope=wq_nope.astype(BF16), wq_rope=wq_rope.astype(BF16),
        wq_rot=_rotate_half_columns(wq_rope).astype(BF16), wkv=p["mla_w_kv_up"].astype(BF16),
        cos=cos_t, sin=sin_t,
        w_out_gla=p["w_out"][:GLA_WIDTH].astype(BF16), w_out_mla=p["w_out"][GLA_WIDTH:].astype(BF16),
        g_mem_pre=row(p["norm_mem_pre"]), g_mem_post=row(p["norm_mem_post"]), g_mem_kv=row(p["mem_kv_norm"]),
        w_mem_q=p["w_mem_q"].astype(BF16), w_mem_o=p["w_mem_o"].astype(BF16),
        w_mem_kv=jnp.concatenate([p["w_mem_k"], p["w_mem_v"]], axis=1).astype(BF16),
        g_ffn_pre=row(p["norm_ffn_pre"]), g_ffn_post=row(p["norm_ffn_post"]),
        w_ffn_up=p["w_ffn_up"].astype(BF16),
        ffn_conv_w=p["ffn_conv_w"].astype(F32) * value_half,
        ffn_conv_b=row(p["ffn_conv_b"]) * value_half,
        w_ffn_down=p["w_ffn_down"].astype(BF16),
    )


def _tiles(seq):
    tm = min(512, seq)
    return dict(tm=tm, gla_ts=min(1024, seq), mla_ts=tm, mla_tq=min(1024, seq), mla_tk=tm, ffn_tm=tm, ffn_fc=512)


def _apply_layer(x, mem, w):
    b, s, d = x.shape
    t = _tiles(s)
    tm = t["tm"]
    x2 = x.reshape(b * s, d)
    proj = _norm_matmul(x2, w["g_mix_pre"], w["w_in"], tm).reshape(b, s, IN_WIDTH_PADDED)
    o_gla = _gla(proj, w["w2f"], w["b2f"], w["w2b"], w["b2b"], w["gla_out_norm"], t["gla_ts"])
    o_mla = _mla(proj, w["mla_q_norm"], w["mla_kv_norm"], w["wq_nope"], w["wq_rope"], w["wq_rot"], w["wkv"],
                 w["cos"], w["sin"], t["mla_ts"], t["mla_tq"], t["mla_tk"])
    x2 = _mix_out(o_gla.reshape(b * s, GLA_WIDTH), o_mla.reshape(b * s, MLA_WIDTH),
                  w["w_out_gla"], w["w_out_mla"], x2, w["g_mix_post"], tm)
    mtok = mem.shape[1]
    kv = _norm_matmul(mem.reshape(b * mtok, d), w["g_mem_kv"], w["w_mem_kv"], min(tm, b * mtok))
    x3 = _mem_attn(x2.reshape(b, s, d), w["g_mem_pre"], w["w_mem_q"], kv.reshape(b, mtok, 2 * d),
                   w["w_mem_o"], w["g_mem_post"], tm)
    return _ffn(x3, w["g_ffn_pre"], w["w_ffn_up"], w["ffn_conv_w"], w["ffn_conv_b"], w["w_ffn_down"],
                w["g_ffn_post"], t["ffn_tm"], t["ffn_fc"])


def kernel(x_prompt, x_sample, mem_prompt, mem_sample, norm_mix_pre, norm_mix_post, w_in, gla_gate_w2_fwd, gla_gate_b_fwd, gla_gate_w2_bwd, gla_gate_b_bwd, gla_out_norm, mla_q_norm, mla_w_q_up, mla_kv_norm, mla_w_kv_up, w_out, norm_mem_pre, norm_mem_post, mem_kv_norm, w_mem_q, w_mem_k, w_mem_v, w_mem_o, norm_ffn_pre, norm_ffn_post, w_ffn_up, ffn_conv_w, ffn_conv_b, w_ffn_down):
    params = dict(
        norm_mix_pre=norm_mix_pre, norm_mix_post=norm_mix_post, w_in=w_in,
        gla_gate_w2_fwd=gla_gate_w2_fwd, gla_gate_b_fwd=gla_gate_b_fwd,
        gla_gate_w2_bwd=gla_gate_w2_bwd, gla_gate_b_bwd=gla_gate_b_bwd, gla_out_norm=gla_out_norm,
        mla_q_norm=mla_q_norm, mla_w_q_up=mla_w_q_up, mla_kv_norm=mla_kv_norm, mla_w_kv_up=mla_w_kv_up,
        w_out=w_out, norm_mem_pre=norm_mem_pre, norm_mem_post=norm_mem_post, mem_kv_norm=mem_kv_norm,
        w_mem_q=w_mem_q, w_mem_k=w_mem_k, w_mem_v=w_mem_v, w_mem_o=w_mem_o,
        norm_ffn_pre=norm_ffn_pre, norm_ffn_post=norm_ffn_post, w_ffn_up=w_ffn_up,
        ffn_conv_w=ffn_conv_w, ffn_conv_b=ffn_conv_b, w_ffn_down=w_ffn_down,
    )
    depth = w_in.shape[0]
    yp, ys = x_prompt, x_sample
    for layer in range(depth):
        p = {k: v[layer] for k, v in params.items()}
        assert yp.shape[1] == ys.shape[1]
        w = _prepare_weights(p, yp.shape[1])
        yp = _apply_layer(yp, mem_prompt, w)
        ys = _apply_layer(ys, mem_sample, w)
    return (yp, ys)
```

```python
import functools
import math

import jax
import jax.numpy as jnp
from jax import lax
from jax.experimental import pallas as pl
from jax.experimental.pallas import tpu as pltpu

F32 = jnp.float32
BF16 = jnp.bfloat16

EPS = 1e-6
D_MODEL = 2048
GLA_HEADS = 4
GLA_DK = 128
GLA_DV = 256
GLA_GATE_RANK = 16
GLA_TAU = 16.0
GLA_CHUNK = 64
GLA_SUB = 256
GLA_QK = GLA_HEADS * GLA_DK
GLA_WIDTH = GLA_HEADS * GLA_DV
MLA_HEADS = 8
MLA_RANK = 512
MLA_NOPE = 128
MLA_ROPE = 64
MLA_V = 128
MLA_QK = MLA_NOPE + MLA_ROPE
MLA_WIDTH = MLA_HEADS * MLA_V
ROPE_BASE = 10000.0
MEM_HEADS = 4
MEM_HEAD_DIM = D_MODEL // MEM_HEADS
D_FF = 5632
CONV_WIDTH = 3
FFN_SUB = 256

LANES = 128
SUBLANES = 8
BF16_ROWS = 16
VMEM_LIMIT_BYTES = 56 * 1024 * 1024
MLA_VT_ROWS = MLA_V + BF16_ROWS
LOG2E = 1.4426950408889634

IN_GQ = 0
IN_GK = IN_GQ + GLA_QK
IN_GV = IN_GK + GLA_QK
IN_GR = IN_GV + GLA_WIDTH
IN_CQ = IN_GR + GLA_WIDTH
IN_CKV = IN_CQ + MLA_RANK
IN_GATE = IN_CKV + MLA_RANK
IN_KR = IN_GATE + LANES
IN_WIDTH_PADDED = IN_KR + LANES


def _cparams(*semantics):
    return pltpu.CompilerParams(dimension_semantics=semantics, vmem_limit_bytes=VMEM_LIMIT_BYTES)


def _resident(shape):
    zeros = (0,) * len(shape)
    return pl.BlockSpec(shape, lambda *_: zeros, pipeline_mode=pl.Buffered(1))


def _rms(x, g):
    return x * lax.rsqrt(jnp.mean(x * x, axis=-1, keepdims=True) + EPS) * g


def _dot(a, b):
    return jnp.dot(a, b, preferred_element_type=F32)


def _dot_nt(a, b):
    return lax.dot_general(a, b, (((1,), (1,)), ((), ())), preferred_element_type=F32)


def _dot_tn(a, b):
    return lax.dot_general(a, b, (((0,), (0,)), ((), ())), preferred_element_type=F32)


def _col_chunk(n):
    for c in (512, 256, LANES):
        if n % c == 0:
            return c
    raise ValueError(f"width {n} is not a multiple of {LANES}")


def _norm_matmul_body(x_ref, g_ref, w_ref, o_ref):
    h = _rms(x_ref[...], g_ref[...]).astype(BF16)
    n = o_ref.shape[-1]
    nc = _col_chunk(n)
    for c in range(0, n, nc):
        o_ref[:, c:c + nc] = _dot(h, w_ref[:, c:c + nc]).astype(o_ref.dtype)


def _norm_matmul(x, g, w, tm):
    t, d = x.shape
    n = w.shape[1]
    return pl.pallas_call(
        _norm_matmul_body,
        out_shape=jax.ShapeDtypeStruct((t, n), BF16),
        grid=(t // tm,),
        in_specs=[pl.BlockSpec((tm, d), lambda i: (i, 0)), _resident((1, d)), _resident((d, n))],
        out_specs=pl.BlockSpec((tm, n), lambda i: (i, 0)),
        compiler_params=_cparams("parallel"),
        name="norm_matmul",
    )(x, g, w)


def _log_sigmoid(x):
    return jnp.minimum(x, 0.0) - jnp.log(1.0 + jnp.exp(-jnp.abs(x)))


def _gla_tables(reverse):
    shift = GLA_CHUNK.bit_length() - 1
    row = lax.broadcasted_iota(jnp.int32, (GLA_SUB, GLA_SUB), 0)
    col = lax.broadcasted_iota(jnp.int32, (GLA_SUB, GLA_SUB), 1)
    same_chunk = lax.shift_right_logical(row, shift) == lax.shift_right_logical(col, shift)
    in_chunk = lambda cond: jnp.where(same_chunk, jnp.where(cond, 1.0, 0.0), 0.0)
    if reverse:
        cum = in_chunk(col >= row)
        keep = in_chunk(col > row) > 0.5
    else:
        cum = in_chunk(col <= row)
        keep = cum > 0.5
    return cum.astype(BF16), keep


def _gla_log_decay(gate_ref, w2_ref, b2_ref, la_ref):
    logits = _dot(gate_ref[0], w2_ref[...]) + b2_ref[...]
    la_ref[...] = _log_sigmoid(logits) * (1.0 / GLA_TAU)


def _gla_block(q_ref, k_ref, v_ref, la_ref, state_ref, reverse):
    ts = q_ref.shape[1]
    c = GLA_CHUNK
    n_chunks = ts // c
    cum_mat, keep = _gla_tables(reverse)
    la = la_ref[...]
    la_hi = la.astype(BF16)
    la_lo = (la - la_hi.astype(F32)).astype(BF16)
    cum = jnp.concatenate(
        [_dot(cum_mat, la_hi[r:r + GLA_SUB]) + _dot(cum_mat, la_lo[r:r + GLA_SUB])
         for r in range(0, ts, GLA_SUB)], axis=0).reshape(n_chunks, c, GLA_QK)
    mid_row, last_row = (c // 2 - 1, 0) if reverse else (c // 2, c - 1)
    mid = cum[:, mid_row:mid_row + 1, :]
    last = cum[:, last_row:last_row + 1, :]
    q = q_ref[0].astype(F32).reshape(n_chunks, c, GLA_QK) * (GLA_DK ** -0.5)
    k = k_ref[0].astype(F32).reshape(n_chunks, c, GLA_QK)
    qi_f = q * jnp.exp(cum - mid)
    ki_f = k * jnp.exp(mid - cum)
    flat = lambda t: t.reshape(ts, GLA_QK).astype(BF16)
    qi = flat(qi_f)
    ki = flat(ki_f)
    qd = flat(qi_f * jnp.exp(mid))
    kd = flat(ki_f * jnp.exp(last - mid))
    chunk_decay = jnp.exp(last)
    order = range(n_chunks - 1, -1, -1) if reverse else range(n_chunks)
    outs = []
    for h in range(GLA_HEADS):
        kc = slice(h * GLA_DK, (h + 1) * GLA_DK)
        v = v_ref[0, :, h * GLA_DV:(h + 1) * GLA_DV]
        intra = []
        for r in range(0, ts, GLA_SUB):
            rows = slice(r, r + GLA_SUB)
            att = jnp.where(keep, _dot_nt(qi[rows, kc], ki[rows, kc]), 0.0).astype(BF16)
            intra.append(_dot(att, v[rows]))
        state = state_ref[h]
        inter = [None] * n_chunks
        for n in order:
            rows = slice(n * c, (n + 1) * c)
            inter[n] = _dot_nt(qd[rows, kc], state.astype(BF16))
            state = state * chunk_decay[n, :, kc] + _dot_tn(v[rows], kd[rows, kc])
        state_ref[h] = state
        outs.append(jnp.concatenate(intra, axis=0) + jnp.concatenate(inter, axis=0))
    return outs


def _gla_fwd_body(q_ref, k_ref, v_ref, gate_ref, w2_ref, b2_ref, o_ref, state_ref, la_ref):
    @pl.when(pl.program_id(1) == 0)
    def _():
        state_ref[...] = jnp.zeros_like(state_ref)

    _gla_log_decay(gate_ref, w2_ref, b2_ref, la_ref)
    outs = _gla_block(q_ref, k_ref, v_ref, la_ref, state_ref, False)
    for h in range(GLA_HEADS):
        o_ref[0, :, h * GLA_DV:(h + 1) * GLA_DV] = outs[h]


def _gla_bwd_body(q_ref, k_ref, v_ref, gate_ref, w2_ref, b2_ref, of_ref, r_ref, gn_ref,
                  o_ref, state_ref, la_ref):
    @pl.when(pl.program_id(1) == 0)
    def _():
        state_ref[...] = jnp.zeros_like(state_ref)

    _gla_log_decay(gate_ref, w2_ref, b2_ref, la_ref)
    outs = _gla_block(q_ref, k_ref, v_ref, la_ref, state_ref, True)
    gn = gn_ref[...]
    for h in range(GLA_HEADS):
        vcols = slice(h * GLA_DV, (h + 1) * GLA_DV)
        o = outs[h] + of_ref[0, :, vcols]
        half_r = 0.5 * r_ref[0, :, vcols].astype(F32)
        silu = half_r + half_r * jnp.tanh(half_r)
        o_ref[0, :, vcols] = (_rms(o, gn) * silu).astype(o_ref.dtype)


def _gla(proj, w2f, b2f, w2b, b2b, out_norm, ts):
    b, s, _ = proj.shape
    ns = s // ts

    def col_spec(width, offset, rev):
        blk = offset // width
        if rev:
            return pl.BlockSpec((1, ts, width), lambda bi, si: (bi, ns - 1 - si, blk))
        return pl.BlockSpec((1, ts, width), lambda bi, si: (bi, si, blk))

    def common_specs(rev):
        return [col_spec(GLA_QK, IN_GQ, rev), col_spec(GLA_QK, IN_GK, rev), col_spec(GLA_WIDTH, IN_GV, rev),
                col_spec(LANES, IN_GATE, rev), _resident((LANES, GLA_QK)), _resident((1, GLA_QK))]

    scratch = [pltpu.VMEM((GLA_HEADS, GLA_DV, GLA_DK), F32), pltpu.VMEM((ts, GLA_QK), F32)]
    o_fwd = pl.pallas_call(
        _gla_fwd_body,
        out_shape=jax.ShapeDtypeStruct((b, s, GLA_WIDTH), F32),
        grid=(b, ns),
        in_specs=common_specs(False),
        out_specs=pl.BlockSpec((1, ts, GLA_WIDTH), lambda bi, si: (bi, si, 0)),
        scratch_shapes=scratch,
        compiler_params=_cparams("parallel", "arbitrary"),
        name="gla_fwd",
    )(proj, proj, proj, proj, w2f, b2f)
    return pl.pallas_call(
        _gla_bwd_body,
        out_shape=jax.ShapeDtypeStruct((b, s, GLA_WIDTH), BF16),
        grid=(b, ns),
        in_specs=common_specs(True) + [
            pl.BlockSpec((1, ts, GLA_WIDTH), lambda bi, si: (bi, ns - 1 - si, 0)),
            col_spec(GLA_WIDTH, IN_GR, True),
            _resident((1, GLA_DV)),
        ],
        out_specs=pl.BlockSpec((1, ts, GLA_WIDTH), lambda bi, si: (bi, ns - 1 - si, 0)),
        scratch_shapes=scratch,
        compiler_params=_cparams("parallel", "arbitrary"),
        name="gla_bwd",
    )(proj, proj, proj, proj, w2b, b2b, o_fwd, proj, out_norm)


def _mla_prep_body(cq_ref, ckv_ref, kr_ref, qn_ref, kvn_ref, wq_nope_ref, wq_rope_ref, wq_rot_ref, wkv_ref,
                   cos_ref, sin_ref, q_out, k_out, vt_out):
    hq = _rms(cq_ref[0].astype(F32), qn_ref[...]).astype(BF16)
    hkv = _rms(ckv_ref[0].astype(F32), kvn_ref[...]).astype(BF16)
    lane_tiles = MLA_HEADS * MLA_ROPE // LANES
    cos = jnp.concatenate([cos_ref[...]] * lane_tiles, axis=1)
    sin = jnp.concatenate([sin_ref[...]] * lane_tiles, axis=1)
    q_scale = MLA_QK ** -0.5 * LOG2E
    q_nope = _dot(hq, wq_nope_ref[...]) * q_scale
    q_pe = (_dot(hq, wq_rope_ref[...]) * cos + _dot(hq, wq_rot_ref[...]) * sin) * q_scale
    kv = _dot(hkv, wkv_ref[...])
    kr = kr_ref[0].astype(F32)
    k_pe = (kr[:, :MLA_ROPE] * cos[:, :MLA_ROPE] + kr[:, MLA_ROPE:] * sin[:, :MLA_ROPE]).astype(BF16)
    pad_rows = (MLA_VT_ROWS - MLA_V, cq_ref.shape[1])
    ones_row = jnp.where(lax.broadcasted_iota(jnp.int32, pad_rows, 0) == 0, 1.0, 0.0).astype(BF16)
    for h in range(MLA_HEADS):
        q_out[0, h, :, :MLA_NOPE] = q_nope[:, h * MLA_NOPE:(h + 1) * MLA_NOPE].astype(BF16)
        q_out[0, h, :, MLA_NOPE:] = q_pe[:, h * MLA_ROPE:(h + 1) * MLA_ROPE].astype(BF16)
        base = h * (MLA_NOPE + MLA_V)
        k_out[0, h, :, :MLA_NOPE] = kv[:, base:base + MLA_NOPE].astype(BF16)
        k_out[0, h, :, MLA_NOPE:] = k_pe
        vt_out[0, h, :MLA_V] = kv[:, base + MLA_NOPE:base + MLA_NOPE + MLA_V].T.astype(BF16)
        vt_out[0, h, MLA_V:] = ones_row


def _reduce_rows(op, x, groups=8):
    rows, n = x.shape
    partial = op(x.reshape(groups, rows // groups, n), axis=0)
    return op(partial, axis=0, keepdims=True)


def _mla_attn_body(q_ref, k_ref, vt_ref, o_ref, s_ref, *, tk):
    q = q_ref[0, 0]
    n_chunks = k_ref.shape[2] // tk

    n_slots = s_ref.shape[0]
    chunk_max = [None] * n_chunks

    def scores(c):
        s = _dot_nt(k_ref[0, 0, c * tk:(c + 1) * tk, :], q)
        s_ref[c % n_slots] = s
        chunk_max[c] = _reduce_rows(jnp.max, s)

    m = acc = None
    for c in range(min(n_slots - 1, n_chunks)):
        scores(c)
    for c in range(n_chunks):
        if c + n_slots - 1 < n_chunks:
            scores(c + n_slots - 1)
        m_new = chunk_max[c] if m is None else jnp.maximum(m, chunk_max[c])
        p = jnp.exp2(s_ref[c % n_slots] - m_new).astype(BF16)
        pv = _dot(vt_ref[0, 0, :, c * tk:(c + 1) * tk], p)
        acc = pv if acc is None else jnp.exp2(m - m_new) * acc + pv
        m = m_new
    o_ref[0] = (acc[:MLA_V] / acc[MLA_V:MLA_V + 1]).T.astype(o_ref.dtype)


def _mla(proj, q_norm, kv_norm, wq_nope, wq_rope, wq_rot, wkv, cos_t, sin_t, ts, tq, tk):
    b, s, _ = proj.shape
    rope_w = MLA_HEADS * MLA_ROPE
    q, k, vt = pl.pallas_call(
        _mla_prep_body,
        out_shape=(jax.ShapeDtypeStruct((b, MLA_HEADS, s, MLA_QK), BF16),
                   jax.ShapeDtypeStruct((b, MLA_HEADS, s, MLA_QK), BF16),
                   jax.ShapeDtypeStruct((b, MLA_HEADS, MLA_VT_ROWS, s), BF16)),
        grid=(b, s // ts),
        in_specs=[
            pl.BlockSpec((1, ts, MLA_RANK), lambda bi, si: (bi, si, IN_CQ // MLA_RANK)),
            pl.BlockSpec((1, ts, MLA_RANK), lambda bi, si: (bi, si, IN_CKV // MLA_RANK)),
            pl.BlockSpec((1, ts, LANES), lambda bi, si: (bi, si, IN_KR // LANES)),
            _resident((1, MLA_RANK)), _resident((1, MLA_RANK)),
            _resident((MLA_RANK, MLA_HEADS * MLA_NOPE)), _resident((MLA_RANK, rope_w)),
            _resident((MLA_RANK, rope_w)), _resident((MLA_RANK, MLA_HEADS * (MLA_NOPE + MLA_V))),
            pl.BlockSpec((ts, LANES), lambda bi, si: (si, 0)),
            pl.BlockSpec((ts, LANES), lambda bi, si: (si, 0)),
        ],
        out_specs=(pl.BlockSpec((1, MLA_HEADS, ts, MLA_QK), lambda bi, si: (bi, 0, si, 0)),
                   pl.BlockSpec((1, MLA_HEADS, ts, MLA_QK), lambda bi, si: (bi, 0, si, 0)),
                   pl.BlockSpec((1, MLA_HEADS, MLA_VT_ROWS, ts), lambda bi, si: (bi, 0, 0, si))),
        compiler_params=_cparams("parallel", "parallel"),
        name="mla_prep",
    )(proj, proj, proj, q_norm, kv_norm, wq_nope, wq_rope, wq_rot, wkv, cos_t, sin_t)
    return pl.pallas_call(
        functools.partial(_mla_attn_body, tk=tk),
        out_shape=jax.ShapeDtypeStruct((b, s, MLA_WIDTH), BF16),
        grid=(b, MLA_HEADS, s // tq),
        in_specs=[
            pl.BlockSpec((1, 1, tq, MLA_QK), lambda bi, hi, qi: (bi, hi, qi, 0)),
            pl.BlockSpec((1, 1, s, MLA_QK), lambda bi, hi, qi: (bi, hi, 0, 0)),
            pl.BlockSpec((1, 1, MLA_VT_ROWS, s), lambda bi, hi, qi: (bi, hi, 0, 0)),
        ],
        out_specs=pl.BlockSpec((1, tq, MLA_V), lambda bi, hi, qi: (bi, qi, hi)),
        scratch_shapes=[pltpu.VMEM((3, tk, tq), F32)],
        compiler_params=_cparams("parallel", "parallel", "parallel"),
        name="mla_attn",
    )(q, k, vt)


def _residual_norm_store(x, o_ref, sumsq, g):
    scale = lax.rsqrt(sumsq * (1.0 / o_ref.shape[-1]) + EPS)
    o_ref[...] = x + o_ref[...] * scale * g


def _project_rows(o_ref, lhs_and_weights):
    n = o_ref.shape[-1]
    nc = _col_chunk(n)
    sumsq = jnp.zeros((o_ref.shape[0], 1), F32)
    for c in range(0, n, nc):
        y = sum(_dot(a, w_ref[:, c:c + nc]) for a, w_ref in lhs_and_weights)
        sumsq = sumsq + jnp.sum(y * y, axis=-1, keepdims=True)
        o_ref[:, c:c + nc] = y
    return sumsq


def _mix_out_body(a1_ref, a2_ref, w1_ref, w2_ref, x_ref, g_ref, o_ref):
    sumsq = _project_rows(o_ref, [(a1_ref[...], w1_ref), (a2_ref[...], w2_ref)])
    _residual_norm_store(x_ref[...], o_ref, sumsq, g_ref[...])


def _mix_out(o_gla, o_mla, w1, w2, x, g, tm):
    t, d = x.shape
    return pl.pallas_call(
        _mix_out_body,
        out_shape=jax.ShapeDtypeStruct((t, d), F32),
        grid=(t // tm,),
        in_specs=[
            pl.BlockSpec((tm, GLA_WIDTH), lambda i: (i, 0)),
            pl.BlockSpec((tm, MLA_WIDTH), lambda i: (i, 0)),
            _resident((GLA_WIDTH, d)), _resident((MLA_WIDTH, d)),
            pl.BlockSpec((tm, d), lambda i: (i, 0)),
            _resident((1, d)),
        ],
        out_specs=pl.BlockSpec((tm, d), lambda i: (i, 0)),
        compiler_params=_cparams("parallel"),
        name="mix_out",
    )(o_gla, o_mla, w1, w2, x, g)


def _mem_attn_body(x_ref, gpre_ref, wq_ref, kv_ref, wo_ref, gpost_ref, o_ref, a_ref, q_ref):
    q_scale = MEM_HEAD_DIM ** -0.5 * LOG2E
    x = x_ref[0]
    h = _rms(x, gpre_ref[...]).astype(BF16)
    head_cols = [slice(hd * MEM_HEAD_DIM, (hd + 1) * MEM_HEAD_DIM) for hd in range(MEM_HEADS)]

    def project_q(hd):
        q_ref[hd % 2] = (_dot(h, wq_ref[:, head_cols[hd]]) * q_scale).astype(BF16)

    project_q(0)
    for hd in range(MEM_HEADS):
        if hd + 1 < MEM_HEADS:
            project_q(hd + 1)
        cols = head_cols[hd]
        vcols = slice(D_MODEL + hd * MEM_HEAD_DIM, D_MODEL + (hd + 1) * MEM_HEAD_DIM)
        s = _dot_nt(q_ref[hd % 2], kv_ref[0, :, cols])
        m = jnp.max(s, axis=-1, keepdims=True)
        p = jnp.exp2(s - m)
        l = jnp.sum(p, axis=-1, keepdims=True)
        a_ref[:, cols] = (_dot(p.astype(BF16), kv_ref[0, :, vcols]) / l).astype(BF16)
    sumsq = _project_rows(o_ref.at[0], [(a_ref[...], wo_ref)])
    _residual_norm_store(x, o_ref.at[0], sumsq, gpost_ref[...])


def _mem_attn(x, gpre, wq, kv, wo, gpost, tm):
    b, s, d = x.shape
    m = kv.shape[1]
    return pl.pallas_call(
        _mem_attn_body,
        out_shape=jax.ShapeDtypeStruct((b, s, d), F32),
        grid=(b, s // tm),
        in_specs=[
            pl.BlockSpec((1, tm, d), lambda bi, si: (bi, si, 0)),
            _resident((1, d)), _resident((d, d)),
            pl.BlockSpec((1, m, 2 * d), lambda bi, si: (bi, 0, 0)),
            _resident((d, d)), _resident((1, d)),
        ],
        out_specs=pl.BlockSpec((1, tm, d), lambda bi, si: (bi, si, 0)),
        scratch_shapes=[pltpu.VMEM((tm, d), BF16), pltpu.VMEM((2, tm, MEM_HEAD_DIM), BF16)],
        compiler_params=_cparams("parallel", "parallel"),
        name="mem_attn",
    )(x, gpre, wq, kv, wo, gpost)


def _gelu_tanh_times(x, v_half):
    c = math.sqrt(2.0 / math.pi)
    inner = x * (c + (c * 0.044715) * (x * x))
    return x * (1.0 + jnp.tanh(inner)) * v_half


def _ffn_body(x_ref, xprev_ref, xnext_ref, gpre_ref, wg_ref, wv_ref, cwg_ref, cwv_ref, cbg_ref, cbv_ref,
              wd_ref, gpost_ref, o_ref, h_ref, u_ref):
    si = pl.program_id(1)
    j = pl.program_id(2)
    tm = x_ref.shape[1]
    sub = FFN_SUB
    hl = SUBLANES
    n_lane_tiles = 2 * sub // LANES

    @pl.when(j == 0)
    def _():
        g = gpre_ref[...]
        h_ref[:tm] = _rms(x_ref[0], g).astype(BF16)
        nxt = jnp.where(si == pl.num_programs(1) - 1, 0.0, _rms(xnext_ref[0], g))
        prv = jnp.where(si == 0, 0.0, _rms(xprev_ref[0], g))
        h_ref[tm:] = jnp.concatenate([nxt, prv], axis=0).astype(BF16)
        o_ref[...] = jnp.zeros_like(o_ref)

    h = h_ref[...]
    n_sub = wd_ref.shape[0] // sub
    cols = [slice(c * sub, (c + 1) * sub) for c in range(n_sub)]

    def gate_value(g_ref, v_ref, c):
        return jnp.concatenate([g_ref[:, cols[c]], v_ref[:, cols[c]]], axis=1)

    def up_project(c):
        u = _dot(h, gate_value(wg_ref, wv_ref, c))
        for k in range(n_lane_tiles):
            lanes = slice(k * LANES, (k + 1) * LANES)
            u_ref[c % 2, k, hl:hl + tm] = u[:tm, lanes]
            u_ref[c % 2, k, :hl] = u[tm + hl:, lanes]
            u_ref[c % 2, k, hl + tm:] = u[tm:tm + hl, lanes]

    def window(c, offset):
        return jnp.concatenate([u_ref[c % 2, k, hl + offset:hl + offset + tm] for k in range(n_lane_tiles)],
                               axis=1)

    def gate_and_down(c):
        cw = gate_value(cwg_ref, cwv_ref, c)
        cu = (window(c, -1) * cw[0:1] + window(c, 0) * cw[1:2] + window(c, 1) * cw[2:3]
              + gate_value(cbg_ref, cbv_ref, c))
        act = _gelu_tanh_times(cu[:, :sub], cu[:, sub:]).astype(BF16)
        n = o_ref.shape[-1]
        nc = _col_chunk(n)
        for k in range(0, n, nc):
            o_ref[0, :, k:k + nc] += _dot(act, wd_ref[cols[c], k:k + nc])

    up_project(0)
    for c in range(n_sub):
        if c + 1 < n_sub:
            up_project(c + 1)
        gate_and_down(c)

    @pl.when(j == pl.num_programs(2) - 1)
    def _():
        o_ref[0] = x_ref[0] + _rms(o_ref[0], gpost_ref[...])


def _ffn(x, gpre, w_up, conv_w, conv_b, w_down, gpost, tm, fc):
    b, s, d = x.shape
    nf = D_FF // fc
    halo = SUBLANES
    tb = tm // halo
    last_blk = s // halo - 1
    return pl.pallas_call(
        _ffn_body,
        out_shape=jax.ShapeDtypeStruct((b, s, d), F32),
        grid=(b, s // tm, nf),
        in_specs=[
            pl.BlockSpec((1, tm, d), lambda bi, si, j: (bi, si, 0)),
            pl.BlockSpec((1, halo, d), lambda bi, si, j: (bi, jnp.maximum(si * tb - 1, 0), 0)),
            pl.BlockSpec((1, halo, d), lambda bi, si, j: (bi, jnp.minimum((si + 1) * tb, last_blk), 0)),
            _resident((1, d)),
            pl.BlockSpec((d, fc), lambda bi, si, j: (0, j)),
            pl.BlockSpec((d, fc), lambda bi, si, j: (0, nf + j)),
            pl.BlockSpec((CONV_WIDTH, fc), lambda bi, si, j: (0, j)),
            pl.BlockSpec((CONV_WIDTH, fc), lambda bi, si, j: (0, nf + j)),
            pl.BlockSpec((1, fc), lambda bi, si, j: (0, j)),
            pl.BlockSpec((1, fc), lambda bi, si, j: (0, nf + j)),
            pl.BlockSpec((fc, d), lambda bi, si, j: (j, 0)),
            _resident((1, d)),
        ],
        out_specs=pl.BlockSpec((1, tm, d), lambda bi, si, j: (bi, si, 0)),
        scratch_shapes=[pltpu.VMEM((tm + 2 * halo, d), BF16),
                        pltpu.VMEM((2, 2 * FFN_SUB // LANES, tm + 2 * halo, LANES), F32)],
        compiler_params=_cparams("parallel", "parallel", "arbitrary"),
        name="conv_ffn",
    )(x, x, x, gpre, w_up, w_up, conv_w, conv_w, conv_b, conv_b, w_down, gpost)


def _rotate_half_columns(w):
    k, n = w.shape
    g = w.reshape(k, n // MLA_ROPE, 2, MLA_ROPE // 2)
    return jnp.concatenate([-g[:, :, 1:], g[:, :, :1]], axis=2).reshape(k, n)


def _prepare_weights(p, seq):
    row = lambda v: v.reshape(1, -1).astype(F32)
    w_in = p["w_in"]
    offs, acc = [], 0
    for wdt in (GLA_QK, GLA_QK, GLA_WIDTH, GLA_WIDTH, GLA_GATE_RANK, GLA_GATE_RANK, MLA_RANK, MLA_RANK, MLA_ROPE):
        offs.append((acc, acc + wdt))
        acc += wdt
    part = [w_in[:, a:b] for a, b in offs]
    gq, gk, gv, gr, ggf, ggb, cq, ckv, kr = part
    d = w_in.shape[0]
    gate_pad = jnp.zeros((d, LANES - 2 * GLA_GATE_RANK), w_in.dtype)
    w_in_p = jnp.concatenate([gq, gk, gv, gr, cq, ckv, ggf, ggb, gate_pad, kr, _rotate_half_columns(kr)],
                             axis=1).astype(BF16)

    def gate_w2(w2, slot):
        full = jnp.zeros((LANES, GLA_QK), F32)
        return full.at[slot * GLA_GATE_RANK:(slot + 1) * GLA_GATE_RANK].set(w2).astype(BF16)

    wq = p["mla_w_q_up"].reshape(MLA_RANK, MLA_HEADS, MLA_QK)
    wq_nope = wq[:, :, :MLA_NOPE].reshape(MLA_RANK, MLA_HEADS * MLA_NOPE)
    wq_rope = wq[:, :, MLA_NOPE:].reshape(MLA_RANK, MLA_HEADS * MLA_ROPE)

    half = MLA_ROPE // 2
    freqs = ROPE_BASE ** (-jnp.arange(half, dtype=F32) / half)
    ang = jnp.arange(seq, dtype=F32)[:, None] * freqs[None, :]
    cos_t = jnp.tile(jnp.cos(ang), (1, LANES // half))
    sin_t = jnp.tile(jnp.sin(ang), (1, LANES // half))

    value_half = jnp.concatenate([jnp.ones((1, D_FF), F32), jnp.full((1, D_FF), 0.5, F32)], axis=1)

    return dict(
        g_mix_pre=row(p["norm_mix_pre"]), g_mix_post=row(p["norm_mix_post"]), w_in=w_in_p,
        w2f=gate_w2(p["gla_gate_w2_fwd"], 0), b2f=row(p["gla_gate_b_fwd"]),
        w2b=gate_w2(p["gla_gate_w2_bwd"], 1), b2b=row(p["gla_gate_b_bwd"]),
        gla_out_norm=row(p["gla_out_norm"]),
        mla_q_norm=row(p["mla_q_norm"]), mla_kv_norm=row(p["mla_kv_norm"]),
        wq_nope=wq_nope.astype(BF16), wq_rope=wq_rope.astype(BF16),
        wq_rot=_rotate_half_columns(wq_rope).astype(BF16), wkv=p["mla_w_kv_up"].astype(BF16),
        cos=cos_t, sin=sin_t,
        w_out_gla=p["w_out"][:GLA_WIDTH].astype(BF16), w_out_mla=p["w_out"][GLA_WIDTH:].astype(BF16),
        g_mem_pre=row(p["norm_mem_pre"]), g_mem_post=row(p["norm_mem_post"]), g_mem_kv=row(p["mem_kv_norm"]),
        w_mem_q=p["w_mem_q"].astype(BF16), w_mem_o=p["w_mem_o"].astype(BF16),
        w_mem_kv=jnp.concatenate([p["w_mem_k"], p["w_mem_v"]], axis=1).astype(BF16),
        g_ffn_pre=row(p["norm_ffn_pre"]), g_ffn_post=row(p["norm_ffn_post"]),
        w_ffn_up=p["w_ffn_up"].astype(BF16),
        ffn_conv_w=p["ffn_conv_w"].astype(F32) * value_half,
        ffn_conv_b=row(p["ffn_conv_b"]) * value_half,
        w_ffn_down=p["w_ffn_down"].astype(BF16),
    )


def _tiles(seq):
    tm = min(512, seq)
    return dict(tm=tm, gla_ts=min(1024, seq), mla_ts=min(1024, seq), mla_tq=min(1024, seq), mla_tk=tm, ffn_tm=tm, ffn_fc=512)


def _apply_layer(x, mem, w):
    b, s, d = x.shape
    t = _tiles(s)
    tm = t["tm"]
    x2 = x.reshape(b * s, d)
    proj = _norm_matmul(x2, w["g_mix_pre"], w["w_in"], tm).reshape(b, s, IN_WIDTH_PADDED)
    o_gla = _gla(proj, w["w2f"], w["b2f"], w["w2b"], w["b2b"], w["gla_out_norm"], t["gla_ts"])
    o_mla = _mla(proj, w["mla_q_norm"], w["mla_kv_norm"], w["wq_nope"], w["wq_rope"], w["wq_rot"], w["wkv"],
                 w["cos"], w["sin"], t["mla_ts"], t["mla_tq"], t["mla_tk"])
    x2 = _mix_out(o_gla.reshape(b * s, GLA_WIDTH), o_mla.reshape(b * s, MLA_WIDTH),
                  w["w_out_gla"], w["w_out_mla"], x2, w["g_mix_post"], tm)
    mtok = mem.shape[1]
    kv = _norm_matmul(mem.reshape(b * mtok, d), w["g_mem_kv"], w["w_mem_kv"], min(tm, b * mtok))
    x3 = _mem_attn(x2.reshape(b, s, d), w["g_mem_pre"], w["w_mem_q"], kv.reshape(b, mtok, 2 * d),
                   w["w_mem_o"], w["g_mem_post"], tm)
    return _ffn(x3, w["g_ffn_pre"], w["w_ffn_up"], w["ffn_conv_w"], w["ffn_conv_b"], w["w_ffn_down"],
                w["g_ffn_post"], t["ffn_tm"], t["ffn_fc"])


def kernel(x_prompt, x_sample, mem_prompt, mem_sample, norm_mix_pre, norm_mix_post, w_in, gla_gate_w2_fwd, gla_gate_b_fwd, gla_gate_w2_bwd, gla_gate_b_bwd, gla_out_norm, mla_q_norm, mla_w_q_up, mla_kv_norm, mla_w_kv_up, w_out, norm_mem_pre, norm_mem_post, mem_kv_norm, w_mem_q, w_mem_k, w_mem_v, w_mem_o, norm_ffn_pre, norm_ffn_post, w_ffn_up, ffn_conv_w, ffn_conv_b, w_ffn_down):
    params = dict(
        norm_mix_pre=norm_mix_pre, norm_mix_post=norm_mix_post, w_in=w_in,
        gla_gate_w2_fwd=gla_gate_w2_fwd, gla_gate_b_fwd=gla_gate_b_fwd,
        gla_gate_w2_bwd=gla_gate_w2_bwd, gla_gate_b_bwd=gla_gate_b_bwd, gla_out_norm=gla_out_norm,
        mla_q_norm=mla_q_norm, mla_w_q_up=mla_w_q_up, mla_kv_norm=mla_kv_norm, mla_w_kv_up=mla_w_kv_up,
        w_out=w_out, norm_mem_pre=norm_mem_pre, norm_mem_post=norm_mem_post, mem_kv_norm=mem_kv_norm,
        w_mem_q=w_mem_q, w_mem_k=w_mem_k, w_mem_v=w_mem_v, w_mem_o=w_mem_o,
        norm_ffn_pre=norm_ffn_pre, norm_ffn_post=norm_ffn_post, w_ffn_up=w_ffn_up,
        ffn_conv_w=ffn_conv_w, ffn_conv_b=ffn_conv_b, w_ffn_down=w_ffn_down,
    )
    depth = w_in.shape[0]
    yp, ys = x_prompt, x_sample
    for layer in range(depth):
        p = {k: v[layer] for k, v in params.items()}
        assert yp.shape[1] == ys.shape[1]
        w = _prepare_weights(p, yp.shape[1])
        yp = _apply_layer(yp, mem_prompt, w)
        ys = _apply_layer(ys, mem_sample, w)
    return (yp, ys)
```
